```python
import jax, jax.numpy as jnp
from jax import lax
import numpy as np

D_MODEL = 4096
BATCH = 4
SEQ = 4096
DEPTH = 2
DEC_BATCH = 2
DEC_SEQ = 4096
PAST_LEN = 128

GRID_W = 64
PLE_DIM = 256
NA_HEADS = 16
NA_HEAD_DIM = 128
NA_WIDTH = NA_HEADS * NA_HEAD_DIM
NA_WIN_ROWS = 8
NA_WIN_COLS = 16
RET_HEADS = 8
RET_QK_DIM = 128
RET_V_DIM = 256
RET_QK_WIDTH = RET_HEADS * RET_QK_DIM
RET_V_WIDTH = RET_HEADS * RET_V_DIM
RET_CHUNK = 128
ROPE_BASE = 10000.0
EPS = 1e-6
SPLITS = (NA_WIDTH, NA_WIDTH, NA_WIDTH, NA_WIDTH,
          RET_QK_WIDTH, RET_QK_WIDTH, RET_V_WIDTH, RET_V_WIDTH,
          D_MODEL, D_MODEL)
IN_WIDTH = sum(SPLITS)

kernel_name = 'hybrid_natten_retention_encoder'


def rmsnorm(x, g):
    xf = x.astype(jnp.float32)
    y = xf * lax.rsqrt(jnp.mean(xf * xf, axis=-1, keepdims=True) + EPS)
    return (y * g.astype(jnp.float32)).astype(x.dtype)


def rotary(t, positions):
    half = t.shape[-1] // 2
    inv = ROPE_BASE ** (-jnp.arange(half, dtype=jnp.float32) / half)
    ang = positions.astype(jnp.float32)[:, None] * inv[None, :]
    cos = jnp.cos(ang)[None, :, None, :]
    sin = jnp.sin(ang)[None, :, None, :]
    t1 = t[..., :half].astype(jnp.float32)
    t2 = t[..., half:].astype(jnp.float32)
    return jnp.concatenate([t1 * cos - t2 * sin, t1 * sin + t2 * cos], axis=-1).astype(t.dtype)


def neighborhood_attention(q, k, v, rpb):
    B, S, _ = q.shape
    rows = S // GRID_W
    wr = min(NA_WIN_ROWS, rows)
    grid = lambda t: t.reshape(B, rows, GRID_W, NA_HEADS, NA_HEAD_DIM)
    qg, kg, vg = grid(q), grid(k), grid(v)
    cols = jnp.arange(GRID_W)
    col_start = jnp.clip(cols - NA_WIN_COLS // 2, 0, GRID_W - NA_WIN_COLS)
    col_valid = (cols[None, :] >= col_start[:, None]) & (cols[None, :] < col_start[:, None] + NA_WIN_COLS)
    col_idx = jnp.clip(cols[None, :] - cols[:, None] + NA_WIN_COLS - 1, 0, 2 * NA_WIN_COLS - 2)
    scale = NA_HEAD_DIM ** -0.5

    def one_row(r):
        rs = jnp.clip(r - wr // 2, 0, rows - wr)
        q_r = lax.dynamic_index_in_dim(qg, r, axis=1, keepdims=False)
        k_r = lax.dynamic_slice_in_dim(kg, rs, wr, axis=1)
        v_r = lax.dynamic_slice_in_dim(vg, rs, wr, axis=1)
        s = jnp.einsum('bqhd,bwkhd->bhqwk', q_r, k_r, preferred_element_type=jnp.float32) * scale
        row_idx = rs + jnp.arange(wr) - r + NA_WIN_ROWS - 1
        bias = rpb[:, row_idx][:, :, col_idx].transpose(0, 2, 1, 3)
        s = s + bias.astype(jnp.float32)[None]
        s = jnp.where(col_valid[None, None, :, None, :], s, -jnp.inf)
        p = jax.nn.softmax(s, axis=(-2, -1))
        return jnp.einsum('bhqwk,bwkhd->bqhd', p.astype(v_r.dtype), v_r)

    out = lax.map(one_row, jnp.arange(rows))
    return out.transpose(1, 0, 2, 3, 4).reshape(B, S, NA_WIDTH)


def chunkwise_retention(q, k, v, log_decay):
    B, H, N, C, dk = q.shape
    dv = v.shape[-1]
    pos = jnp.arange(C, dtype=jnp.float32)
    ld = log_decay[:, None]
    diff = pos[:, None] - pos[None, :]
    decay_mat = jnp.where(diff[None] >= 0, jnp.exp(ld[:, :, None] * jnp.maximum(diff, 0.0)[None]), 0.0)
    scores = jnp.einsum('bhncd,bhnsd->bhncs', q, k) * decay_mat[None, :, None]
    inner = jnp.einsum('bhncs,bhnse->bhnce', scores, v)
    q_decay = jnp.exp(ld * (pos[None, :] + 1.0))
    k_decay = jnp.exp(ld * (C - 1.0 - pos[None, :]))
    chunk_decay = jnp.exp(log_decay * C)
    kv = jnp.einsum('bhncd,hc,bhnce->bhnde', k, k_decay, v)

    def step(state, kv_n):
        return state * chunk_decay[None, :, None, None] + kv_n, state

    init = jnp.zeros((B, H, dk, dv), jnp.float32)
    _, prev = lax.scan(step, init, jnp.moveaxis(kv, 2, 0))
    prev = jnp.moveaxis(prev, 0, 2)
    cross = jnp.einsum('bhncd,bhnde->bhnce', q * q_decay[None, :, None, :, None], prev)
    return inner + cross


def retention_branch(q, k, v, log_decay_fwd, log_decay_bwd, gn_gain):
    B, S, _ = q.shape
    n_chunks = S // RET_CHUNK
    pos = jnp.arange(S)
    q = rotary(q.reshape(B, S, RET_HEADS, RET_QK_DIM), pos)
    k = rotary(k.reshape(B, S, RET_HEADS, RET_QK_DIM), pos) * (RET_QK_DIM ** -0.5)
    v4 = v.reshape(B, S, RET_HEADS, RET_V_DIM)
    chunk = lambda t: t.astype(jnp.float32).reshape(B, n_chunks, RET_CHUNK, RET_HEADS, -1).transpose(0, 3, 1, 2, 4)
    unchunk = lambda t: t.transpose(0, 2, 3, 1, 4).reshape(B, S, RET_HEADS, RET_V_DIM)
    flip = lambda t: jnp.flip(t, axis=1)
    ld_f = -jnp.abs(log_decay_fwd.astype(jnp.float32))
    ld_b = -jnp.abs(log_decay_bwd.astype(jnp.float32))
    fwd = unchunk(chunkwise_retention(chunk(q), chunk(k), chunk(v4), ld_f))
    bwd = flip(unchunk(chunkwise_retention(chunk(flip(q)), chunk(flip(k)), chunk(flip(v4)), ld_b)))
    o = fwd + bwd
    o = o * lax.rsqrt(jnp.mean(o * o, axis=-1, keepdims=True) + EPS)
    o = o * gn_gain.astype(jnp.float32).reshape(RET_HEADS, RET_V_DIM)
    return o.reshape(B, S, RET_V_WIDTH).astype(v.dtype)


def encoder_layer(x, p_i, w_in, ln_pre, ln_post, na_rpb, ret_ld_f, ret_ld_b, ret_gn,
                  w_proj_a, w_proj_b, w_out, w_ple, w_ple_gate):
    xn = rmsnorm(x, ln_pre)
    proj = xn @ w_in
    offsets = [int(o) for o in np.cumsum(SPLITS)[:-1]]
    na_q, na_k, na_v, na_g, r_q, r_k, r_v, r_g, gate_a, gate_b = jnp.split(proj, offsets, axis=-1)
    a = neighborhood_attention(na_q, na_k, na_v, na_rpb) * jax.nn.silu(na_g)
    b = retention_branch(r_q, r_k, r_v, ret_ld_f, ret_ld_b, ret_gn) * jax.nn.silu(r_g)
    merged = jax.nn.sigmoid(gate_a) * (a @ w_proj_a) + jax.nn.sigmoid(gate_b) * (b @ w_proj_b)
    x = x + rmsnorm(merged @ w_out, ln_post)
    x = x + jax.nn.sigmoid(x @ w_ple_gate) * (p_i @ w_ple)
    return x


def trunk(x, p, w_in, ln_pre, ln_post, na_rpb, ret_log_decay_fwd, ret_log_decay_bwd, ret_gn_gain,
          w_proj_a, w_proj_b, w_out, w_ple, w_ple_gate):
    for i in range(DEPTH):
        x = encoder_layer(x, p[i], w_in[i], ln_pre[i], ln_post[i], na_rpb[i],
                          ret_log_decay_fwd[i], ret_log_decay_bwd[i], ret_gn_gain[i],
                          w_proj_a[i], w_proj_b[i], w_out[i], w_ple[i], w_ple_gate[i])
    return x


def setup_inputs(seed: int = 0) -> dict:
    key = jax.random.key(seed)
    ks = jax.random.split(key, 17)
    f32 = jnp.float32
    nrm = lambda k, shape, s: jax.random.normal(k, shape, f32) * s
    base_ld = jnp.asarray(np.log(1.0 - 2.0 ** (-5.0 - np.arange(RET_HEADS))), f32)
    return {
        'x_prompt': nrm(ks[0], (BATCH, SEQ, D_MODEL), 1.0),
        'x_sample': nrm(ks[1], (DEC_BATCH, DEC_SEQ, D_MODEL), 1.0),
        'p_prompt': nrm(ks[2], (DEPTH, BATCH, SEQ, PLE_DIM), 1.0),
        'p_sample': nrm(ks[3], (DEPTH, DEC_BATCH, DEC_SEQ, PLE_DIM), 1.0),
        'w_in': nrm(ks[4], (DEPTH, D_MODEL, IN_WIDTH), D_MODEL ** -0.5),
        'ln_pre': 1.0 + nrm(ks[5], (DEPTH, D_MODEL), 0.02),
        'ln_post': 1.0 + nrm(ks[6], (DEPTH, D_MODEL), 0.02),
        'na_rpb': nrm(ks[7], (DEPTH, NA_HEADS, 2 * NA_WIN_ROWS - 1, 2 * NA_WIN_COLS - 1), 0.1),
        'ret_log_decay_fwd': base_ld[None] * (1.0 + nrm(ks[8], (DEPTH, RET_HEADS), 0.05)),
        'ret_log_decay_bwd': base_ld[None] * (1.0 + nrm(ks[9], (DEPTH, RET_HEADS), 0.05)),
        'ret_gn_gain': 1.0 + nrm(ks[10], (DEPTH, RET_V_WIDTH), 0.02),
        'w_proj_a': nrm(ks[11], (DEPTH, NA_WIDTH, D_MODEL), NA_WIDTH ** -0.5),
        'w_proj_b': nrm(ks[12], (DEPTH, RET_V_WIDTH, D_MODEL), RET_V_WIDTH ** -0.5),
        'w_out': nrm(ks[13], (DEPTH, D_MODEL, D_MODEL), D_MODEL ** -0.5),
        'w_ple': nrm(ks[14], (DEPTH, PLE_DIM, D_MODEL), PLE_DIM ** -0.5),
        'w_ple_gate': nrm(ks[15], (DEPTH, D_MODEL, D_MODEL), D_MODEL ** -0.5),
    }


def reference(x_prompt, x_sample, p_prompt, p_sample, w_in, ln_pre, ln_post, na_rpb,
              ret_log_decay_fwd, ret_log_decay_bwd, ret_gn_gain, w_proj_a, w_proj_b,
              w_out, w_ple, w_ple_gate):
    y_prompt = trunk(x_prompt, p_prompt, w_in, ln_pre, ln_post, na_rpb, ret_log_decay_fwd,
                     ret_log_decay_bwd, ret_gn_gain, w_proj_a, w_proj_b, w_out, w_ple, w_ple_gate)
    y_sample = trunk(x_sample, p_sample, w_in, ln_pre, ln_post, na_rpb, ret_log_decay_fwd,
                     ret_log_decay_bwd, ret_gn_gain, w_proj_a, w_proj_b, w_out, w_ple, w_ple_gate)
    return (y_prompt, y_sample)
```

```python
import functools

import numpy as np
import jax
import jax.numpy as jnp
from jax import lax
from jax.experimental import pallas as pl
from jax.experimental.pallas import tpu as pltpu

D_MODEL = 4096
GRID_W = 64
PLE_DIM = 256
NA_HEADS = 16
NA_HEAD_DIM = 128
NA_WIDTH = NA_HEADS * NA_HEAD_DIM
NA_WIN_ROWS = 8
NA_WIN_COLS = 16
RET_HEADS = 8
RET_QK_DIM = 128
RET_V_DIM = 256
RET_QK_WIDTH = RET_HEADS * RET_QK_DIM
RET_V_WIDTH = RET_HEADS * RET_V_DIM
RET_CHUNK = 128
ROPE_BASE = 10000.0
EPS = 1e-6

OFF_NA_Q = 0
OFF_NA_K = OFF_NA_Q + NA_WIDTH
OFF_NA_V = OFF_NA_K + NA_WIDTH
OFF_NA_G = OFF_NA_V + NA_WIDTH
OFF_R_Q = OFF_NA_G + NA_WIDTH
OFF_R_K = OFF_R_Q + RET_QK_WIDTH
OFF_R_V = OFF_R_K + RET_QK_WIDTH
OFF_R_G = OFF_R_V + RET_V_WIDTH
OFF_GATE_A = OFF_R_G + RET_V_WIDTH
OFF_GATE_B = OFF_GATE_A + D_MODEL
IN_WIDTH = OFF_GATE_B + D_MODEL

TILE_N = 1024
MAX_TILE_M = 1024
ROW_TILE = 256
V7X_VMEM_LIMIT_BYTES = 56 * 1024 * 1024

F32 = jnp.float32
BF16 = jnp.bfloat16


def _params(semantics, vmem=V7X_VMEM_LIMIT_BYTES):
    return pltpu.CompilerParams(dimension_semantics=semantics, vmem_limit_bytes=vmem)


def _rmsnorm_kernel(x_ref, g_ref, o_ref):
    x = x_ref[...]
    ms = jnp.mean(x * x, axis=-1, keepdims=True)
    o_ref[...] = (x * lax.rsqrt(ms + EPS) * g_ref[...]).astype(o_ref.dtype)


def _rmsnorm_pre(x, g):
    t = x.shape[0]
    return pl.pallas_call(
        _rmsnorm_kernel,
        grid=(t // ROW_TILE,),
        in_specs=[pl.BlockSpec((ROW_TILE, D_MODEL), lambda i: (i, 0)),
                  pl.BlockSpec((1, D_MODEL), lambda i: (0, 0))],
        out_specs=pl.BlockSpec((ROW_TILE, D_MODEL), lambda i: (i, 0)),
        out_shape=jax.ShapeDtypeStruct((t, D_MODEL), BF16),
        compiler_params=_params(("arbitrary",)),
        name="rmsnorm_pre",
    )(x, g.reshape(1, D_MODEL))


_J_NA_K = OFF_NA_K // TILE_N
_J_NA_G = OFF_NA_G // TILE_N
_J_R_Q = OFF_R_Q // TILE_N
_J_R_K = OFF_R_K // TILE_N
_J_R_V = OFF_R_V // TILE_N
_J_R_G = OFF_R_G // TILE_N
_J_GATE = OFF_GATE_A // TILE_N


def _inproj_kernel(x_ref, w_ref, cos_ref, sin_ref, o_ref, acc_ref):
    j = pl.program_id(1)
    acc_ref[...] = jnp.dot(x_ref[...], w_ref[...], preferred_element_type=F32)

    @pl.when(j < _J_NA_K)
    def _():
        o_ref[...] = (acc_ref[...] * (NA_HEAD_DIM ** -0.5)).astype(o_ref.dtype)

    is_silu = ((j >= _J_NA_G) & (j < _J_R_Q)) | ((j >= _J_R_G) & (j < _J_GATE))

    @pl.when(is_silu)
    def _():
        y = acc_ref[...]
        o_ref[...] = (y * jax.nn.sigmoid(y)).astype(o_ref.dtype)

    @pl.when(j >= _J_GATE)
    def _():
        o_ref[...] = jax.nn.sigmoid(acc_ref[...]).astype(o_ref.dtype)

    is_plain = ((j >= _J_NA_K) & (j < _J_NA_G)) | ((j >= _J_R_V) & (j < _J_R_G))

    @pl.when(is_plain)
    def _():
        o_ref[...] = acc_ref[...].astype(o_ref.dtype)

    def rotary(scale):
        c = cos_ref[...]
        s = sin_ref[...]
        for h in range(TILE_N // RET_QK_DIM):
            sl = slice(h * RET_QK_DIM, (h + 1) * RET_QK_DIM)
            t = acc_ref[:, sl]
            r = pltpu.roll(t, RET_QK_DIM // 2, axis=1)
            y = t * c + r * s
            if scale is not None:
                y = y * scale
            o_ref[:, sl] = y.astype(o_ref.dtype)

    @pl.when(j == _J_R_Q)
    def _():
        rotary(None)

    @pl.when(j == _J_R_K)
    def _():
        rotary(RET_QK_DIM ** -0.5)


def _in_proj(xn, w, cos_t, sin_t, seq):
    t = xn.shape[0]
    tm = min(MAX_TILE_M, seq)
    pos_tiles = seq // tm
    return pl.pallas_call(
        _inproj_kernel,
        grid=(t // tm, IN_WIDTH // TILE_N),
        in_specs=[pl.BlockSpec((tm, D_MODEL), lambda i, j: (i, 0)),
                  pl.BlockSpec((D_MODEL, TILE_N), lambda i, j: (0, j)),
                  pl.BlockSpec((tm, RET_QK_DIM), lambda i, j: (i % pos_tiles, 0)),
                  pl.BlockSpec((tm, RET_QK_DIM), lambda i, j: (i % pos_tiles, 0))],
        out_specs=pl.BlockSpec((tm, TILE_N), lambda i, j: (i, j)),
        out_shape=jax.ShapeDtypeStruct((t, IN_WIDTH), BF16),
        scratch_shapes=[pltpu.VMEM((tm, TILE_N), F32)],
        compiler_params=_params(("arbitrary", "arbitrary")),
        name="in_proj",
    )(xn, w, cos_t, sin_t)


def _na_kernel(q_ref, k_ref, v_ref, g_ref, bias_ref, o_ref, *, rows):
    wr = NA_WIN_ROWS

    def body(r, carry):
        rs = jnp.clip(r - wr // 2, 0, rows - wr)
        var = rs - r + (NA_WIN_ROWS - 1)
        q0 = pl.multiple_of(r * GRID_W, GRID_W)
        k0 = pl.multiple_of(rs * GRID_W, GRID_W)
        q = q_ref[pl.ds(q0, GRID_W), :]
        kw = k_ref[pl.ds(k0, wr * GRID_W), :]
        vw = v_ref[pl.ds(k0, wr * GRID_W), :]
        s = lax.dot_general(q, kw, (((1,), (1,)), ((), ())), preferred_element_type=F32)
        s = s + bias_ref[var]
        m = jnp.max(s, axis=1, keepdims=True)
        p = jnp.exp(s - m)
        l = jnp.sum(p, axis=1, keepdims=True)
        o = jnp.dot(p.astype(BF16), vw, preferred_element_type=F32)
        o = o * (1.0 / l) * g_ref[pl.ds(q0, GRID_W), :].astype(F32)
        o_ref[pl.ds(q0, GRID_W), :] = o.astype(o_ref.dtype)
        return carry

    lax.fori_loop(0, rows, body, 0)


def _na_bias_table(rpb, rows):
    wr = min(NA_WIN_ROWS, rows)
    cols = np.arange(GRID_W)
    col_start = np.clip(cols - NA_WIN_COLS // 2, 0, GRID_W - NA_WIN_COLS)
    col_valid = (cols[None, :] >= col_start[:, None]) & (cols[None, :] < col_start[:, None] + NA_WIN_COLS)
    col_idx = np.clip(cols[None, :] - cols[:, None] + NA_WIN_COLS - 1, 0, 2 * NA_WIN_COLS - 2)
    row_idx = np.arange(NA_WIN_ROWS)[:, None] + np.arange(wr)[None, :]
    row_idx = np.minimum(row_idx, 2 * NA_WIN_ROWS - 2)
    b = rpb.astype(F32)[:, row_idx]
    b = b[:, :, :, col_idx]
    b = jnp.where(col_valid[None, None, None], b, -jnp.inf)
    b = b.transpose(0, 1, 3, 2, 4)
    return b.reshape(NA_HEADS, NA_WIN_ROWS, GRID_W, wr * GRID_W)


def _na_attention(proj, bias, batch, seq):
    rows = seq // GRID_W
    wr = min(NA_WIN_ROWS, rows)
    assert wr == NA_WIN_ROWS
    hd = NA_HEAD_DIM
    blk = lambda off: pl.BlockSpec((seq, hd), lambda b, h, off=off: (b, off // hd + h))
    return pl.pallas_call(
        functools.partial(_na_kernel, rows=rows),
        grid=(batch, NA_HEADS),
        in_specs=[blk(OFF_NA_Q), blk(OFF_NA_K), blk(OFF_NA_V), blk(OFF_NA_G),
                  pl.BlockSpec((None, NA_WIN_ROWS, GRID_W, wr * GRID_W), lambda b, h: (h, 0, 0, 0))],
        out_specs=pl.BlockSpec((seq, hd), lambda b, h: (b, h)),
        out_shape=jax.ShapeDtypeStruct((batch * seq, NA_WIDTH), BF16),
        compiler_params=_params(("arbitrary", "arbitrary")),
        name="na_attention",
    )(proj, proj, proj, proj, bias)


def _ret_kernel(ld_ref, q_ref, k_ref, v_ref, g_ref, gain_ref, o_ref,
                sb_ref, sf_ref, st_ref, dsum_ref, qdf_ref, qdb_ref, kdf_ref, kdb_ref, *, n_chunks):
    c_len = RET_CHUNK
    h = pl.program_id(1)
    ldf = -jnp.abs(ld_ref[0, h])
    ldb = -jnp.abs(ld_ref[1, h])
    row = lax.broadcasted_iota(jnp.int32, (c_len, c_len), 0).astype(F32)
    col = lax.broadcasted_iota(jnp.int32, (c_len, c_len), 1).astype(F32)
    diff = row - col
    dsum_ref[...] = (jnp.where(diff >= 0, jnp.exp(ldf * jnp.maximum(diff, 0.0)), 0.0)
                     + jnp.where(diff <= 0, jnp.exp(ldb * jnp.maximum(-diff, 0.0)), 0.0))
    qdf_ref[...] = jnp.exp(ldf * (row + 1.0))
    kdf_ref[...] = jnp.exp(ldf * (c_len - 1.0 - row))
    qdb_ref[...] = jnp.exp(ldb * (c_len - row))
    kdb_ref[...] = jnp.exp(ldb * row)
    zero_row = jnp.zeros((1, RET_V_DIM), F32)
    cdf = jnp.exp(zero_row + ldf * c_len)
    cdb = jnp.exp(zero_row + ldb * c_len)
    tdot = lambda a, b: lax.dot_general(a, b, (((0,), (0,)), ((), ())), preferred_element_type=F32)

    st_ref[...] = jnp.zeros_like(st_ref)

    def back(t, carry):
        c = n_chunks - 1 - t
        r0 = pl.multiple_of(c * c_len, c_len)
        sb_ref[c] = st_ref[...].astype(BF16)
        kd = (k_ref[pl.ds(r0, c_len), :].astype(F32) * kdb_ref[...]).astype(BF16)
        st_ref[...] = st_ref[...] * cdb + tdot(kd, v_ref[pl.ds(r0, c_len), :])
        return carry

    lax.fori_loop(0, n_chunks, back, 0)

    sf_ref[...] = jnp.zeros_like(sf_ref)

    def fwd(c, carry):
        r0 = pl.multiple_of(c * c_len, c_len)
        q = q_ref[pl.ds(r0, c_len), :]
        k = k_ref[pl.ds(r0, c_len), :]
        v = v_ref[pl.ds(r0, c_len), :]
        qf = q.astype(F32)
        sc = lax.dot_general(q, k, (((1,), (1,)), ((), ())), preferred_element_type=F32)
        o = jnp.dot((sc * dsum_ref[...]).astype(BF16), v, preferred_element_type=F32)
        o = o + jnp.dot((qf * qdf_ref[...]).astype(BF16), sf_ref[...].astype(BF16),
                        preferred_element_type=F32)
        o = o + jnp.dot((qf * qdb_ref[...]).astype(BF16), sb_ref[c], preferred_element_type=F32)
        o = o * lax.rsqrt(jnp.mean(o * o, axis=-1, keepdims=True) + EPS)
        o = o * gain_ref[...] * g_ref[pl.ds(r0, c_len), :].astype(F32)
        o_ref[pl.ds(r0, c_len), :] = o.astype(o_ref.dtype)
        kd = (k.astype(F32) * kdf_ref[...]).astype(BF16)
        sf_ref[...] = sf_ref[...] * cdf + tdot(kd, v)
        return carry

    lax.fori_loop(0, n_chunks, fwd, 0)


def _retention(proj, ld, gain, batch, seq):
    n_chunks = seq // RET_CHUNK
    dk, dv, c_len = RET_QK_DIM, RET_V_DIM, RET_CHUNK
    qk_blk = lambda off: pl.BlockSpec((seq, dk), lambda b, h, off=off: (b, off // dk + h))
    v_blk = lambda off: pl.BlockSpec((seq, dv), lambda b, h, off=off: (b, off // dv + h))
    return pl.pallas_call(
        functools.partial(_ret_kernel, n_chunks=n_chunks),
        grid=(batch, RET_HEADS),
        in_specs=[pl.BlockSpec(memory_space=pltpu.SMEM),
                  qk_blk(OFF_R_Q), qk_blk(OFF_R_K), v_blk(OFF_R_V), v_blk(OFF_R_G),
                  pl.BlockSpec((None, 1, dv), lambda b, h: (h, 0, 0))],
        out_specs=pl.BlockSpec((seq, dv), lambda b, h: (b, h)),
        out_shape=jax.ShapeDtypeStruct((batch * seq, RET_V_WIDTH), BF16),
        scratch_shapes=[pltpu.VMEM((n_chunks, dk, dv), BF16),
                        pltpu.VMEM((dk, dv), F32),
                        pltpu.VMEM((dk, dv), F32),
                        pltpu.VMEM((c_len, c_len), F32),
                        pltpu.VMEM((c_len, dk), F32),
                        pltpu.VMEM((c_len, dk), F32),
                        pltpu.VMEM((c_len, dk), F32),
                        pltpu.VMEM((c_len, dk), F32)],
        compiler_params=_params(("arbitrary", "arbitrary")),
        name="retention",
    )(ld, proj, proj, proj, proj, gain.astype(F32).reshape(RET_HEADS, 1, dv))


def _merge_kernel(a_ref, b_ref, wa_ref, wb_ref, ga_ref, gb_ref, o_ref):
    ya = jnp.dot(a_ref[...], wa_ref[...], preferred_element_type=F32)
    yb = jnp.dot(b_ref[...], wb_ref[...], preferred_element_type=F32)
    o_ref[...] = (ga_ref[...].astype(F32) * ya + gb_ref[...].astype(F32) * yb).astype(o_ref.dtype)


def _merge(a, b, wa, wb, proj):
    t = a.shape[0]
    tm = min(MAX_TILE_M, t)
    ja, jb = OFF_GATE_A // TILE_N, OFF_GATE_B // TILE_N
    return pl.pallas_call(
        _merge_kernel,
        grid=(t // tm, D_MODEL // TILE_N),
        in_specs=[pl.BlockSpec((tm, NA_WIDTH), lambda i, j: (i, 0)),
                  pl.BlockSpec((tm, RET_V_WIDTH), lambda i, j: (i, 0)),
                  pl.BlockSpec((NA_WIDTH, TILE_N), lambda i, j: (0, j)),
                  pl.BlockSpec((RET_V_WIDTH, TILE_N), lambda i, j: (0, j)),
                  pl.BlockSpec((tm, TILE_N), lambda i, j: (i, ja + j)),
                  pl.BlockSpec((tm, TILE_N), lambda i, j: (i, jb + j))],
        out_specs=pl.BlockSpec((tm, TILE_N), lambda i, j: (i, j)),
        out_shape=jax.ShapeDtypeStruct((t, D_MODEL), BF16),
        compiler_params=_params(("arbitrary", "arbitrary")),
        name="merge",
    )(a, b, wa, wb, proj, proj)


def _matmul_kernel(x_ref, w_ref, o_ref):
    o_ref[...] = jnp.dot(x_ref[...], w_ref[...], preferred_element_type=F32).astype(o_ref.dtype)


def _out_proj(merged, w):
    t = merged.shape[0]
    tm = min(MAX_TILE_M, t)
    return pl.pallas_call(
        _matmul_kernel,
        grid=(t // tm, D_MODEL // TILE_N),
        in_specs=[pl.BlockSpec((tm, D_MODEL), lambda i, j: (i, 0)),
                  pl.BlockSpec((D_MODEL, TILE_N), lambda i, j: (0, j))],
        out_specs=pl.BlockSpec((tm, TILE_N), lambda i, j: (i, j)),
        out_shape=jax.ShapeDtypeStruct((t, D_MODEL), F32),
        compiler_params=_params(("arbitrary", "arbitrary")),
        name="out_proj",
    )(merged, w)


def _post_kernel(x_ref, y_ref, g_ref, o_ref, ob_ref):
    y = y_ref[...]
    ms = jnp.mean(y * y, axis=-1, keepdims=True)
    x1 = x_ref[...] + y * lax.rsqrt(ms + EPS) * g_ref[...]
    o_ref[...] = x1
    ob_ref[...] = x1.astype(ob_ref.dtype)


def _post_norm(x, y, g):
    t = x.shape[0]
    row = pl.BlockSpec((ROW_TILE, D_MODEL), lambda i: (i, 0))
    return pl.pallas_call(
        _post_kernel,
        grid=(t // ROW_TILE,),
        in_specs=[row, row, pl.BlockSpec((1, D_MODEL), lambda i: (0, 0))],
        out_specs=[row, row],
        out_shape=[jax.ShapeDtypeStruct((t, D_MODEL), F32), jax.ShapeDtypeStruct((t, D_MODEL), BF16)],
        compiler_params=_params(("arbitrary",)),
        name="post_norm",
    )(x, y, g.reshape(1, D_MODEL))


def _ple_kernel(xb_ref, wg_ref, p_ref, wp_ref, x_ref, o_ref):
    gate = jax.nn.sigmoid(jnp.dot(xb_ref[...], wg_ref[...], preferred_element_type=F32))
    emb = jnp.dot(p_ref[...].astype(BF16), wp_ref[...], preferred_element_type=F32)
    o_ref[...] = x_ref[...] + gate * emb


def _ple(x1, x1b, p, wg, wp):
    t = x1.shape[0]
    tm = min(MAX_TILE_M // 2, t)
    return pl.pallas_call(
        _ple_kernel,
        grid=(t // tm, D_MODEL // TILE_N),
        in_specs=[pl.BlockSpec((tm, D_MODEL), lambda i, j: (i, 0)),
                  pl.BlockSpec((D_MODEL, TILE_N), lambda i, j: (0, j)),
                  pl.BlockSpec((tm, PLE_DIM), lambda i, j: (i, 0)),
                  pl.BlockSpec((PLE_DIM, TILE_N), lambda i, j: (0, j)),
                  pl.BlockSpec((tm, TILE_N), lambda i, j: (i, j))],
        out_specs=pl.BlockSpec((tm, TILE_N), lambda i, j: (i, j)),
        out_shape=jax.ShapeDtypeStruct((t, D_MODEL), F32),
        compiler_params=_params(("arbitrary", "arbitrary")),
        name="ple",
    )(x1b, wg, p, wp, x1)


def _rope_tables(seq):
    half = RET_QK_DIM // 2
    inv = ROPE_BASE ** (-jnp.arange(half, dtype=F32) / half)
    ang = jnp.arange(seq).astype(F32)[:, None] * inv[None, :]
    cos, sin = jnp.cos(ang), jnp.sin(ang)
    return jnp.concatenate([cos, cos], axis=-1), jnp.concatenate([-sin, sin], axis=-1)


def _trunk(x, p, layers, depth):
    batch, seq, _ = x.shape
    t = batch * seq
    x = x.reshape(t, D_MODEL)
    cos_t, sin_t = _rope_tables(seq)
    for i in range(depth):
        lw = layers[i]
        xn = _rmsnorm_pre(x, lw["ln_pre"])
        proj = _in_proj(xn, lw["w_in"], cos_t, sin_t, seq)
        a = _na_attention(proj, lw["na_bias"], batch, seq)
        b = _retention(proj, lw["ret_ld"], lw["ret_gn"], batch, seq)
        merged = _merge(a, b, lw["w_proj_a"], lw["w_proj_b"], proj)
        y = _out_proj(merged, lw["w_out"])
        x1, x1b = _post_norm(x, y, lw["ln_post"])
        x = _ple(x1, x1b, p[i].reshape(t, PLE_DIM), lw["w_ple_gate"], lw["w_ple"])
    return x.reshape(batch, seq, D_MODEL)


def kernel(x_prompt, x_sample, p_prompt, p_sample, w_in, ln_pre, ln_post, na_rpb, ret_log_decay_fwd,
           ret_log_decay_bwd, ret_gn_gain, w_proj_a, w_proj_b, w_out, w_ple, w_ple_gate):
    depth = w_in.shape[0]
    rows = x_prompt.shape[1] // GRID_W
    layers = []
    for i in range(depth):
        layers.append({
            "w_in": w_in[i].astype(BF16),
            "ln_pre": ln_pre[i].astype(F32),
            "ln_post": ln_post[i].astype(F32),
            "na_bias": _na_bias_table(na_rpb[i], rows),
            "ret_ld": jnp.stack([ret_log_decay_fwd[i], ret_log_decay_bwd[i]]).astype(F32),
            "ret_gn": ret_gn_gain[i],
            "w_proj_a": w_proj_a[i].astype(BF16),
            "w_proj_b": w_proj_b[i].astype(BF16),
            "w_out": w_out[i].astype(BF16),
            "w_ple": w_ple[i].astype(BF16),
            "w_ple_gate": w_ple_gate[i].astype(BF16),
        })
    y_prompt = _trunk(x_prompt, p_prompt, layers, depth)
    y_sample = _trunk(x_sample, p_sample, layers, depth)
    return (y_prompt, y_sample)
```

```python
import functools

import numpy as np
import jax
import jax.numpy as jnp
from jax import lax
from jax.experimental import pallas as pl
from jax.experimental.pallas import tpu as pltpu

D_MODEL = 4096
GRID_W = 64
PLE_DIM = 256
NA_HEADS = 16
NA_HEAD_DIM = 128
NA_WIDTH = NA_HEADS * NA_HEAD_DIM
NA_WIN_ROWS = 8
NA_WIN_COLS = 16
RET_HEADS = 8
RET_QK_DIM = 128
RET_V_DIM = 256
RET_QK_WIDTH = RET_HEADS * RET_QK_DIM
RET_V_WIDTH = RET_HEADS * RET_V_DIM
RET_CHUNK = 128
ROPE_BASE = 10000.0
EPS = 1e-6

OFF_NA_Q = 0
OFF_NA_K = OFF_NA_Q + NA_WIDTH
OFF_NA_V = OFF_NA_K + NA_WIDTH
OFF_NA_G = OFF_NA_V + NA_WIDTH
OFF_R_Q = OFF_NA_G + NA_WIDTH
OFF_R_K = OFF_R_Q + RET_QK_WIDTH
OFF_R_V = OFF_R_K + RET_QK_WIDTH
OFF_R_G = OFF_R_V + RET_V_WIDTH
OFF_GATE_A = OFF_R_G + RET_V_WIDTH
OFF_GATE_B = OFF_GATE_A + D_MODEL
IN_WIDTH = OFF_GATE_B + D_MODEL

TILE_N = 1024
MAX_TILE_M = 1024
ROW_TILE = 256
V7X_VMEM_LIMIT_BYTES = 56 * 1024 * 1024

F32 = jnp.float32
BF16 = jnp.bfloat16


def _params(semantics, vmem=V7X_VMEM_LIMIT_BYTES):
    return pltpu.CompilerParams(dimension_semantics=semantics, vmem_limit_bytes=vmem)


def _rmsnorm_kernel(x_ref, g_ref, o_ref):
    x = x_ref[...]
    ms = jnp.mean(x * x, axis=-1, keepdims=True)
    o_ref[...] = (x * lax.rsqrt(ms + EPS) * g_ref[...]).astype(o_ref.dtype)


def _rmsnorm_pre(x, g):
    t = x.shape[0]
    return pl.pallas_call(
        _rmsnorm_kernel,
        grid=(t // ROW_TILE,),
        in_specs=[pl.BlockSpec((ROW_TILE, D_MODEL), lambda i: (i, 0)),
                  pl.BlockSpec((1, D_MODEL), lambda i: (0, 0))],
        out_specs=pl.BlockSpec((ROW_TILE, D_MODEL), lambda i: (i, 0)),
        out_shape=jax.ShapeDtypeStruct((t, D_MODEL), BF16),
        compiler_params=_params(("arbitrary",)),
        name="rmsnorm_pre",
    )(x, g.reshape(1, D_MODEL))


_J_NA_K = OFF_NA_K // TILE_N
_J_NA_G = OFF_NA_G // TILE_N
_J_R_Q = OFF_R_Q // TILE_N
_J_R_K = OFF_R_K // TILE_N
_J_R_V = OFF_R_V // TILE_N
_J_R_G = OFF_R_G // TILE_N
_J_GATE = OFF_GATE_A // TILE_N


def _inproj_kernel(x_ref, w_ref, cos_ref, sin_ref, o_ref, acc_ref):
    j = pl.program_id(1)
    acc_ref[...] = jnp.dot(x_ref[...], w_ref[...], preferred_element_type=F32)

    @pl.when(j < _J_NA_K)
    def _():
        o_ref[...] = (acc_ref[...] * (NA_HEAD_DIM ** -0.5)).astype(o_ref.dtype)

    is_silu = ((j >= _J_NA_G) & (j < _J_R_Q)) | ((j >= _J_R_G) & (j < _J_GATE))

    @pl.when(is_silu)
    def _():
        y = acc_ref[...]
        o_ref[...] = (y * jax.nn.sigmoid(y)).astype(o_ref.dtype)

    @pl.when(j >= _J_GATE)
    def _():
        o_ref[...] = jax.nn.sigmoid(acc_ref[...]).astype(o_ref.dtype)

    is_plain = ((j >= _J_NA_K) & (j < _J_NA_G)) | ((j >= _J_R_V) & (j < _J_R_G))

    @pl.when(is_plain)
    def _():
        o_ref[...] = acc_ref[...].astype(o_ref.dtype)

    def rotary(scale):
        c = cos_ref[...]
        s = sin_ref[...]
        for h in range(TILE_N // RET_QK_DIM):
            sl = slice(h * RET_QK_DIM, (h + 1) * RET_QK_DIM)
            t = acc_ref[:, sl]
            r = pltpu.roll(t, RET_QK_DIM // 2, axis=1)
            y = t * c + r * s
            if scale is not None:
                y = y * scale
            o_ref[:, sl] = y.astype(o_ref.dtype)

    @pl.when(j == _J_R_Q)
    def _():
        rotary(None)

    @pl.when(j == _J_R_K)
    def _():
        rotary(RET_QK_DIM ** -0.5)


def _in_proj(xn, w, cos_t, sin_t, seq):
    t = xn.shape[0]
    tm = min(MAX_TILE_M, seq)
    pos_tiles = seq // tm
    return pl.pallas_call(
        _inproj_kernel,
        grid=(t // tm, IN_WIDTH // TILE_N),
        in_specs=[pl.BlockSpec((tm, D_MODEL), lambda i, j: (i, 0)),
                  pl.BlockSpec((D_MODEL, TILE_N), lambda i, j: (0, j)),
                  pl.BlockSpec((tm, RET_QK_DIM), lambda i, j: (i % pos_tiles, 0)),
                  pl.BlockSpec((tm, RET_QK_DIM), lambda i, j: (i % pos_tiles, 0))],
        out_specs=pl.BlockSpec((tm, TILE_N), lambda i, j: (i, j)),
        out_shape=jax.ShapeDtypeStruct((t, IN_WIDTH), BF16),
        scratch_shapes=[pltpu.VMEM((tm, TILE_N), F32)],
        compiler_params=_params(("arbitrary", "arbitrary")),
        name="in_proj",
    )(xn, w, cos_t, sin_t)


NA_PAIR_ROWS = 2
NA_PAIR_TOKENS = NA_PAIR_ROWS * GRID_W
NA_KEY_ROWS = NA_WIN_ROWS + NA_PAIR_ROWS
NA_KEY_CHUNKS = NA_KEY_ROWS // NA_PAIR_ROWS
NA_KEYS = NA_KEY_ROWS * GRID_W
NA_VARIANTS = 5
NA_PAIR_BLOCK = 4


def _na_kernel(q_ref, k_ref, v_ref, g_ref, bias_ref, o_ref, vt_ref, s_ref, p_ref, linv_ref, *, rows):
    n_pairs = rows // NA_PAIR_ROWS
    n_chunks = n_pairs
    ct = NA_PAIR_TOKENS
    nt = (((1,), (1,)), ((), ()))
    eye = (lax.broadcasted_iota(jnp.int32, (ct, ct), 0)
           == lax.broadcasted_iota(jnp.int32, (ct, ct), 1)).astype(BF16)

    def transpose_chunk(c, carry):
        r0 = pl.multiple_of(c * ct, ct)
        vt = lax.dot_general(eye, v_ref[pl.ds(r0, ct), :], nt, preferred_element_type=F32)
        vt_ref[c] = vt.astype(BF16)
        return carry

    lax.fori_loop(0, n_chunks, transpose_chunk, 0, unroll=4)

    def block(blk, carry):
        pairs = []
        for u in range(NA_PAIR_BLOCK):
            pr = blk * NA_PAIR_BLOCK + u
            c0 = jnp.clip(pr - NA_WIN_ROWS // 4, 0, n_chunks - NA_KEY_CHUNKS)
            var = jnp.where(pr < 2, pr, jnp.where(pr >= n_pairs - 2, pr - (n_pairs - 2) + 3, 2))
            q0 = pl.multiple_of(pr * ct, ct)
            k0 = pl.multiple_of(c0 * ct, ct)
            pairs.append((c0, q0))
            s = lax.dot_general(k_ref[pl.ds(k0, NA_KEYS), :], q_ref[pl.ds(q0, ct), :], nt,
                                preferred_element_type=F32)
            s_ref[u] = s + bias_ref[var]
        for u in range(NA_PAIR_BLOCK):
            s = s_ref[u]
            m = jnp.max(s, axis=0, keepdims=True)
            p = jnp.exp(s - m)
            linv_ref[u] = 1.0 / jnp.sum(p, axis=0, keepdims=True)
            p_ref[u] = p.astype(BF16)
        for u in range(NA_PAIR_BLOCK):
            c0, q0 = pairs[u]
            vt = jnp.concatenate([vt_ref[c0 + c] for c in range(NA_KEY_CHUNKS)], axis=1)
            ot = jnp.dot(vt, p_ref[u], preferred_element_type=F32) * linv_ref[u]
            o = ot.T * g_ref[pl.ds(q0, ct), :].astype(F32)
            o_ref[pl.ds(q0, ct), :] = o.astype(o_ref.dtype)
        return carry

    lax.fori_loop(0, n_pairs // NA_PAIR_BLOCK, block, 0)


def _na_bias_table(rpb, rows):
    n_pairs = rows // NA_PAIR_ROWS
    cols = np.arange(GRID_W)
    col_start = np.clip(cols - NA_WIN_COLS // 2, 0, GRID_W - NA_WIN_COLS)
    col_valid = (cols[None, :] >= col_start[:, None]) & (cols[None, :] < col_start[:, None] + NA_WIN_COLS)
    col_idx = np.clip(cols[None, :] - cols[:, None] + NA_WIN_COLS - 1, 0, 2 * NA_WIN_COLS - 2)
    w = np.arange(NA_KEY_ROWS)
    i = np.arange(NA_PAIR_ROWS)
    tables = []
    for pr in (0, 1, 2, n_pairs - 2, n_pairs - 1):
        base = np.clip(NA_PAIR_ROWS * pr - NA_WIN_ROWS // 2, 0, rows - NA_KEY_ROWS)
        r_q = NA_PAIR_ROWS * pr + i
        rs = np.clip(r_q - NA_WIN_ROWS // 2, 0, rows - NA_WIN_ROWS)
        kr = base + w
        row_valid = (kr[:, None] >= rs[None, :]) & (kr[:, None] < rs[None, :] + NA_WIN_ROWS)
        row_idx = np.clip(kr[:, None] - r_q[None, :] + NA_WIN_ROWS - 1, 0, 2 * NA_WIN_ROWS - 2)
        shape = (NA_KEY_ROWS, GRID_W, NA_PAIR_ROWS, GRID_W)
        ridx = np.broadcast_to(row_idx[:, None, :, None], shape)
        cidx = np.broadcast_to(col_idx.T[None, :, None, :], shape)
        valid = row_valid[:, None, :, None] & col_valid.T[None, :, None, :]
        t = jnp.where(valid[None], rpb.astype(F32)[:, ridx, cidx], -jnp.inf)
        tables.append(t.reshape(NA_HEADS, NA_KEYS, NA_PAIR_TOKENS))
    return jnp.stack(tables, axis=1)


def _na_attention(proj, bias, batch, seq):
    rows = seq // GRID_W
    n_pairs = rows // NA_PAIR_ROWS
    assert rows % NA_PAIR_ROWS == 0 and rows >= NA_KEY_ROWS + 2 and n_pairs % NA_PAIR_BLOCK == 0
    hd = NA_HEAD_DIM
    blk = lambda off: pl.BlockSpec((seq, hd), lambda b, h, off=off: (b, off // hd + h))
    return pl.pallas_call(
        functools.partial(_na_kernel, rows=rows),
        grid=(batch, NA_HEADS),
        in_specs=[blk(OFF_NA_Q), blk(OFF_NA_K), blk(OFF_NA_V), blk(OFF_NA_G),
                  pl.BlockSpec((None, NA_VARIANTS, NA_KEYS, NA_PAIR_TOKENS), lambda b, h: (h, 0, 0, 0))],
        out_specs=pl.BlockSpec((seq, hd), lambda b, h: (b, h)),
        out_shape=jax.ShapeDtypeStruct((batch * seq, NA_WIDTH), BF16),
        scratch_shapes=[pltpu.VMEM((n_pairs, hd, NA_PAIR_TOKENS), BF16),
                        pltpu.VMEM((NA_PAIR_BLOCK, NA_KEYS, NA_PAIR_TOKENS), F32),
                        pltpu.VMEM((NA_PAIR_BLOCK, NA_KEYS, NA_PAIR_TOKENS), BF16),
                        pltpu.VMEM((NA_PAIR_BLOCK, 1, NA_PAIR_TOKENS), F32)],
        compiler_params=_params(("arbitrary", "arbitrary")),
        name="na_attention",
    )(proj, proj, proj, proj, bias)


def _ret_kernel(ld_ref, q_ref, k_ref, v_ref, g_ref, gain_ref, o_ref,
                sb_ref, sf_ref, st_ref, dsum_ref, qdf_ref, qdb_ref, kdf_ref, kdb_ref, *, n_chunks):
    c_len = RET_CHUNK
    h = pl.program_id(1)
    ldf = -jnp.abs(ld_ref[0, h])
    ldb = -jnp.abs(ld_ref[1, h])
    row = lax.broadcasted_iota(jnp.int32, (c_len, c_len), 0).astype(F32)
    col = lax.broadcasted_iota(jnp.int32, (c_len, c_len), 1).astype(F32)
    diff = row - col
    dsum_ref[...] = (jnp.where(diff >= 0, jnp.exp(ldf * jnp.maximum(diff, 0.0)), 0.0)
                     + jnp.where(diff <= 0, jnp.exp(ldb * jnp.maximum(-diff, 0.0)), 0.0))
    qdf_ref[...] = jnp.exp(ldf * (row + 1.0))
    kdf_ref[...] = jnp.exp(ldf * (c_len - 1.0 - row))
    qdb_ref[...] = jnp.exp(ldb * (c_len - row))
    kdb_ref[...] = jnp.exp(ldb * row)
    zero_row = jnp.zeros((1, RET_V_DIM), F32)
    cdf = jnp.exp(zero_row + ldf * c_len)
    cdb = jnp.exp(zero_row + ldb * c_len)
    tdot = lambda a, b: lax.dot_general(a, b, (((0,), (0,)), ((), ())), preferred_element_type=F32)

    st_ref[...] = jnp.zeros_like(st_ref)

    def back(t, carry):
        c = n_chunks - 1 - t
        r0 = pl.multiple_of(c * c_len, c_len)
        sb_ref[c] = st_ref[...].astype(BF16)
        kd = (k_ref[pl.ds(r0, c_len), :].astype(F32) * kdb_ref[...]).astype(BF16)
        st_ref[...] = st_ref[...] * cdb + tdot(kd, v_ref[pl.ds(r0, c_len), :])
        return carry

    lax.fori_loop(0, n_chunks, back, 0)

    sf_ref[...] = jnp.zeros_like(sf_ref)

    def fwd(c, carry):
        r0 = pl.multiple_of(c * c_len, c_len)
        q = q_ref[pl.ds(r0, c_len), :]
        k = k_ref[pl.ds(r0, c_len), :]
        v = v_ref[pl.ds(r0, c_len), :]
        qf = q.astype(F32)
        sc = lax.dot_general(q, k, (((1,), (1,)), ((), ())), preferred_element_type=F32)
        o = jnp.dot((sc * dsum_ref[...]).astype(BF16), v, preferred_element_type=F32)
        o = o + jnp.dot((qf * qdf_ref[...]).astype(BF16), sf_ref[...].astype(BF16),
                        preferred_element_type=F32)
        o = o + jnp.dot((qf * qdb_ref[...]).astype(BF16), sb_ref[c], preferred_element_type=F32)
        o = o * lax.rsqrt(jnp.mean(o * o, axis=-1, keepdims=True) + EPS)
        o = o * gain_ref[...] * g_ref[pl.ds(r0, c_len), :].astype(F32)
        o_ref[pl.ds(r0, c_len), :] = o.astype(o_ref.dtype)
        kd = (k.astype(F32) * kdf_ref[...]).astype(BF16)
        sf_ref[...] = sf_ref[...] * cdf + tdot(kd, v)
        return carry

    lax.fori_loop(0, n_chunks, fwd, 0)


def _retention(proj, ld, gain, batch, seq):
    n_chunks = seq // RET_CHUNK
    dk, dv, c_len = RET_QK_DIM, RET_V_DIM, RET_CHUNK
    qk_blk = lambda off: pl.BlockSpec((seq, dk), lambda b, h, off=off: (b, off // dk + h))
    v_blk = lambda off: pl.BlockSpec((seq, dv), lambda b, h, off=off: (b, off // dv + h))
    return pl.pallas_call(
        functools.partial(_ret_kernel, n_chunks=n_chunks),
        grid=(batch, RET_HEADS),
        in_specs=[pl.BlockSpec(memory_space=pltpu.SMEM),
                  qk_blk(OFF_R_Q), qk_blk(OFF_R_K), v_blk(OFF_R_V), v_blk(OFF_R_G),
                  pl.BlockSpec((None, 1, dv), lambda b, h: (h, 0, 0))],
        out_specs=pl.BlockSpec((seq, dv), lambda b, h: (b, h)),
        out_shape=jax.ShapeDtypeStruct((batch * seq, RET_V_WIDTH), BF16),
        scratch_shapes=[pltpu.VMEM((n_chunks, dk, dv), BF16),
                        pltpu.VMEM((dk, dv), F32),
                        pltpu.VMEM((dk, dv), F32),
                        pltpu.VMEM((c_len, c_len), F32),
                        pltpu.VMEM((c_len, dk), F32),
                        pltpu.VMEM((c_len, dk), F32),
                        pltpu.VMEM((c_len, dk), F32),
                        pltpu.VMEM((c_len, dk), F32)],
        compiler_params=_params(("arbitrary", "arbitrary")),
        name="retention",
    )(ld, proj, proj, proj, proj, gain.astype(F32).reshape(RET_HEADS, 1, dv))


def _merge_kernel(a_ref, b_ref, wa_ref, wb_ref, ga_ref, gb_ref, o_ref):
    ya = jnp.dot(a_ref[...], wa_ref[...], preferred_element_type=F32)
    yb = jnp.dot(b_ref[...], wb_ref[...], preferred_element_type=F32)
    o_ref[...] = (ga_ref[...].astype(F32) * ya + gb_ref[...].astype(F32) * yb).astype(o_ref.dtype)


def _merge(a, b, wa, wb, proj):
    t = a.shape[0]
    tm = min(MAX_TILE_M, t)
    ja, jb = OFF_GATE_A // TILE_N, OFF_GATE_B // TILE_N
    return pl.pallas_call(
        _merge_kernel,
        grid=(t // tm, D_MODEL // TILE_N),
        in_specs=[pl.BlockSpec((tm, NA_WIDTH), lambda i, j: (i, 0)),
                  pl.BlockSpec((tm, RET_V_WIDTH), lambda i, j: (i, 0)),
                  pl.BlockSpec((NA_WIDTH, TILE_N), lambda i, j: (0, j)),
                  pl.BlockSpec((RET_V_WIDTH, TILE_N), lambda i, j: (0, j)),
                  pl.BlockSpec((tm, TILE_N), lambda i, j: (i, ja + j)),
                  pl.BlockSpec((tm, TILE_N), lambda i, j: (i, jb + j))],
        out_specs=pl.BlockSpec((tm, TILE_N), lambda i, j: (i, j)),
        out_shape=jax.ShapeDtypeStruct((t, D_MODEL), BF16),
        compiler_params=_params(("arbitrary", "arbitrary")),
        name="merge",
    )(a, b, wa, wb, proj, proj)


def _matmul_kernel(x_ref, w_ref, o_ref):
    o_ref[...] = jnp.dot(x_ref[...], w_ref[...], preferred_element_type=F32).astype(o_ref.dtype)


def _out_proj(merged, w):
    t = merged.shape[0]
    tm = min(MAX_TILE_M, t)
    return pl.pallas_call(
        _matmul_kernel,
        grid=(t // tm, D_MODEL // TILE_N),
        in_specs=[pl.BlockSpec((tm, D_MODEL), lambda i, j: (i, 0)),
                  pl.BlockSpec((D_MODEL, TILE_N), lambda i, j: (0, j))],
        out_specs=pl.BlockSpec((tm, TILE_N), lambda i, j: (i, j)),
        out_shape=jax.ShapeDtypeStruct((t, D_MODEL), F32),
        compiler_params=_params(("arbitrary", "arbitrary")),
        name="out_proj",
    )(merged, w)


def _post_kernel(x_ref, y_ref, g_ref, o_ref, ob_ref):
    y = y_ref[...]
    ms = jnp.mean(y * y, axis=-1, keepdims=True)
    x1 = x_ref[...] + y * lax.rsqrt(ms + EPS) * g_ref[...]
    o_ref[...] = x1
    ob_ref[...] = x1.astype(ob_ref.dtype)


def _post_norm(x, y, g):
    t = x.shape[0]
    row = pl.BlockSpec((ROW_TILE, D_MODEL), lambda i: (i, 0))
    return pl.pallas_call(
        _post_kernel,
        grid=(t // ROW_TILE,),
        in_specs=[row, row, pl.BlockSpec((1, D_MODEL), lambda i: (0, 0))],
        out_specs=[row, row],
        out_shape=[jax.ShapeDtypeStruct((t, D_MODEL), F32), jax.ShapeDtypeStruct((t, D_MODEL), BF16)],
        compiler_params=_params(("arbitrary",)),
        name="post_norm",
    )(x, y, g.reshape(1, D_MODEL))


def _ple_kernel(xb_ref, wg_ref, p_ref, wp_ref, x_ref, o_ref):
    gate = jax.nn.sigmoid(jnp.dot(xb_ref[...], wg_ref[...], preferred_element_type=F32))
    emb = jnp.dot(p_ref[...].astype(BF16), wp_ref[...], preferred_element_type=F32)
    o_ref[...] = x_ref[...] + gate * emb


def _ple(x1, x1b, p, wg, wp):
    t = x1.shape[0]
    tm = min(MAX_TILE_M // 2, t)
    return pl.pallas_call(
        _ple_kernel,
        grid=(t // tm, D_MODEL // TILE_N),
        in_specs=[pl.BlockSpec((tm, D_MODEL), lambda i, j: (i, 0)),
                  pl.BlockSpec((D_MODEL, TILE_N), lambda i, j: (0, j)),
                  pl.BlockSpec((tm, PLE_DIM), lambda i, j: (i, 0)),
                  pl.BlockSpec((PLE_DIM, TILE_N), lambda i, j: (0, j)),
                  pl.BlockSpec((tm, TILE_N), lambda i, j: (i, j))],
        out_specs=pl.BlockSpec((tm, TILE_N), lambda i, j: (i, j)),
        out_shape=jax.ShapeDtypeStruct((t, D_MODEL), F32),
        compiler_params=_params(("arbitrary", "arbitrary")),
        name="ple",
    )(x1b, wg, p, wp, x1)


def _rope_tables(seq):
    half = RET_QK_DIM // 2
    inv = ROPE_BASE ** (-jnp.arange(half, dtype=F32) / half)
    ang = jnp.arange(seq).astype(F32)[:, None] * inv[None, :]
    cos, sin = jnp.cos(ang), jnp.sin(ang)
    return jnp.concatenate([cos, cos], axis=-1), jnp.concatenate([-sin, sin], axis=-1)


def _trunk(x, p, layers, depth):
    batch, seq, _ = x.shape
    t = batch * seq
    x = x.reshape(t, D_MODEL)
    cos_t, sin_t = _rope_tables(seq)
    for i in range(depth):
        lw = layers[i]
        xn = _rmsnorm_pre(x, lw["ln_pre"])
        proj = _in_proj(xn, lw["w_in"], cos_t, sin_t, seq)
        a = _na_attention(proj, lw["na_bias"], batch, seq)
        b = _retention(proj, lw["ret_ld"], lw["ret_gn"], batch, seq)
        merged = _merge(a, b, lw["w_proj_a"], lw["w_proj_b"], proj)
        y = _out_proj(merged, lw["w_out"])
        x1, x1b = _post_norm(x, y, lw["ln_post"])
        x = _ple(x1, x1b, p[i].reshape(t, PLE_DIM), lw["w_ple_gate"], lw["w_ple"])
    return x.reshape(batch, seq, D_MODEL)


def kernel(x_prompt, x_sample, p_prompt, p_sample, w_in, ln_pre, ln_post, na_rpb, ret_log_decay_fwd,
           ret_log_decay_bwd, ret_gn_gain, w_proj_a, w_proj_b, w_out, w_ple, w_ple_gate):
    depth = w_in.shape[0]
    rows = x_prompt.shape[1] // GRID_W
    layers = []
    for i in range(depth):
        layers.append({
            "w_in": w_in[i].astype(BF16),
            "ln_pre": ln_pre[i].astype(F32),
            "ln_post": ln_post[i].astype(F32),
            "na_bias": _na_bias_table(na_rpb[i], rows),
            "ret_ld": jnp.stack([ret_log_decay_fwd[i], ret_log_decay_bwd[i]]).astype(F32),
            "ret_gn": ret_gn_gain[i],
            "w_proj_a": w_proj_a[i].astype(BF16),
            "w_proj_b": w_proj_b[i].astype(BF16),
            "w_out": w_out[i].astype(BF16),
            "w_ple": w_ple[i].astype(BF16),
            "w_ple_gate": w_ple_gate[i].astype(BF16),
        })
    y_prompt = _trunk(x_prompt, p_prompt, layers, depth)
    y_sample = _trunk(x_sample, p_sample, layers, depth)
    return (y_prompt, y_sample)
```

```python
import functools

import numpy as np
import jax
import jax.numpy as jnp
from jax import lax
from jax.experimental import pallas as pl
from jax.experimental.pallas import tpu as pltpu

D_MODEL = 4096
GRID_W = 64
PLE_DIM = 256
NA_HEADS = 16
NA_HEAD_DIM = 128
NA_WIDTH = NA_HEADS * NA_HEAD_DIM
NA_WIN_ROWS = 8
NA_WIN_COLS = 16
RET_HEADS = 8
RET_QK_DIM = 128
RET_V_DIM = 256
RET_QK_WIDTH = RET_HEADS * RET_QK_DIM
RET_V_WIDTH = RET_HEADS * RET_V_DIM
RET_CHUNK = 128
ROPE_BASE = 10000.0
EPS = 1e-6

OFF_NA_Q = 0
OFF_NA_K = OFF_NA_Q + NA_WIDTH
OFF_NA_V = OFF_NA_K + NA_WIDTH
OFF_NA_G = OFF_NA_V + NA_WIDTH
OFF_R_Q = OFF_NA_G + NA_WIDTH
OFF_R_K = OFF_R_Q + RET_QK_WIDTH
OFF_R_V = OFF_R_K + RET_QK_WIDTH
OFF_R_G = OFF_R_V + RET_V_WIDTH
OFF_GATE_A = OFF_R_G + RET_V_WIDTH
OFF_GATE_B = OFF_GATE_A + D_MODEL
IN_WIDTH = OFF_GATE_B + D_MODEL

TILE_N = 1024
MAX_TILE_M = 1024
ROW_TILE = 256
V7X_VMEM_LIMIT_BYTES = 56 * 1024 * 1024

F32 = jnp.float32
BF16 = jnp.bfloat16


def _params(semantics, vmem=V7X_VMEM_LIMIT_BYTES):
    return pltpu.CompilerParams(dimension_semantics=semantics, vmem_limit_bytes=vmem)


def _rmsnorm_kernel(x_ref, g_ref, o_ref):
    x = x_ref[...]
    ms = jnp.mean(x * x, axis=-1, keepdims=True)
    o_ref[...] = (x * lax.rsqrt(ms + EPS) * g_ref[...]).astype(o_ref.dtype)


def _rmsnorm_pre(x, g):
    t = x.shape[0]
    return pl.pallas_call(
        _rmsnorm_kernel,
        grid=(t // ROW_TILE,),
        in_specs=[pl.BlockSpec((ROW_TILE, D_MODEL), lambda i: (i, 0)),
                  pl.BlockSpec((1, D_MODEL), lambda i: (0, 0))],
        out_specs=pl.BlockSpec((ROW_TILE, D_MODEL), lambda i: (i, 0)),
        out_shape=jax.ShapeDtypeStruct((t, D_MODEL), BF16),
        compiler_params=_params(("arbitrary",)),
        name="rmsnorm_pre",
    )(x, g.reshape(1, D_MODEL))


_J_NA_K = OFF_NA_K // TILE_N
_J_NA_G = OFF_NA_G // TILE_N
_J_R_Q = OFF_R_Q // TILE_N
_J_R_K = OFF_R_K // TILE_N
_J_R_G = OFF_R_G // TILE_N
_J_GATE = OFF_GATE_A // TILE_N
assert NA_HEAD_DIM == RET_QK_DIM
QK_SCALE = NA_HEAD_DIM ** -0.5


def _inproj_kernel(x_ref, w_ref, o_ref):
    j = pl.program_id(1)
    y = jnp.dot(x_ref[...], w_ref[...], preferred_element_type=F32)
    sig = 0.5 * jnp.tanh(0.5 * y) + 0.5
    is_gate = j >= _J_GATE
    is_silu = ((j >= _J_NA_G) & (j < _J_R_Q)) | ((j >= _J_R_G) & (j < _J_GATE))
    is_scaled = (j < _J_NA_K) | (j == _J_R_K)
    mult = jnp.where(is_silu, sig, jnp.where(is_scaled, QK_SCALE, 1.0))
    o_ref[...] = jnp.where(is_gate, sig, y * mult).astype(o_ref.dtype)


def _in_proj(xn, w, seq):
    t = xn.shape[0]
    tm = min(MAX_TILE_M, seq)
    return pl.pallas_call(
        _inproj_kernel,
        grid=(t // tm, IN_WIDTH // TILE_N),
        in_specs=[pl.BlockSpec((tm, D_MODEL), lambda i, j: (i, 0)),
                  pl.BlockSpec((D_MODEL, TILE_N), lambda i, j: (0, j))],
        out_specs=pl.BlockSpec((tm, TILE_N), lambda i, j: (i, j)),
        out_shape=jax.ShapeDtypeStruct((t, IN_WIDTH), BF16),
        compiler_params=_params(("arbitrary", "arbitrary")),
        name="in_proj",
    )(xn, w)


NA_PAIR_ROWS = 2
NA_PAIR_TOKENS = NA_PAIR_ROWS * GRID_W
NA_KEY_ROWS = NA_WIN_ROWS + NA_PAIR_ROWS
NA_KEY_CHUNKS = NA_KEY_ROWS // NA_PAIR_ROWS
NA_KEYS = NA_KEY_ROWS * GRID_W
NA_VARIANTS = 5
NA_PAIR_BLOCK = 8


def _na_kernel(q_ref, k_ref, v_ref, g_ref, bias_ref, o_ref, vt_ref, s_ref, p_ref, linv_ref, *, rows):
    n_pairs = rows // NA_PAIR_ROWS
    n_chunks = n_pairs
    ct = NA_PAIR_TOKENS
    nt = (((1,), (1,)), ((), ()))
    eye = (lax.broadcasted_iota(jnp.int32, (ct, ct), 0)
           == lax.broadcasted_iota(jnp.int32, (ct, ct), 1)).astype(BF16)

    def transpose_chunk(c, carry):
        r0 = pl.multiple_of(c * ct, ct)
        vt = lax.dot_general(eye, v_ref[pl.ds(r0, ct), :], nt, preferred_element_type=F32)
        vt_ref[c] = vt.astype(BF16)
        return carry

    lax.fori_loop(0, n_chunks, transpose_chunk, 0, unroll=NA_PAIR_BLOCK)

    def block(blk, carry):
        pairs = []
        for u in range(NA_PAIR_BLOCK):
            pr = blk * NA_PAIR_BLOCK + u
            c0 = jnp.clip(pr - NA_WIN_ROWS // 4, 0, n_chunks - NA_KEY_CHUNKS)
            var = jnp.where(pr < 2, pr, jnp.where(pr >= n_pairs - 2, pr - (n_pairs - 2) + 3, 2))
            q0 = pl.multiple_of(pr * ct, ct)
            k0 = pl.multiple_of(c0 * ct, ct)
            pairs.append((c0, q0))
            s = lax.dot_general(k_ref[pl.ds(k0, NA_KEYS), :], q_ref[pl.ds(q0, ct), :], nt,
                                preferred_element_type=F32)
            s_ref[u] = s + bias_ref[var]
        for u in range(NA_PAIR_BLOCK):
            s = s_ref[u]
            m = jnp.max(s, axis=0, keepdims=True)
            p = jnp.exp(s - m)
            linv_ref[u] = 1.0 / jnp.sum(p, axis=0, keepdims=True)
            p_ref[u] = p.astype(BF16)
        for u in range(NA_PAIR_BLOCK):
            c0, q0 = pairs[u]
            vt = jnp.concatenate([vt_ref[c0 + c] for c in range(NA_KEY_CHUNKS)], axis=1)
            ot = jnp.dot(vt, p_ref[u], preferred_element_type=F32) * linv_ref[u]
            o = ot.T * g_ref[pl.ds(q0, ct), :].astype(F32)
            o_ref[pl.ds(q0, ct), :] = o.astype(o_ref.dtype)
        return carry

    lax.fori_loop(0, n_pairs // NA_PAIR_BLOCK, block, 0)


def _na_bias_table(rpb, rows):
    n_pairs = rows // NA_PAIR_ROWS
    cols = np.arange(GRID_W)
    col_start = np.clip(cols - NA_WIN_COLS // 2, 0, GRID_W - NA_WIN_COLS)
    col_valid = (cols[None, :] >= col_start[:, None]) & (cols[None, :] < col_start[:, None] + NA_WIN_COLS)
    pad = GRID_W - NA_WIN_COLS
    rp = jnp.pad(rpb.astype(F32), ((0, 0), (0, 0), (pad, pad)))
    toep = jnp.stack([rp[:, :, GRID_W - 1 - qc: 2 * GRID_W - 1 - qc] for qc in range(GRID_W)], axis=2)
    toep = jnp.where(col_valid[None, None], toep, -jnp.inf)
    toep_t = toep.transpose(0, 1, 3, 2)
    masked = jnp.full((NA_HEADS, GRID_W, GRID_W), -jnp.inf, F32)
    tables = []
    for pr in (0, 1, 2, n_pairs - 2, n_pairs - 1):
        base = int(np.clip(NA_PAIR_ROWS * pr - NA_WIN_ROWS // 2, 0, rows - NA_KEY_ROWS))
        slabs = []
        for w in range(NA_KEY_ROWS):
            per_row = []
            for i in range(NA_PAIR_ROWS):
                r_q = NA_PAIR_ROWS * pr + i
                rs = int(np.clip(r_q - NA_WIN_ROWS // 2, 0, rows - NA_WIN_ROWS))
                kr = base + w
                if rs <= kr < rs + NA_WIN_ROWS:
                    per_row.append(toep_t[:, kr - r_q + NA_WIN_ROWS - 1])
                else:
                    per_row.append(masked)
            slabs.append(jnp.concatenate(per_row, axis=-1))
        tables.append(jnp.concatenate(slabs, axis=1))
    return jnp.stack(tables, axis=1)


def _na_attention(proj, bias, batch, seq):
    rows = seq // GRID_W
    n_pairs = rows // NA_PAIR_ROWS
    assert rows % NA_PAIR_ROWS == 0 and rows >= NA_KEY_ROWS + 2 and n_pairs % NA_PAIR_BLOCK == 0
    hd = NA_HEAD_DIM
    blk = lambda off: pl.BlockSpec((seq, hd), lambda b, h, off=off: (b, off // hd + h))
    return pl.pallas_call(
        functools.partial(_na_kernel, rows=rows),
        grid=(batch, NA_HEADS),
        in_specs=[blk(OFF_NA_Q), blk(OFF_NA_K), blk(OFF_NA_V), blk(OFF_NA_G),
                  pl.BlockSpec((None, NA_VARIANTS, NA_KEYS, NA_PAIR_TOKENS), lambda b, h: (h, 0, 0, 0))],
        out_specs=pl.BlockSpec((seq, hd), lambda b, h: (b, h)),
        out_shape=jax.ShapeDtypeStruct((batch * seq, NA_WIDTH), BF16),
        scratch_shapes=[pltpu.VMEM((n_pairs, hd, NA_PAIR_TOKENS), BF16),
                        pltpu.VMEM((NA_PAIR_BLOCK, NA_KEYS, NA_PAIR_TOKENS), F32),
                        pltpu.VMEM((NA_PAIR_BLOCK, NA_KEYS, NA_PAIR_TOKENS), BF16),
                        pltpu.VMEM((NA_PAIR_BLOCK, 1, NA_PAIR_TOKENS), F32)],
        compiler_params=_params(("arbitrary", "arbitrary")),
        name="na_attention",
    )(proj, proj, proj, proj, bias)


RET_UNROLL = 8


def _ret_kernel(ld_ref, q_ref, k_ref, v_ref, g_ref, gain_ref, cos_ref, sin_ref, o_ref,
                qr_ref, kr_ref, kv_ref, st_ref, dsum_ref, qd_ref, kd_ref, *, n_chunks):
    c_len, dk = RET_CHUNK, RET_QK_DIM
    h = pl.program_id(1)
    ldf = -jnp.abs(ld_ref[0, h])
    ldb = -jnp.abs(ld_ref[1, h])
    row = lax.broadcasted_iota(jnp.int32, (c_len, c_len), 0).astype(F32)
    col = lax.broadcasted_iota(jnp.int32, (c_len, c_len), 1).astype(F32)
    diff = row - col
    dsum_ref[...] = (jnp.where(diff >= 0, jnp.exp(ldf * jnp.maximum(diff, 0.0)), 0.0)
                     + jnp.where(diff <= 0, jnp.exp(ldb * jnp.maximum(-diff, 0.0)), 0.0))
    qd_ref[:, :dk] = jnp.exp(ldf * (row + 1.0))
    qd_ref[:, dk:] = jnp.exp(ldb * (c_len - row))
    kd_ref[:, :dk] = jnp.exp(ldf * (c_len - 1.0 - row))
    kd_ref[:, dk:] = jnp.exp(ldb * row)
    zero_row = jnp.zeros((1, RET_V_DIM), F32)
    cdf = jnp.exp(zero_row + ldf * c_len)
    cdb = jnp.exp(zero_row + ldb * c_len)
    chunk = lambda c: pl.ds(pl.multiple_of(c * c_len, c_len), c_len)

    def rotate(c, carry):
        rows = chunk(c)
        cos, sin = cos_ref[rows, :], sin_ref[rows, :]
        for src, dst in ((q_ref, qr_ref), (k_ref, kr_ref)):
            t = src[rows, :].astype(F32)
            dst[rows, :] = t * cos + pltpu.roll(t, dk // 2, axis=1) * sin
        return carry

    lax.fori_loop(0, n_chunks, rotate, 0, unroll=RET_UNROLL)

    def chunk_kv(c, carry):
        rows = chunk(c)
        k = kr_ref[rows, :]
        k2 = (jnp.concatenate([k, k], axis=1) * kd_ref[...]).astype(BF16)
        kv_ref[c] = lax.dot_general(k2, v_ref[rows, :], (((0,), (0,)), ((), ())),
                                    preferred_element_type=F32)
        return carry

    lax.fori_loop(0, n_chunks, chunk_kv, 0, unroll=RET_UNROLL)

    def scan_fwd(c, s):
        st_ref[c, :dk, :] = s.astype(BF16)
        return s * cdf + kv_ref[c, :dk, :]

    def scan_bwd(t, s):
        c = n_chunks - 1 - t
        st_ref[c, dk:, :] = s.astype(BF16)
        return s * cdb + kv_ref[c, dk:, :]

    zero_state = jnp.zeros((dk, RET_V_DIM), F32)
    lax.fori_loop(0, n_chunks, scan_fwd, zero_state, unroll=RET_UNROLL)
    lax.fori_loop(0, n_chunks, scan_bwd, zero_state, unroll=RET_UNROLL)

    def chunk_out(c, carry):
        rows = chunk(c)
        q = qr_ref[rows, :]
        v = v_ref[rows, :]
        sc = lax.dot_general(q.astype(BF16), kr_ref[rows, :].astype(BF16), (((1,), (1,)), ((), ())),
                             preferred_element_type=F32)
        o = jnp.dot((sc * dsum_ref[...]).astype(BF16), v, preferred_element_type=F32)
        q2 = (jnp.concatenate([q, q], axis=1) * qd_ref[...]).astype(BF16)
        o = o + jnp.dot(q2, st_ref[c], preferred_element_type=F32)
        o = o * lax.rsqrt(jnp.mean(o * o, axis=-1, keepdims=True) + EPS)
        o = o * gain_ref[...] * g_ref[rows, :].astype(F32)
        o_ref[rows, :] = o.astype(o_ref.dtype)
        return carry

    lax.fori_loop(0, n_chunks, chunk_out, 0, unroll=RET_UNROLL)


def _retention(proj, ld, gain, cos_t, sin_t, batch, seq):
    n_chunks = seq // RET_CHUNK
    dk, dv, c_len = RET_QK_DIM, RET_V_DIM, RET_CHUNK
    assert n_chunks % RET_UNROLL == 0
    qk_blk = lambda off: pl.BlockSpec((seq, dk), lambda b, h, off=off: (b, off // dk + h))
    v_blk = lambda off: pl.BlockSpec((seq, dv), lambda b, h, off=off: (b, off // dv + h))
    rope_blk = pl.BlockSpec((seq, dk), lambda b, h: (0, 0))
    return pl.pallas_call(
        functools.partial(_ret_kernel, n_chunks=n_chunks),
        grid=(batch, RET_HEADS),
        in_specs=[pl.BlockSpec(memory_space=pltpu.SMEM),
                  qk_blk(OFF_R_Q), qk_blk(OFF_R_K), v_blk(OFF_R_V), v_blk(OFF_R_G),
                  pl.BlockSpec((None, 1, dv), lambda b, h: (h, 0, 0)), rope_blk, rope_blk],
        out_specs=pl.BlockSpec((seq, dv), lambda b, h: (b, h)),
        out_shape=jax.ShapeDtypeStruct((batch * seq, RET_V_WIDTH), BF16),
        scratch_shapes=[pltpu.VMEM((seq, dk), F32),
                        pltpu.VMEM((seq, dk), F32),
                        pltpu.VMEM((n_chunks, 2 * dk, dv), F32),
                        pltpu.VMEM((n_chunks, 2 * dk, dv), BF16),
                        pltpu.VMEM((c_len, c_len), F32),
                        pltpu.VMEM((c_len, 2 * dk), F32),
                        pltpu.VMEM((c_len, 2 * dk), F32)],
        compiler_params=_params(("arbitrary", "arbitrary")),
        name="retention",
    )(ld, proj, proj, proj, proj, gain.astype(F32).reshape(RET_HEADS, 1, dv), cos_t, sin_t)


def _merge_kernel(a_ref, b_ref, wa_ref, wb_ref, ga_ref, gb_ref, o_ref):
    ya = jnp.dot(a_ref[...], wa_ref[...], preferred_element_type=F32)
    yb = jnp.dot(b_ref[...], wb_ref[...], preferred_element_type=F32)
    o_ref[...] = (ga_ref[...].astype(F32) * ya + gb_ref[...].astype(F32) * yb).astype(o_ref.dtype)


def _merge(a, b, wa, wb, proj):
    t = a.shape[0]
    tm = min(MAX_TILE_M, t)
    ja, jb = OFF_GATE_A // TILE_N, OFF_GATE_B // TILE_N
    return pl.pallas_call(
        _merge_kernel,
        grid=(t // tm, D_MODEL // TILE_N),
        in_specs=[pl.BlockSpec((tm, NA_WIDTH), lambda i, j: (i, 0)),
                  pl.BlockSpec((tm, RET_V_WIDTH), lambda i, j: (i, 0)),
                  pl.BlockSpec((NA_WIDTH, TILE_N), lambda i, j: (0, j)),
                  pl.BlockSpec((RET_V_WIDTH, TILE_N), lambda i, j: (0, j)),
                  pl.BlockSpec((tm, TILE_N), lambda i, j: (i, ja + j)),
                  pl.BlockSpec((tm, TILE_N), lambda i, j: (i, jb + j))],
        out_specs=pl.BlockSpec((tm, TILE_N), lambda i, j: (i, j)),
        out_shape=jax.ShapeDtypeStruct((t, D_MODEL), BF16),
        compiler_params=_params(("arbitrary", "arbitrary")),
        name="merge",
    )(a, b, wa, wb, proj, proj)


def _matmul_kernel(x_ref, w_ref, o_ref):
    o_ref[...] = jnp.dot(x_ref[...], w_ref[...], preferred_element_type=F32).astype(o_ref.dtype)


def _out_proj(merged, w):
    t = merged.shape[0]
    tm = min(MAX_TILE_M, t)
    return pl.pallas_call(
        _matmul_kernel,
        grid=(t // tm, D_MODEL // TILE_N),
        in_specs=[pl.BlockSpec((tm, D_MODEL), lambda i, j: (i, 0)),
                  pl.BlockSpec((D_MODEL, TILE_N), lambda i, j: (0, j))],
        out_specs=pl.BlockSpec((tm, TILE_N), lambda i, j: (i, j)),
        out_shape=jax.ShapeDtypeStruct((t, D_MODEL), BF16),
        compiler_params=_params(("arbitrary", "arbitrary")),
        name="out_proj",
    )(merged, w)


def _post_kernel(x_ref, y_ref, g_ref, o_ref, ob_ref):
    y = y_ref[...].astype(F32)
    ms = jnp.mean(y * y, axis=-1, keepdims=True)
    x1 = x_ref[...] + y * lax.rsqrt(ms + EPS) * g_ref[...]
    o_ref[...] = x1
    ob_ref[...] = x1.astype(ob_ref.dtype)


def _post_norm(x, y, g):
    t = x.shape[0]
    row = pl.BlockSpec((ROW_TILE, D_MODEL), lambda i: (i, 0))
    return pl.pallas_call(
        _post_kernel,
        grid=(t // ROW_TILE,),
        in_specs=[row, row, pl.BlockSpec((1, D_MODEL), lambda i: (0, 0))],
        out_specs=[row, row],
        out_shape=[jax.ShapeDtypeStruct((t, D_MODEL), F32), jax.ShapeDtypeStruct((t, D_MODEL), BF16)],
        compiler_params=_params(("arbitrary",)),
        name="post_norm",
    )(x, y, g.reshape(1, D_MODEL))


def _ple_kernel(xb_ref, wg_ref, p_ref, wp_ref, x_ref, o_ref):
    gate = jax.nn.sigmoid(jnp.dot(xb_ref[...], wg_ref[...], preferred_element_type=F32))
    emb = jnp.dot(p_ref[...].astype(BF16), wp_ref[...], preferred_element_type=F32)
    o_ref[...] = x_ref[...] + gate * emb


def _ple(x1, x1b, p, wg, wp):
    t = x1.shape[0]
    tm = min(MAX_TILE_M // 2, t)
    return pl.pallas_call(
        _ple_kernel,
        grid=(t // tm, D_MODEL // TILE_N),
        in_specs=[pl.BlockSpec((tm, D_MODEL), lambda i, j: (i, 0)),
                  pl.BlockSpec((D_MODEL, TILE_N), lambda i, j: (0, j)),
                  pl.BlockSpec((tm, PLE_DIM), lambda i, j: (i, 0)),
                  pl.BlockSpec((PLE_DIM, TILE_N), lambda i, j: (0, j)),
                  pl.BlockSpec((tm, TILE_N), lambda i, j: (i, j))],
        out_specs=pl.BlockSpec((tm, TILE_N), lambda i, j: (i, j)),
        out_shape=jax.ShapeDtypeStruct((t, D_MODEL), F32),
        compiler_params=_params(("arbitrary", "arbitrary")),
        name="ple",
    )(x1b, wg, p, wp, x1)


def _rope_tables(seq):
    half = RET_QK_DIM // 2
    inv = ROPE_BASE ** (-jnp.arange(half, dtype=F32) / half)
    ang = jnp.arange(seq).astype(F32)[:, None] * inv[None, :]
    cos, sin = jnp.cos(ang), jnp.sin(ang)
    return jnp.concatenate([cos, cos], axis=-1), jnp.concatenate([-sin, sin], axis=-1)


def _trunk(x, p, layers, depth):
    batch, seq, _ = x.shape
    t = batch * seq
    x = x.reshape(t, D_MODEL)
    cos_t, sin_t = _rope_tables(seq)
    for i in range(depth):
        lw = layers[i]
        xn = _rmsnorm_pre(x, lw["ln_pre"])
        proj = _in_proj(xn, lw["w_in"], seq)
        a = _na_attention(proj, lw["na_bias"], batch, seq)
        b = _retention(proj, lw["ret_ld"], lw["ret_gn"], cos_t, sin_t, batch, seq)
        merged = _merge(a, b, lw["w_proj_a"], lw["w_proj_b"], proj)
        y = _out_proj(merged, lw["w_out"])
        x1, x1b = _post_norm(x, y, lw["ln_post"])
        x = _ple(x1, x1b, p[i].reshape(t, PLE_DIM), lw["w_ple_gate"], lw["w_ple"])
    return x.reshape(batch, seq, D_MODEL)


def kernel(x_prompt, x_sample, p_prompt, p_sample, w_in, ln_pre, ln_post, na_rpb, ret_log_decay_fwd,
           ret_log_decay_bwd, ret_gn_gain, w_proj_a, w_proj_b, w_out, w_ple, w_ple_gate):
    depth = w_in.shape[0]
    rows = x_prompt.shape[1] // GRID_W
    layers = []
    for i in range(depth):
        layers.append({
            "w_in": w_in[i].astype(BF16),
            "ln_pre": ln_pre[i].astype(F32),
            "ln_post": ln_post[i].astype(F32),
            "na_bias": _na_bias_table(na_rpb[i], rows),
            "ret_ld": jnp.stack([ret_log_decay_fwd[i], ret_log_decay_bwd[i]]).astype(F32),
            "ret_gn": ret_gn_gain[i],
            "w_proj_a": w_proj_a[i].astype(BF16),
            "w_proj_b": w_proj_b[i].astype(BF16),
            "w_out": w_out[i].astype(BF16),
            "w_ple": w_ple[i].astype(BF16),
            "w_ple_gate": w_ple_gate[i].astype(BF16),
        })
    y_prompt = _trunk(x_prompt, p_prompt, layers, depth)
    y_sample = _trunk(x_sample, p_sample, layers, depth)
    return (y_prompt, y_sample)
```

```python
import functools

import numpy as np
import jax
import jax.numpy as jnp
from jax import lax
from jax.experimental import pallas as pl
from jax.experimental.pallas import tpu as pltpu

D_MODEL = 4096
GRID_W = 64
PLE_DIM = 256
NA_HEADS = 16
NA_HEAD_DIM = 128
NA_WIDTH = NA_HEADS * NA_HEAD_DIM
NA_WIN_ROWS = 8
NA_WIN_COLS = 16
RET_HEADS = 8
RET_QK_DIM = 128
RET_V_DIM = 256
RET_QK_WIDTH = RET_HEADS * RET_QK_DIM
RET_V_WIDTH = RET_HEADS * RET_V_DIM
RET_CHUNK = 128
ROPE_BASE = 10000.0
EPS = 1e-6

OFF_NA_Q = 0
OFF_NA_K = OFF_NA_Q + NA_WIDTH
OFF_NA_V = OFF_NA_K + NA_WIDTH
OFF_NA_G = OFF_NA_V + NA_WIDTH
OFF_R_Q = OFF_NA_G + NA_WIDTH
OFF_R_K = OFF_R_Q + RET_QK_WIDTH
OFF_R_V = OFF_R_K + RET_QK_WIDTH
OFF_R_G = OFF_R_V + RET_V_WIDTH
OFF_GATE_A = OFF_R_G + RET_V_WIDTH
OFF_GATE_B = OFF_GATE_A + D_MODEL
IN_WIDTH = OFF_GATE_B + D_MODEL

TILE_N = 1024
MAX_TILE_M = 1024
ROW_TILE = 256
V7X_VMEM_LIMIT_BYTES = 56 * 1024 * 1024

F32 = jnp.float32
BF16 = jnp.bfloat16


def _params(semantics, vmem=V7X_VMEM_LIMIT_BYTES):
    return pltpu.CompilerParams(dimension_semantics=semantics, vmem_limit_bytes=vmem)


CAST_BLOCK_BYTES = 8 * 1024 * 1024


def _cast_kernel(w_ref, o_ref):
    o_ref[...] = w_ref[...].astype(o_ref.dtype)


def _cast_layer_bf16(w, layer):
    _, k, n = w.shape
    rb = max(r for r in range(16, k + 1, 16) if k % r == 0 and (r * n * 4 <= CAST_BLOCK_BYTES or r == 16))
    return pl.pallas_call(
        _cast_kernel,
        grid=(k // rb,),
        in_specs=[pl.BlockSpec((None, rb, n), lambda i: (layer, i, 0))],
        out_specs=pl.BlockSpec((rb, n), lambda i: (i, 0)),
        out_shape=jax.ShapeDtypeStruct((k, n), BF16),
        compiler_params=_params(("arbitrary",)),
        name="cast_bf16",
    )(w)


def _rmsnorm_kernel(x_ref, g_ref, o_ref):
    x = x_ref[...]
    ms = jnp.mean(x * x, axis=-1, keepdims=True)
    o_ref[...] = (x * lax.rsqrt(ms + EPS) * g_ref[...]).astype(o_ref.dtype)


def _rmsnorm_pre(x, g):
    t = x.shape[0]
    return pl.pallas_call(
        _rmsnorm_kernel,
        grid=(t // ROW_TILE,),
        in_specs=[pl.BlockSpec((ROW_TILE, D_MODEL), lambda i: (i, 0)),
                  pl.BlockSpec((1, D_MODEL), lambda i: (0, 0))],
        out_specs=pl.BlockSpec((ROW_TILE, D_MODEL), lambda i: (i, 0)),
        out_shape=jax.ShapeDtypeStruct((t, D_MODEL), BF16),
        compiler_params=_params(("arbitrary",)),
        name="rmsnorm_pre",
    )(x, g.reshape(1, D_MODEL))


_J_NA_K = OFF_NA_K // TILE_N
_J_NA_G = OFF_NA_G // TILE_N
_J_R_Q = OFF_R_Q // TILE_N
_J_R_K = OFF_R_K // TILE_N
_J_R_G = OFF_R_G // TILE_N
_J_GATE = OFF_GATE_A // TILE_N
assert NA_HEAD_DIM == RET_QK_DIM
QK_SCALE = NA_HEAD_DIM ** -0.5
LOG2E = float(np.log2(np.e))
NA_Q_SCALE = QK_SCALE * LOG2E


def _inproj_kernel(x_ref, w_ref, o_ref):
    j = pl.program_id(1)
    y = jnp.dot(x_ref[...], w_ref[...], preferred_element_type=F32)
    sig = 0.5 * jnp.tanh(0.5 * y) + 0.5
    is_gate = j >= _J_GATE
    is_silu = ((j >= _J_NA_G) & (j < _J_R_Q)) | ((j >= _J_R_G) & (j < _J_GATE))
    scale = jnp.where(j < _J_NA_K, NA_Q_SCALE, jnp.where(j == _J_R_K, QK_SCALE, 1.0))
    mult = jnp.where(is_silu, sig, scale)
    o_ref[...] = jnp.where(is_gate, sig, y * mult).astype(o_ref.dtype)


def _in_proj(xn, w, seq):
    t = xn.shape[0]
    tm = min(MAX_TILE_M, seq)
    return pl.pallas_call(
        _inproj_kernel,
        grid=(t // tm, IN_WIDTH // TILE_N),
        in_specs=[pl.BlockSpec((tm, D_MODEL), lambda i, j: (i, 0)),
                  pl.BlockSpec((D_MODEL, TILE_N), lambda i, j: (0, j))],
        out_specs=pl.BlockSpec((tm, TILE_N), lambda i, j: (i, j)),
        out_shape=jax.ShapeDtypeStruct((t, IN_WIDTH), BF16),
        compiler_params=_params(("arbitrary", "arbitrary")),
        name="in_proj",
    )(xn, w)


NA_PAIR_ROWS = 2
NA_PAIR_TOKENS = NA_PAIR_ROWS * GRID_W
NA_KEY_ROWS = NA_WIN_ROWS + NA_PAIR_ROWS
NA_KEY_CHUNKS = NA_KEY_ROWS // NA_PAIR_ROWS
NA_KEYS = NA_KEY_ROWS * GRID_W
NA_VARIANTS = 5
NA_PAIR_BLOCK = 8
NA_SUM_ROWS = 16


def _na_kernel(q_ref, k_ref, v_ref, g_ref, bias_ref, o_ref, vt_ref, s_ref, p_ref, *, rows):
    n_pairs = rows // NA_PAIR_ROWS
    n_chunks = n_pairs
    ct = NA_PAIR_TOKENS
    nt = (((1,), (1,)), ((), ()))

    def transpose_chunk(c, carry):
        r0 = pl.multiple_of(c * ct, ct)
        vt_ref[c] = v_ref[pl.ds(r0, ct), :].astype(F32).T.astype(BF16)
        return carry

    lax.fori_loop(0, n_chunks, transpose_chunk, 0, unroll=NA_PAIR_BLOCK)
    ones_rows = jnp.ones((NA_SUM_ROWS, NA_KEYS), BF16)

    def block(blk, carry):
        pairs = []
        for u in range(NA_PAIR_BLOCK):
            pr = blk * NA_PAIR_BLOCK + u
            c0 = jnp.clip(pr - NA_WIN_ROWS // 4, 0, n_chunks - NA_KEY_CHUNKS)
            var = jnp.where(pr < 2, pr, jnp.where(pr >= n_pairs - 2, pr - (n_pairs - 2) + 3, 2))
            q0 = pl.multiple_of(pr * ct, ct)
            k0 = pl.multiple_of(c0 * ct, ct)
            pairs.append((c0, q0))
            s = lax.dot_general(k_ref[pl.ds(k0, NA_KEYS), :], q_ref[pl.ds(q0, ct), :], nt,
                                preferred_element_type=F32)
            s_ref[u] = s + bias_ref[var]
        for u in range(NA_PAIR_BLOCK):
            s = s_ref[u]
            m = jnp.max(s, axis=0, keepdims=True)
            p_ref[u] = jnp.exp2(s - m).astype(BF16)
        for u in range(NA_PAIR_BLOCK):
            c0, q0 = pairs[u]
            vt = jnp.concatenate([vt_ref[c0 + c] for c in range(NA_KEY_CHUNKS)], axis=1)
            ot = jnp.dot(jnp.concatenate([vt, ones_rows], axis=0), p_ref[u], preferred_element_type=F32)
            ot = ot[:NA_HEAD_DIM] * (1.0 / ot[NA_HEAD_DIM:NA_HEAD_DIM + 1])
            o = ot.T * g_ref[pl.ds(q0, ct), :].astype(F32)
            o_ref[pl.ds(q0, ct), :] = o.astype(o_ref.dtype)
        return carry

    lax.fori_loop(0, n_pairs // NA_PAIR_BLOCK, block, 0)


def _na_bias_table(rpb, rows):
    n_pairs = rows // NA_PAIR_ROWS
    cols = np.arange(GRID_W)
    col_start = np.clip(cols - NA_WIN_COLS // 2, 0, GRID_W - NA_WIN_COLS)
    col_valid = (cols[None, :] >= col_start[:, None]) & (cols[None, :] < col_start[:, None] + NA_WIN_COLS)
    pad = GRID_W - NA_WIN_COLS
    rp = jnp.pad(rpb.astype(F32), ((0, 0), (0, 0), (pad, pad)))
    toep = jnp.stack([rp[:, :, GRID_W - 1 - qc: 2 * GRID_W - 1 - qc] for qc in range(GRID_W)], axis=2)
    toep = jnp.where(col_valid[None, None], toep, -jnp.inf)
    toep_t = toep.transpose(0, 1, 3, 2)
    masked = jnp.full((NA_HEADS, GRID_W, GRID_W), -jnp.inf, F32)
    tables = []
    for pr in (0, 1, 2, n_pairs - 2, n_pairs - 1):
        base = int(np.clip(NA_PAIR_ROWS * pr - NA_WIN_ROWS // 2, 0, rows - NA_KEY_ROWS))
        slabs = []
        for w in range(NA_KEY_ROWS):
            per_row = []
            for i in range(NA_PAIR_ROWS):
                r_q = NA_PAIR_ROWS * pr + i
                rs = int(np.clip(r_q - NA_WIN_ROWS // 2, 0, rows - NA_WIN_ROWS))
                kr = base + w
                if rs <= kr < rs + NA_WIN_ROWS:
                    per_row.append(toep_t[:, kr - r_q + NA_WIN_ROWS - 1])
                else:
                    per_row.append(masked)
            slabs.append(jnp.concatenate(per_row, axis=-1))
        tables.append(jnp.concatenate(slabs, axis=1))
    return jnp.stack(tables, axis=1) * LOG2E


def _na_attention(proj, bias, batch, seq):
    rows = seq // GRID_W
    n_pairs = rows // NA_PAIR_ROWS
    assert rows % NA_PAIR_ROWS == 0 and rows >= NA_KEY_ROWS + 2 and n_pairs % NA_PAIR_BLOCK == 0
    hd = NA_HEAD_DIM
    blk = lambda off: pl.BlockSpec((seq, hd), lambda b, h, off=off: (b, off // hd + h))
    return pl.pallas_call(
        functools.partial(_na_kernel, rows=rows),
        grid=(batch, NA_HEADS),
        in_specs=[blk(OFF_NA_Q), blk(OFF_NA_K), blk(OFF_NA_V), blk(OFF_NA_G),
                  pl.BlockSpec((None, NA_VARIANTS, NA_KEYS, NA_PAIR_TOKENS), lambda b, h: (h, 0, 0, 0))],
        out_specs=pl.BlockSpec((seq, hd), lambda b, h: (b, h)),
        out_shape=jax.ShapeDtypeStruct((batch * seq, NA_WIDTH), BF16),
        scratch_shapes=[pltpu.VMEM((n_pairs, hd, NA_PAIR_TOKENS), BF16),
                        pltpu.VMEM((NA_PAIR_BLOCK, NA_KEYS, NA_PAIR_TOKENS), F32),
                        pltpu.VMEM((NA_PAIR_BLOCK, NA_KEYS, NA_PAIR_TOKENS), BF16)],
        compiler_params=_params(("arbitrary", "arbitrary")),
        name="na_attention",
    )(proj, proj, proj, proj, bias)


RET_UNROLL = 8


def _ret_kernel(ld_ref, q_ref, k_ref, v_ref, g_ref, gain_ref, cos_ref, sin_ref, o_ref,
                qr_ref, kr_ref, kv_ref, st_ref, dsum_ref, qd_ref, kd_ref, *, n_chunks):
    c_len, dk = RET_CHUNK, RET_QK_DIM
    h = pl.program_id(1)
    ldf = -jnp.abs(ld_ref[0, h])
    ldb = -jnp.abs(ld_ref[1, h])
    row = lax.broadcasted_iota(jnp.int32, (c_len, c_len), 0).astype(F32)
    col = lax.broadcasted_iota(jnp.int32, (c_len, c_len), 1).astype(F32)
    diff = row - col
    dsum_ref[...] = (jnp.where(diff >= 0, jnp.exp(ldf * jnp.maximum(diff, 0.0)), 0.0)
                     + jnp.where(diff <= 0, jnp.exp(ldb * jnp.maximum(-diff, 0.0)), 0.0))
    qd_ref[:, :dk] = jnp.exp(ldf * (row + 1.0))
    qd_ref[:, dk:] = jnp.exp(ldb * (c_len - row))
    kd_ref[:, :dk] = jnp.exp(ldf * (c_len - 1.0 - row))
    kd_ref[:, dk:] = jnp.exp(ldb * row)
    zero_row = jnp.zeros((1, RET_V_DIM), F32)
    cdf = jnp.exp(zero_row + ldf * c_len)
    cdb = jnp.exp(zero_row + ldb * c_len)
    chunk = lambda c: pl.ds(pl.multiple_of(c * c_len, c_len), c_len)

    def rotate(c, carry):
        rows = chunk(c)
        cos, sin = cos_ref[rows, :], sin_ref[rows, :]
        for src, dst in ((q_ref, qr_ref), (k_ref, kr_ref)):
            t = src[rows, :].astype(F32)
            dst[rows, :] = t * cos + pltpu.roll(t, dk // 2, axis=1) * sin
        return carry

    lax.fori_loop(0, n_chunks, rotate, 0, unroll=RET_UNROLL)

    def chunk_kv(c, carry):
        rows = chunk(c)
        k = kr_ref[rows, :]
        k2 = (jnp.concatenate([k, k], axis=1) * kd_ref[...]).astype(BF16)
        kv_ref[c] = lax.dot_general(k2, v_ref[rows, :], (((0,), (0,)), ((), ())),
                                    preferred_element_type=F32)
        return carry

    lax.fori_loop(0, n_chunks, chunk_kv, 0, unroll=RET_UNROLL)

    def scan_fwd(c, s):
        st_ref[c, :dk, :] = s.astype(BF16)
        return s * cdf + kv_ref[c, :dk, :]

    def scan_bwd(t, s):
        c = n_chunks - 1 - t
        st_ref[c, dk:, :] = s.astype(BF16)
        return s * cdb + kv_ref[c, dk:, :]

    zero_state = jnp.zeros((dk, RET_V_DIM), F32)
    lax.fori_loop(0, n_chunks, scan_fwd, zero_state, unroll=RET_UNROLL)
    lax.fori_loop(0, n_chunks, scan_bwd, zero_state, unroll=RET_UNROLL)

    def chunk_out(c, carry):
        rows = chunk(c)
        q = qr_ref[rows, :]
        v = v_ref[rows, :]
        sc = lax.dot_general(q.astype(BF16), kr_ref[rows, :].astype(BF16), (((1,), (1,)), ((), ())),
                             preferred_element_type=F32)
        o = jnp.dot((sc * dsum_ref[...]).astype(BF16), v, preferred_element_type=F32)
        q2 = (jnp.concatenate([q, q], axis=1) * qd_ref[...]).astype(BF16)
        o = o + jnp.dot(q2, st_ref[c], preferred_element_type=F32)
        o = o * lax.rsqrt(jnp.mean(o * o, axis=-1, keepdims=True) + EPS)
        o = o * gain_ref[...] * g_ref[rows, :].astype(F32)
        o_ref[rows, :] = o.astype(o_ref.dtype)
        return carry

    lax.fori_loop(0, n_chunks, chunk_out, 0, unroll=RET_UNROLL)


def _retention(proj, ld, gain, cos_t, sin_t, batch, seq):
    n_chunks = seq // RET_CHUNK
    dk, dv, c_len = RET_QK_DIM, RET_V_DIM, RET_CHUNK
    assert n_chunks % RET_UNROLL == 0
    qk_blk = lambda off: pl.BlockSpec((seq, dk), lambda b, h, off=off: (b, off // dk + h))
    v_blk = lambda off: pl.BlockSpec((seq, dv), lambda b, h, off=off: (b, off // dv + h))
    rope_blk = pl.BlockSpec((seq, dk), lambda b, h: (0, 0))
    return pl.pallas_call(
        functools.partial(_ret_kernel, n_chunks=n_chunks),
        grid=(batch, RET_HEADS),
        in_specs=[pl.BlockSpec(memory_space=pltpu.SMEM),
                  qk_blk(OFF_R_Q), qk_blk(OFF_R_K), v_blk(OFF_R_V), v_blk(OFF_R_G),
                  pl.BlockSpec((None, 1, dv), lambda b, h: (h, 0, 0)), rope_blk, rope_blk],
        out_specs=pl.BlockSpec((seq, dv), lambda b, h: (b, h)),
        out_shape=jax.ShapeDtypeStruct((batch * seq, RET_V_WIDTH), BF16),
        scratch_shapes=[pltpu.VMEM((seq, dk), F32),
                        pltpu.VMEM((seq, dk), F32),
                        pltpu.VMEM((n_chunks, 2 * dk, dv), F32),
                        pltpu.VMEM((n_chunks, 2 * dk, dv), BF16),
                        pltpu.VMEM((c_len, c_len), F32),
                        pltpu.VMEM((c_len, 2 * dk), F32),
                        pltpu.VMEM((c_len, 2 * dk), F32)],
        compiler_params=_params(("arbitrary", "arbitrary")),
        name="retention",
    )(ld, proj, proj, proj, proj, gain.astype(F32).reshape(RET_HEADS, 1, dv), cos_t, sin_t)


def _merge_kernel(a_ref, b_ref, wa_ref, wb_ref, ga_ref, gb_ref, o_ref):
    ya = jnp.dot(a_ref[...], wa_ref[...], preferred_element_type=F32)
    yb = jnp.dot(b_ref[...], wb_ref[...], preferred_element_type=F32)
    o_ref[...] = (ga_ref[...].astype(F32) * ya + gb_ref[...].astype(F32) * yb).astype(o_ref.dtype)


def _merge(a, b, wa, wb, proj):
    t = a.shape[0]
    tm = min(MAX_TILE_M, t)
    ja, jb = OFF_GATE_A // TILE_N, OFF_GATE_B // TILE_N
    return pl.pallas_call(
        _merge_kernel,
        grid=(t // tm, D_MODEL // TILE_N),
        in_specs=[pl.BlockSpec((tm, NA_WIDTH), lambda i, j: (i, 0)),
                  pl.BlockSpec((tm, RET_V_WIDTH), lambda i, j: (i, 0)),
                  pl.BlockSpec((NA_WIDTH, TILE_N), lambda i, j: (0, j)),
                  pl.BlockSpec((RET_V_WIDTH, TILE_N), lambda i, j: (0, j)),
                  pl.BlockSpec((tm, TILE_N), lambda i, j: (i, ja + j)),
                  pl.BlockSpec((tm, TILE_N), lambda i, j: (i, jb + j))],
        out_specs=pl.BlockSpec((tm, TILE_N), lambda i, j: (i, j)),
        out_shape=jax.ShapeDtypeStruct((t, D_MODEL), BF16),
        compiler_params=_params(("arbitrary", "arbitrary")),
        name="merge",
    )(a, b, wa, wb, proj, proj)


def _matmul_kernel(x_ref, w_ref, o_ref):
    o_ref[...] = jnp.dot(x_ref[...], w_ref[...], preferred_element_type=F32).astype(o_ref.dtype)


def _out_proj(merged, w):
    t = merged.shape[0]
    tm = min(MAX_TILE_M, t)
    return pl.pallas_call(
        _matmul_kernel,
        grid=(t // tm, D_MODEL // TILE_N),
        in_specs=[pl.BlockSpec((tm, D_MODEL), lambda i, j: (i, 0)),
                  pl.BlockSpec((D_MODEL, TILE_N), lambda i, j: (0, j))],
        out_specs=pl.BlockSpec((tm, TILE_N), lambda i, j: (i, j)),
        out_shape=jax.ShapeDtypeStruct((t, D_MODEL), BF16),
        compiler_params=_params(("arbitrary", "arbitrary")),
        name="out_proj",
    )(merged, w)


LANES = 128


def _post_kernel(x_ref, y_ref, g_ref, ob_ref, rs_ref):
    y = y_ref[...].astype(F32)
    rs = lax.rsqrt(jnp.mean(y * y, axis=-1, keepdims=True) + EPS)
    ob_ref[...] = (x_ref[...] + y * rs * g_ref[...]).astype(ob_ref.dtype)
    rs_ref[...] = jnp.broadcast_to(rs, rs_ref.shape)


def _post_norm(x, y, g):
    t = x.shape[0]
    row = pl.BlockSpec((ROW_TILE, D_MODEL), lambda i: (i, 0))
    return pl.pallas_call(
        _post_kernel,
        grid=(t // ROW_TILE,),
        in_specs=[row, row, pl.BlockSpec((1, D_MODEL), lambda i: (0, 0))],
        out_specs=[row, pl.BlockSpec((ROW_TILE, LANES), lambda i: (i, 0))],
        out_shape=[jax.ShapeDtypeStruct((t, D_MODEL), BF16), jax.ShapeDtypeStruct((t, LANES), F32)],
        compiler_params=_params(("arbitrary",)),
        name="post_norm",
    )(x, y, g.reshape(1, D_MODEL))


PLE_TILE_N = 512


def _ple_kernel(xb_ref, wg_ref, p_ref, wp_ref, x_ref, y_ref, rs_ref, g_ref, o_ref):
    z = jnp.dot(xb_ref[...], wg_ref[...], preferred_element_type=F32)
    gate = 0.5 * jnp.tanh(0.5 * z) + 0.5
    upd = gate * jnp.dot(p_ref[...].astype(BF16), wp_ref[...], preferred_element_type=F32)
    rs = rs_ref[...]
    for c in range(PLE_TILE_N // LANES):
        sl = slice(c * LANES, (c + 1) * LANES)
        x1 = x_ref[:, sl] + y_ref[:, sl].astype(F32) * rs * g_ref[:, sl]
        o_ref[:, sl] = x1 + upd[:, sl]


def _ple(x, y, rs, g, x1b, p, wg, wp):
    t = x.shape[0]
    tm, tn = min(MAX_TILE_M, t), PLE_TILE_N
    tile = pl.BlockSpec((tm, tn), lambda i, j: (i, j))
    return pl.pallas_call(
        _ple_kernel,
        grid=(t // tm, D_MODEL // tn),
        in_specs=[pl.BlockSpec((tm, D_MODEL), lambda i, j: (i, 0)),
                  pl.BlockSpec((D_MODEL, tn), lambda i, j: (0, j)),
                  pl.BlockSpec((tm, PLE_DIM), lambda i, j: (i, 0)),
                  pl.BlockSpec((PLE_DIM, tn), lambda i, j: (0, j)),
                  tile, tile,
                  pl.BlockSpec((tm, LANES), lambda i, j: (i, 0)),
                  pl.BlockSpec((1, tn), lambda i, j: (0, j))],
        out_specs=tile,
        out_shape=jax.ShapeDtypeStruct((t, D_MODEL), F32),
        compiler_params=_params(("arbitrary", "arbitrary")),
        name="ple",
    )(x1b, wg, p, wp, x, y, rs, g.reshape(1, D_MODEL))


def _rope_tables(seq):
    half = RET_QK_DIM // 2
    inv = ROPE_BASE ** (-jnp.arange(half, dtype=F32) / half)
    ang = jnp.arange(seq).astype(F32)[:, None] * inv[None, :]
    cos, sin = jnp.cos(ang), jnp.sin(ang)
    return jnp.concatenate([cos, cos], axis=-1), jnp.concatenate([-sin, sin], axis=-1)


def _trunk(x, p, layers, depth):
    batch, seq, _ = x.shape
    t = batch * seq
    x = x.reshape(t, D_MODEL)
    cos_t, sin_t = _rope_tables(seq)
    for i in range(depth):
        lw = layers[i]
        xn = _rmsnorm_pre(x, lw["ln_pre"])
        proj = _in_proj(xn, lw["w_in"], seq)
        a = _na_attention(proj, lw["na_bias"], batch, seq)
        b = _retention(proj, lw["ret_ld"], lw["ret_gn"], cos_t, sin_t, batch, seq)
        merged = _merge(a, b, lw["w_proj_a"], lw["w_proj_b"], proj)
        y = _out_proj(merged, lw["w_out"])
        x1b, rs = _post_norm(x, y, lw["ln_post"])
        x = _ple(x, y, rs, lw["ln_post"], x1b, p[i].reshape(t, PLE_DIM), lw["w_ple_gate"], lw["w_ple"])
    return x.reshape(batch, seq, D_MODEL)


def kernel(x_prompt, x_sample, p_prompt, p_sample, w_in, ln_pre, ln_post, na_rpb, ret_log_decay_fwd,
           ret_log_decay_bwd, ret_gn_gain, w_proj_a, w_proj_b, w_out, w_ple, w_ple_gate):
    depth = w_in.shape[0]
    rows = x_prompt.shape[1] // GRID_W
    layers = []
    for i in range(depth):
        layers.append({
            "w_in": _cast_layer_bf16(w_in, i),
            "ln_pre": ln_pre[i].astype(F32),
            "ln_post": ln_post[i].astype(F32),
            "na_bias": _na_bias_table(na_rpb[i], rows),
            "ret_ld": jnp.stack([ret_log_decay_fwd[i], ret_log_decay_bwd[i]]).astype(F32),
            "ret_gn": ret_gn_gain[i],
            "w_proj_a": _cast_layer_bf16(w_proj_a, i),
            "w_proj_b": _cast_layer_bf16(w_proj_b, i),
            "w_out": _cast_layer_bf16(w_out, i),
            "w_ple": _cast_layer_bf16(w_ple, i),
            "w_ple_gate": _cast_layer_bf16(w_ple_gate, i),
        })
    y_prompt = _trunk(x_prompt, p_prompt, layers, depth)
    y_sample = _trunk(x_sample, p_sample, layers, depth)
    return (y_prompt, y_sample)
```

```python
import functools

import numpy as np
import jax
import jax.numpy as jnp
from jax import lax
from jax.experimental import pallas as pl
from jax.experimental.pallas import tpu as pltpu

D_MODEL = 4096
GRID_W = 64
PLE_DIM = 256
NA_HEADS = 16
NA_HEAD_DIM = 128
NA_WIDTH = NA_HEADS * NA_HEAD_DIM
NA_WIN_ROWS = 8
NA_WIN_COLS = 16
RET_HEADS = 8
RET_QK_DIM = 128
RET_V_DIM = 256
RET_QK_WIDTH = RET_HEADS * RET_QK_DIM
RET_V_WIDTH = RET_HEADS * RET_V_DIM
RET_CHUNK = 128
ROPE_BASE = 10000.0
EPS = 1e-6

OFF_NA_Q = 0
OFF_NA_K = OFF_NA_Q + NA_WIDTH
OFF_NA_V = OFF_NA_K + NA_WIDTH
OFF_NA_G = OFF_NA_V + NA_WIDTH
OFF_R_Q = OFF_NA_G + NA_WIDTH
OFF_R_K = OFF_R_Q + RET_QK_WIDTH
OFF_R_V = OFF_R_K + RET_QK_WIDTH
OFF_R_G = OFF_R_V + RET_V_WIDTH
OFF_GATE_A = OFF_R_G + RET_V_WIDTH
OFF_GATE_B = OFF_GATE_A + D_MODEL
IN_WIDTH = OFF_GATE_B + D_MODEL

TILE_N = 1024
MAX_TILE_M = 1024
ROW_TILE = 256
V7X_VMEM_LIMIT_BYTES = 56 * 1024 * 1024
LANES = 128

F32 = jnp.float32
BF16 = jnp.bfloat16


def _params(semantics, vmem=V7X_VMEM_LIMIT_BYTES):
    return pltpu.CompilerParams(dimension_semantics=semantics, vmem_limit_bytes=vmem)


CAST_BLOCK_BYTES = 8 * 1024 * 1024


def _cast_kernel(w_ref, o_ref):
    o_ref[...] = w_ref[...].astype(o_ref.dtype)


def _cast_layer_bf16(w, layer):
    _, k, n = w.shape
    rb = max(r for r in range(16, k + 1, 16) if k % r == 0 and (r * n * 4 <= CAST_BLOCK_BYTES or r == 16))
    return pl.pallas_call(
        _cast_kernel,
        grid=(k // rb,),
        in_specs=[pl.BlockSpec((None, rb, n), lambda i: (layer, i, 0))],
        out_specs=pl.BlockSpec((rb, n), lambda i: (i, 0)),
        out_shape=jax.ShapeDtypeStruct((k, n), BF16),
        compiler_params=_params(("arbitrary",)),
        name="cast_bf16",
    )(w)


SUBLANES_BF16 = 16


SIDE_CAST_MAX_BYTES = 1024 * 1024


def _side_cast_fits(w, ni, nj):
    _, k, n = w.shape
    return (k % ni == 0 and n % nj == 0 and (k // ni) % SUBLANES_BF16 == 0 and (n // nj) % LANES == 0
            and (k // ni) * (n // nj) * 4 <= SIDE_CAST_MAX_BYTES)


def _side_cast_specs(ws, layer, ni, nj):
    in_specs, out_specs, out_shapes = [], [], []
    for w in ws:
        _, k, n = w.shape
        blk = (k // ni, n // nj)
        in_specs.append(pl.BlockSpec((None,) + blk, lambda i, j: (layer, i, j)))
        out_specs.append(pl.BlockSpec(blk, lambda i, j: (i, j)))
        out_shapes.append(jax.ShapeDtypeStruct((k, n), BF16))
    return in_specs, out_specs, out_shapes


def _split_side_refs(refs, n_out):
    n = (len(refs) - n_out) // 2
    return refs[:n], refs[n:n + n_out], refs[n + n_out:]


def _side_cast(src_refs, dst_refs):
    for src, dst in zip(src_refs, dst_refs):
        dst[...] = src[...].astype(dst.dtype)


def _rmsnorm_kernel(x_ref, g_ref, o_ref):
    x = x_ref[...]
    ms = jnp.mean(x * x, axis=-1, keepdims=True)
    o_ref[...] = (x * lax.rsqrt(ms + EPS) * g_ref[...]).astype(o_ref.dtype)


def _rmsnorm_pre(x, g):
    t = x.shape[0]
    return pl.pallas_call(
        _rmsnorm_kernel,
        grid=(t // ROW_TILE,),
        in_specs=[pl.BlockSpec((ROW_TILE, D_MODEL), lambda i: (i, 0)),
                  pl.BlockSpec((1, D_MODEL), lambda i: (0, 0))],
        out_specs=pl.BlockSpec((ROW_TILE, D_MODEL), lambda i: (i, 0)),
        out_shape=jax.ShapeDtypeStruct((t, D_MODEL), BF16),
        compiler_params=_params(("arbitrary",)),
        name="rmsnorm_pre",
    )(x, g.reshape(1, D_MODEL))


_J_NA_K = OFF_NA_K // TILE_N
_J_NA_G = OFF_NA_G // TILE_N
_J_R_Q = OFF_R_Q // TILE_N
_J_R_K = OFF_R_K // TILE_N
_J_R_G = OFF_R_G // TILE_N
_J_GATE = OFF_GATE_A // TILE_N
assert NA_HEAD_DIM == RET_QK_DIM
QK_SCALE = NA_HEAD_DIM ** -0.5
LOG2E = float(np.log2(np.e))
NA_Q_SCALE = QK_SCALE * LOG2E


def _inproj_kernel(x_ref, w_ref, *refs):
    cast_src, (o_ref,), cast_dst = _split_side_refs(refs, 1)
    _side_cast(cast_src, cast_dst)
    j = pl.program_id(1)
    y = jnp.dot(x_ref[...], w_ref[...], preferred_element_type=F32)
    is_gate = j >= _J_GATE
    is_silu = ((j >= _J_NA_G) & (j < _J_R_Q)) | ((j >= _J_R_G) & (j < _J_GATE))
    uses_sigmoid = is_gate | is_silu
    scale = jnp.where(j < _J_NA_K, NA_Q_SCALE, jnp.where(j == _J_R_K, QK_SCALE, 1.0))
    a = jnp.where(uses_sigmoid, 0.5, scale)
    b = jnp.where(uses_sigmoid, 0.5, 0.0)
    u = jnp.where(is_gate, 1.0, y)
    o_ref[...] = (u * (a + b * jnp.tanh(0.5 * y))).astype(o_ref.dtype)


def _in_proj(xn, w, seq, side_ws=(), side_layer=0):
    t = xn.shape[0]
    tm = min(MAX_TILE_M, seq)
    grid = (t // tm, IN_WIDTH // TILE_N)
    side_in, side_out, side_shapes = _side_cast_specs(side_ws, side_layer, *grid)
    return pl.pallas_call(
        _inproj_kernel,
        grid=grid,
        in_specs=[pl.BlockSpec((tm, D_MODEL), lambda i, j: (i, 0)),
                  pl.BlockSpec((D_MODEL, TILE_N), lambda i, j: (0, j))] + side_in,
        out_specs=[pl.BlockSpec((tm, TILE_N), lambda i, j: (i, j))] + side_out,
        out_shape=[jax.ShapeDtypeStruct((t, IN_WIDTH), BF16)] + side_shapes,
        compiler_params=_params(("arbitrary", "arbitrary")),
        name="in_proj",
    )(xn, w, *side_ws)


NA_PAIR_ROWS = 2
NA_PAIR_TOKENS = NA_PAIR_ROWS * GRID_W
NA_KEY_ROWS = NA_WIN_ROWS + NA_PAIR_ROWS
NA_KEY_CHUNKS = NA_KEY_ROWS // NA_PAIR_ROWS
NA_KEYS = NA_KEY_ROWS * GRID_W
NA_VARIANTS = 5
NA_PAIR_BLOCK = 16
NA_SUM_ROWS = 16


def _na_kernel(q_ref, k_ref, v_ref, g_ref, bias_ref, o_ref, vt_ref, s_ref, p_ref, *, rows):
    n_pairs = rows // NA_PAIR_ROWS
    n_chunks = n_pairs
    ct = NA_PAIR_TOKENS
    nt = (((1,), (1,)), ((), ()))
    pair_block = s_ref.shape[0]

    def transpose_chunk(c, carry):
        r0 = pl.multiple_of(c * ct, ct)
        vt_ref[c] = v_ref[pl.ds(r0, ct), :].astype(F32).T.astype(BF16)
        return carry

    lax.fori_loop(0, n_chunks, transpose_chunk, 0, unroll=pair_block)
    ones_rows = jnp.ones((NA_SUM_ROWS, NA_KEYS), BF16)

    def block(blk, carry):
        pairs = []
        for u in range(pair_block):
            pr = blk * pair_block + u
            c0 = jnp.clip(pr - NA_WIN_ROWS // 4, 0, n_chunks - NA_KEY_CHUNKS)
            var = jnp.where(pr < 2, pr, jnp.where(pr >= n_pairs - 2, pr - (n_pairs - 2) + 3, 2))
            q0 = pl.multiple_of(pr * ct, ct)
            k0 = pl.multiple_of(c0 * ct, ct)
            pairs.append((c0, q0))
            s = lax.dot_general(k_ref[pl.ds(k0, NA_KEYS), :], q_ref[pl.ds(q0, ct), :], nt,
                                preferred_element_type=F32)
            s_ref[u] = s + bias_ref[var]
        for u in range(pair_block):
            s = s_ref[u]
            m = jnp.max(s, axis=0, keepdims=True)
            p_ref[u] = jnp.exp2(s - m).astype(BF16)
        for u in range(pair_block):
            c0, q0 = pairs[u]
            vt = jnp.concatenate([vt_ref[c0 + c] for c in range(NA_KEY_CHUNKS)], axis=1)
            ot = jnp.dot(jnp.concatenate([vt, ones_rows], axis=0), p_ref[u], preferred_element_type=F32)
            ot = ot[:NA_HEAD_DIM] * (1.0 / ot[NA_HEAD_DIM:NA_HEAD_DIM + 1])
            o = ot.T * g_ref[pl.ds(q0, ct), :].astype(F32)
            o_ref[pl.ds(q0, ct), :] = o.astype(o_ref.dtype)
        return carry

    lax.fori_loop(0, n_pairs // pair_block, block, 0)


def _na_bias_table(rpb, rows):
    n_pairs = rows // NA_PAIR_ROWS
    cols = np.arange(GRID_W)
    col_start = np.clip(cols - NA_WIN_COLS // 2, 0, GRID_W - NA_WIN_COLS)
    col_valid = (cols[None, :] >= col_start[:, None]) & (cols[None, :] < col_start[:, None] + NA_WIN_COLS)
    pad = GRID_W - NA_WIN_COLS
    rp = jnp.pad(rpb.astype(F32), ((0, 0), (0, 0), (pad, pad)))
    toep = jnp.stack([rp[:, :, GRID_W - 1 - qc: 2 * GRID_W - 1 - qc] for qc in range(GRID_W)], axis=2)
    toep = jnp.where(col_valid[None, None], toep, -jnp.inf)
    toep_t = toep.transpose(0, 1, 3, 2)
    masked = jnp.full((NA_HEADS, GRID_W, GRID_W), -jnp.inf, F32)
    tables = []
    for pr in (0, 1, 2, n_pairs - 2, n_pairs - 1):
        base = int(np.clip(NA_PAIR_ROWS * pr - NA_WIN_ROWS // 2, 0, rows - NA_KEY_ROWS))
        slabs = []
        for w in range(NA_KEY_ROWS):
            per_row = []
            for i in range(NA_PAIR_ROWS):
                r_q = NA_PAIR_ROWS * pr + i
                rs = int(np.clip(r_q - NA_WIN_ROWS // 2, 0, rows - NA_WIN_ROWS))
                kr = base + w
                if rs <= kr < rs + NA_WIN_ROWS:
                    per_row.append(toep_t[:, kr - r_q + NA_WIN_ROWS - 1])
                else:
                    per_row.append(masked)
            slabs.append(jnp.concatenate(per_row, axis=-1))
        tables.append(jnp.concatenate(slabs, axis=1))
    return jnp.stack(tables, axis=1) * LOG2E


def _na_attention(proj, bias, batch, seq):
    rows = seq // GRID_W
    n_pairs = rows // NA_PAIR_ROWS
    pair_block = min(NA_PAIR_BLOCK, n_pairs)
    assert rows % NA_PAIR_ROWS == 0 and rows >= NA_KEY_ROWS + 2 and n_pairs % pair_block == 0
    hd = NA_HEAD_DIM
    blk = lambda off: pl.BlockSpec((seq, hd), lambda b, h, off=off: (b, off // hd + h))
    return pl.pallas_call(
        functools.partial(_na_kernel, rows=rows),
        grid=(batch, NA_HEADS),
        in_specs=[blk(OFF_NA_Q), blk(OFF_NA_K), blk(OFF_NA_V), blk(OFF_NA_G),
                  pl.BlockSpec((None, NA_VARIANTS, NA_KEYS, NA_PAIR_TOKENS), lambda b, h: (h, 0, 0, 0))],
        out_specs=pl.BlockSpec((seq, hd), lambda b, h: (b, h)),
        out_shape=jax.ShapeDtypeStruct((batch * seq, NA_WIDTH), BF16),
        scratch_shapes=[pltpu.VMEM((n_pairs, hd, NA_PAIR_TOKENS), BF16),
                        pltpu.VMEM((pair_block, NA_KEYS, NA_PAIR_TOKENS), F32),
                        pltpu.VMEM((pair_block, NA_KEYS, NA_PAIR_TOKENS), BF16)],
        compiler_params=_params(("arbitrary", "arbitrary")),
        name="na_attention",
    )(proj, proj, proj, proj, bias)


RET_UNROLL = 16


def _ret_kernel(ld_ref, q_ref, k_ref, v_ref, g_ref, gain_ref, cos_ref, sin_ref, o_ref,
                qr_ref, kr_ref, kv_ref, st_ref, dsum_ref, qd_ref, kd_ref, *, n_chunks):
    c_len, dk = RET_CHUNK, RET_QK_DIM
    unroll = min(RET_UNROLL, n_chunks)
    h = pl.program_id(1)
    ldf = -jnp.abs(ld_ref[0, h])
    ldb = -jnp.abs(ld_ref[1, h])
    row = lax.broadcasted_iota(jnp.int32, (c_len, c_len), 0).astype(F32)
    col = lax.broadcasted_iota(jnp.int32, (c_len, c_len), 1).astype(F32)
    diff = row - col
    dsum_ref[...] = (jnp.where(diff >= 0, jnp.exp(ldf * jnp.maximum(diff, 0.0)), 0.0)
                     + jnp.where(diff <= 0, jnp.exp(ldb * jnp.maximum(-diff, 0.0)), 0.0))
    qd_ref[:, :dk] = jnp.exp(ldf * (row + 1.0))
    qd_ref[:, dk:] = jnp.exp(ldb * (c_len - row))
    kd_ref[:, :dk] = jnp.exp(ldf * (c_len - 1.0 - row))
    kd_ref[:, dk:] = jnp.exp(ldb * row)
    zero_row = jnp.zeros((1, RET_V_DIM), F32)
    cdf = jnp.exp(zero_row + ldf * c_len)
    cdb = jnp.exp(zero_row + ldb * c_len)
    chunk = lambda c: pl.ds(pl.multiple_of(c * c_len, c_len), c_len)

    def rotate(c, carry):
        rows = chunk(c)
        cos, sin = cos_ref[rows, :], sin_ref[rows, :]
        for src, dst in ((q_ref, qr_ref), (k_ref, kr_ref)):
            t = src[rows, :].astype(F32)
            dst[rows, :] = t * cos + pltpu.roll(t, dk // 2, axis=1) * sin
        return carry

    lax.fori_loop(0, n_chunks, rotate, 0, unroll=unroll)

    def chunk_kv(c, carry):
        rows = chunk(c)
        k = kr_ref[rows, :]
        k2 = (jnp.concatenate([k, k], axis=1) * kd_ref[...]).astype(BF16)
        kv_ref[c] = lax.dot_general(k2, v_ref[rows, :], (((0,), (0,)), ((), ())),
                                    preferred_element_type=F32)
        return carry

    lax.fori_loop(0, n_chunks, chunk_kv, 0, unroll=unroll)

    def scan_fwd(c, s):
        st_ref[c, :dk, :] = s.astype(BF16)
        return s * cdf + kv_ref[c, :dk, :]

    def scan_bwd(t, s):
        c = n_chunks - 1 - t
        st_ref[c, dk:, :] = s.astype(BF16)
        return s * cdb + kv_ref[c, dk:, :]

    zero_state = jnp.zeros((dk, RET_V_DIM), F32)
    lax.fori_loop(0, n_chunks, scan_fwd, zero_state, unroll=unroll)
    lax.fori_loop(0, n_chunks, scan_bwd, zero_state, unroll=unroll)

    def chunk_out(c, carry):
        rows = chunk(c)
        q = qr_ref[rows, :]
        v = v_ref[rows, :]
        sc = lax.dot_general(q.astype(BF16), kr_ref[rows, :].astype(BF16), (((1,), (1,)), ((), ())),
                             preferred_element_type=F32)
        o = jnp.dot((sc * dsum_ref[...]).astype(BF16), v, preferred_element_type=F32)
        q2 = (jnp.concatenate([q, q], axis=1) * qd_ref[...]).astype(BF16)
        o = o + jnp.dot(q2, st_ref[c], preferred_element_type=F32)
        o = o * lax.rsqrt(jnp.mean(o * o, axis=-1, keepdims=True) + EPS)
        o = o * gain_ref[...] * g_ref[rows, :].astype(F32)
        o_ref[rows, :] = o.astype(o_ref.dtype)
        return carry

    lax.fori_loop(0, n_chunks, chunk_out, 0, unroll=unroll)


def _retention(proj, ld, gain, cos_t, sin_t, batch, seq):
    n_chunks = seq // RET_CHUNK
    dk, dv, c_len = RET_QK_DIM, RET_V_DIM, RET_CHUNK
    assert n_chunks % min(RET_UNROLL, n_chunks) == 0
    qk_blk = lambda off: pl.BlockSpec((seq, dk), lambda b, h, off=off: (b, off // dk + h))
    v_blk = lambda off: pl.BlockSpec((seq, dv), lambda b, h, off=off: (b, off // dv + h))
    rope_blk = pl.BlockSpec((seq, dk), lambda b, h: (0, 0))
    return pl.pallas_call(
        functools.partial(_ret_kernel, n_chunks=n_chunks),
        grid=(batch, RET_HEADS),
        in_specs=[pl.BlockSpec(memory_space=pltpu.SMEM),
                  qk_blk(OFF_R_Q), qk_blk(OFF_R_K), v_blk(OFF_R_V), v_blk(OFF_R_G),
                  pl.BlockSpec((None, 1, dv), lambda b, h: (h, 0, 0)), rope_blk, rope_blk],
        out_specs=pl.BlockSpec((seq, dv), lambda b, h: (b, h)),
        out_shape=jax.ShapeDtypeStruct((batch * seq, RET_V_WIDTH), BF16),
        scratch_shapes=[pltpu.VMEM((seq, dk), F32),
                        pltpu.VMEM((seq, dk), F32),
                        pltpu.VMEM((n_chunks, 2 * dk, dv), F32),
                        pltpu.VMEM((n_chunks, 2 * dk, dv), BF16),
                        pltpu.VMEM((c_len, c_len), F32),
                        pltpu.VMEM((c_len, 2 * dk), F32),
                        pltpu.VMEM((c_len, 2 * dk), F32)],
        compiler_params=_params(("arbitrary", "arbitrary")),
        name="retention",
    )(ld, proj, proj, proj, proj, gain.astype(F32).reshape(RET_HEADS, 1, dv), cos_t, sin_t)


def _merge_kernel(a_ref, b_ref, wa_ref, wb_ref, ga_ref, gb_ref, o_ref):
    ya = jnp.dot(a_ref[...], wa_ref[...], preferred_element_type=F32)
    yb = jnp.dot(b_ref[...], wb_ref[...], preferred_element_type=F32)
    o_ref[...] = (ga_ref[...].astype(F32) * ya + gb_ref[...].astype(F32) * yb).astype(o_ref.dtype)


def _merge(a, b, wa, wb, proj):
    t = a.shape[0]
    tm = min(MAX_TILE_M, t)
    ja, jb = OFF_GATE_A // TILE_N, OFF_GATE_B // TILE_N
    return pl.pallas_call(
        _merge_kernel,
        grid=(t // tm, D_MODEL // TILE_N),
        in_specs=[pl.BlockSpec((tm, NA_WIDTH), lambda i, j: (i, 0)),
                  pl.BlockSpec((tm, RET_V_WIDTH), lambda i, j: (i, 0)),
                  pl.BlockSpec((NA_WIDTH, TILE_N), lambda i, j: (0, j)),
                  pl.BlockSpec((RET_V_WIDTH, TILE_N), lambda i, j: (0, j)),
                  pl.BlockSpec((tm, TILE_N), lambda i, j: (i, ja + j)),
                  pl.BlockSpec((tm, TILE_N), lambda i, j: (i, jb + j))],
        out_specs=pl.BlockSpec((tm, TILE_N), lambda i, j: (i, j)),
        out_shape=jax.ShapeDtypeStruct((t, D_MODEL), BF16),
        compiler_params=_params(("arbitrary", "arbitrary")),
        name="merge",
    )(a, b, wa, wb, proj, proj)


def _matmul_kernel(x_ref, w_ref, *refs):
    cast_src, (o_ref,), cast_dst = _split_side_refs(refs, 1)
    _side_cast(cast_src, cast_dst)
    o_ref[...] = jnp.dot(x_ref[...], w_ref[...], preferred_element_type=F32).astype(o_ref.dtype)


def _out_proj(merged, w, side_ws=(), side_layer=0):
    t = merged.shape[0]
    tm = min(MAX_TILE_M, t)
    grid = (t // tm, D_MODEL // TILE_N)
    side_in, side_out, side_shapes = _side_cast_specs(side_ws, side_layer, *grid)
    return pl.pallas_call(
        _matmul_kernel,
        grid=grid,
        in_specs=[pl.BlockSpec((tm, D_MODEL), lambda i, j: (i, 0)),
                  pl.BlockSpec((D_MODEL, TILE_N), lambda i, j: (0, j))] + side_in,
        out_specs=[pl.BlockSpec((tm, TILE_N), lambda i, j: (i, j))] + side_out,
        out_shape=[jax.ShapeDtypeStruct((t, D_MODEL), BF16)] + side_shapes,
        compiler_params=_params(("arbitrary", "arbitrary")),
        name="out_proj",
    )(merged, w, *side_ws)


def _post_kernel(x_ref, y_ref, g_ref, ob_ref, rs_ref):
    y = y_ref[...].astype(F32)
    rs = lax.rsqrt(jnp.mean(y * y, axis=-1, keepdims=True) + EPS)
    ob_ref[...] = (x_ref[...] + y * rs * g_ref[...]).astype(ob_ref.dtype)
    rs_ref[...] = jnp.broadcast_to(rs, rs_ref.shape)


def _post_norm(x, y, g):
    t = x.shape[0]
    row = pl.BlockSpec((ROW_TILE, D_MODEL), lambda i: (i, 0))
    return pl.pallas_call(
        _post_kernel,
        grid=(t // ROW_TILE,),
        in_specs=[row, row, pl.BlockSpec((1, D_MODEL), lambda i: (0, 0))],
        out_specs=[row, pl.BlockSpec((ROW_TILE, LANES), lambda i: (i, 0))],
        out_shape=[jax.ShapeDtypeStruct((t, D_MODEL), BF16), jax.ShapeDtypeStruct((t, LANES), F32)],
        compiler_params=_params(("arbitrary",)),
        name="post_norm",
    )(x, y, g.reshape(1, D_MODEL))


PLE_TILE_N = 512


def _ple_kernel(xb_ref, wg_ref, p_ref, wp_ref, x_ref, y_ref, rs_ref, g_ref, *refs):
    cast_src, (o_ref,), cast_dst = _split_side_refs(refs, 1)
    _side_cast(cast_src, cast_dst)
    z = jnp.dot(xb_ref[...], wg_ref[...], preferred_element_type=F32)
    gate = 0.5 * jnp.tanh(0.5 * z) + 0.5
    upd = gate * jnp.dot(p_ref[...].astype(BF16), wp_ref[...], preferred_element_type=F32)
    rs = rs_ref[...]
    for c in range(PLE_TILE_N // LANES):
        sl = slice(c * LANES, (c + 1) * LANES)
        x1 = x_ref[:, sl] + y_ref[:, sl].astype(F32) * rs * g_ref[:, sl]
        o_ref[:, sl] = x1 + upd[:, sl]


def _ple(x, y, rs, g, x1b, p, wg, wp, side_ws=(), side_layer=0):
    t = x.shape[0]
    tm, tn = min(MAX_TILE_M, t), PLE_TILE_N
    grid = (t // tm, D_MODEL // tn)
    tile = pl.BlockSpec((tm, tn), lambda i, j: (i, j))
    side_in, side_out, side_shapes = _side_cast_specs(side_ws, side_layer, *grid)
    return pl.pallas_call(
        _ple_kernel,
        grid=grid,
        in_specs=[pl.BlockSpec((tm, D_MODEL), lambda i, j: (i, 0)),
                  pl.BlockSpec((D_MODEL, tn), lambda i, j: (0, j)),
                  pl.BlockSpec((tm, PLE_DIM), lambda i, j: (i, 0)),
                  pl.BlockSpec((PLE_DIM, tn), lambda i, j: (0, j)),
                  tile, tile,
                  pl.BlockSpec((tm, LANES), lambda i, j: (i, 0)),
                  pl.BlockSpec((1, tn), lambda i, j: (0, j))] + side_in,
        out_specs=[tile] + side_out,
        out_shape=[jax.ShapeDtypeStruct((t, D_MODEL), F32)] + side_shapes,
        compiler_params=_params(("arbitrary", "arbitrary")),
        name="ple",
    )(x1b, wg, p, wp, x, y, rs, g.reshape(1, D_MODEL), *side_ws)


def _rope_tables(seq):
    half = RET_QK_DIM // 2
    inv = ROPE_BASE ** (-jnp.arange(half, dtype=F32) / half)
    ang = jnp.arange(seq).astype(F32)[:, None] * inv[None, :]
    cos, sin = jnp.cos(ang), jnp.sin(ang)
    return jnp.concatenate([cos, cos], axis=-1), jnp.concatenate([-sin, sin], axis=-1)


SIDE_CAST_HOSTS = {"in_proj": ("w_in",), "out_proj": ("w_out", "w_proj_a", "w_proj_b"), "ple": ("w_ple_gate",)}


def _layer(x, p_i, batch, seq, wts, small, rope, stacked, next_layer):
    t = batch * seq
    grids = {"in_proj": (t // min(MAX_TILE_M, seq), IN_WIDTH // TILE_N),
             "out_proj": (t // min(MAX_TILE_M, t), D_MODEL // TILE_N),
             "ple": (t // min(MAX_TILE_M, t), D_MODEL // PLE_TILE_N)}
    hosted = {host: tuple(n for n in names if stacked is not None and _side_cast_fits(stacked[n], *grids[host]))
              for host, names in SIDE_CAST_HOSTS.items()}
    side = lambda host: dict(side_ws=tuple(stacked[n] for n in hosted[host]), side_layer=next_layer)
    casts = {}

    xn = _rmsnorm_pre(x, small["ln_pre"])
    proj, *cast = _in_proj(xn, wts["w_in"], seq, **side("in_proj"))
    casts.update(zip(hosted["in_proj"], cast))
    a = _na_attention(proj, small["na_bias"][seq // GRID_W], batch, seq)
    b = _retention(proj, small["ret_ld"], small["ret_gn"], *rope[seq], batch, seq)
    merged = _merge(a, b, wts["w_proj_a"], wts["w_proj_b"], proj)
    y, *cast = _out_proj(merged, wts["w_out"], **side("out_proj"))
    casts.update(zip(hosted["out_proj"], cast))
    x1b, rs = _post_norm(x, y, small["ln_post"])
    x, *cast = _ple(x, y, rs, small["ln_post"], x1b, p_i.reshape(t, PLE_DIM), wts["w_ple_gate"], wts["w_ple"],
                    **side("ple"))
    casts.update(zip(hosted["ple"], cast))
    return x, casts


def kernel(x_prompt, x_sample, p_prompt, p_sample, w_in, ln_pre, ln_post, na_rpb, ret_log_decay_fwd,
           ret_log_decay_bwd, ret_gn_gain, w_proj_a, w_proj_b, w_out, w_ple, w_ple_gate):
    depth = w_in.shape[0]
    stacked = {"w_in": w_in, "w_proj_a": w_proj_a, "w_proj_b": w_proj_b, "w_out": w_out,
               "w_ple": w_ple, "w_ple_gate": w_ple_gate}
    trunks = [(x_prompt, p_prompt), (x_sample, p_sample)]
    shapes = [x.shape[:2] for x, _ in trunks]
    xs = [x.reshape(-1, D_MODEL) for x, _ in trunks]
    rope = {seq: _rope_tables(seq) for _, seq in shapes}
    wts = {name: _cast_layer_bf16(w, 0) for name, w in stacked.items()}
    for i in range(depth):
        small = {"ln_pre": ln_pre[i].astype(F32),
                 "ln_post": ln_post[i].astype(F32),
                 "na_bias": {seq // GRID_W: _na_bias_table(na_rpb[i], seq // GRID_W) for _, seq in shapes},
                 "ret_ld": jnp.stack([ret_log_decay_fwd[i], ret_log_decay_bwd[i]]).astype(F32),
                 "ret_gn": ret_gn_gain[i]}
        next_wts = None
        for tr, ((batch, seq), (_, p)) in enumerate(zip(shapes, trunks)):
            host = i + 1 < depth and next_wts is None
            xs[tr], casts = _layer(xs[tr], p[i], batch, seq, wts, small, rope, stacked if host else None, i + 1)
            if host:
                next_wts = {name: casts[name] if name in casts else _cast_layer_bf16(w, i + 1)
                            for name, w in stacked.items()}
        wts = next_wts
    return tuple(x.reshape(batch, seq, D_MODEL) for x, (batch, seq) in zip(xs, shapes))
```

```python
import functools

import numpy as np
import jax
import jax.numpy as jnp
from jax import lax
from jax.experimental import pallas as pl
from jax.experimental.pallas import tpu as pltpu

D_MODEL = 4096
GRID_W = 64
PLE_DIM = 256
NA_HEADS = 16
NA_HEAD_DIM = 128
NA_WIDTH = NA_HEADS * NA_HEAD_DIM
NA_WIN_ROWS = 8
NA_WIN_COLS = 16
RET_HEADS = 8
RET_QK_DIM = 128
RET_V_DIM = 256
RET_QK_WIDTH = RET_HEADS * RET_QK_DIM
RET_V_WIDTH = RET_HEADS * RET_V_DIM
RET_CHUNK = 128
ROPE_BASE = 10000.0
EPS = 1e-6

OFF_NA_Q = 0
OFF_NA_K = OFF_NA_Q + NA_WIDTH
OFF_NA_V = OFF_NA_K + NA_WIDTH
OFF_NA_G = OFF_NA_V + NA_WIDTH
OFF_R_Q = OFF_NA_G + NA_WIDTH
OFF_R_K = OFF_R_Q + RET_QK_WIDTH
OFF_R_V = OFF_R_K + RET_QK_WIDTH
OFF_R_G = OFF_R_V + RET_V_WIDTH
OFF_GATE_A = OFF_R_G + RET_V_WIDTH
OFF_GATE_B = OFF_GATE_A + D_MODEL
IN_WIDTH = OFF_GATE_B + D_MODEL

TILE_N = 1024
MAX_TILE_M = 1024
ROW_TILE = 256
V7X_VMEM_LIMIT_BYTES = 56 * 1024 * 1024
LANES = 128

F32 = jnp.float32
BF16 = jnp.bfloat16


def _params(semantics, vmem=V7X_VMEM_LIMIT_BYTES):
    return pltpu.CompilerParams(dimension_semantics=semantics, vmem_limit_bytes=vmem)


CAST_BLOCK_BYTES = 8 * 1024 * 1024


def _cast_kernel(w_ref, o_ref):
    o_ref[...] = w_ref[...].astype(o_ref.dtype)


def _cast_layer_bf16(w, layer):
    _, k, n = w.shape
    rb = max(r for r in range(16, k + 1, 16) if k % r == 0 and (r * n * 4 <= CAST_BLOCK_BYTES or r == 16))
    return pl.pallas_call(
        _cast_kernel,
        grid=(k // rb,),
        in_specs=[pl.BlockSpec((None, rb, n), lambda i: (layer, i, 0))],
        out_specs=pl.BlockSpec((rb, n), lambda i: (i, 0)),
        out_shape=jax.ShapeDtypeStruct((k, n), BF16),
        compiler_params=_params(("arbitrary",)),
        name="cast_bf16",
    )(w)


SUBLANES_BF16 = 16


SIDE_CAST_MAX_BYTES = 1024 * 1024


def _side_cast_fits(w, ni, nj):
    _, k, n = w.shape
    return (k % ni == 0 and n % nj == 0 and (k // ni) % SUBLANES_BF16 == 0 and (n // nj) % LANES == 0
            and (k // ni) * (n // nj) * 4 <= SIDE_CAST_MAX_BYTES)


def _side_cast_specs(ws, layer, ni, nj):
    in_specs, out_specs, out_shapes = [], [], []
    for w in ws:
        _, k, n = w.shape
        blk = (k // ni, n // nj)
        in_specs.append(pl.BlockSpec((None,) + blk, lambda i, j: (layer, i, j)))
        out_specs.append(pl.BlockSpec(blk, lambda i, j: (i, j)))
        out_shapes.append(jax.ShapeDtypeStruct((k, n), BF16))
    return in_specs, out_specs, out_shapes


def _split_side_refs(refs, n_out):
    n = (len(refs) - n_out) // 2
    return refs[:n], refs[n:n + n_out], refs[n + n_out:]


def _side_cast(src_refs, dst_refs):
    for src, dst in zip(src_refs, dst_refs):
        dst[...] = src[...].astype(dst.dtype)


def _rmsnorm_kernel(x_ref, g_ref, o_ref):
    x = x_ref[...]
    ms = jnp.mean(x * x, axis=-1, keepdims=True)
    o_ref[...] = (x * lax.rsqrt(ms + EPS) * g_ref[...]).astype(o_ref.dtype)


def _rmsnorm_pre(x, g):
    t = x.shape[0]
    return pl.pallas_call(
        _rmsnorm_kernel,
        grid=(t // ROW_TILE,),
        in_specs=[pl.BlockSpec((ROW_TILE, D_MODEL), lambda i: (i, 0)),
                  pl.BlockSpec((1, D_MODEL), lambda i: (0, 0))],
        out_specs=pl.BlockSpec((ROW_TILE, D_MODEL), lambda i: (i, 0)),
        out_shape=jax.ShapeDtypeStruct((t, D_MODEL), BF16),
        compiler_params=_params(("arbitrary",)),
        name="rmsnorm_pre",
    )(x, g.reshape(1, D_MODEL))


_J_NA_K = OFF_NA_K // TILE_N
_J_NA_G = OFF_NA_G // TILE_N
_J_R_Q = OFF_R_Q // TILE_N
_J_R_K = OFF_R_K // TILE_N
_J_R_G = OFF_R_G // TILE_N
_J_GATE = OFF_GATE_A // TILE_N
assert NA_HEAD_DIM == RET_QK_DIM
QK_SCALE = NA_HEAD_DIM ** -0.5
LOG2E = float(np.log2(np.e))
NA_Q_SCALE = QK_SCALE * LOG2E


IN_TILE_M = MAX_TILE_M
IN_TILES_PER_STEP = 1


def _inproj_kernel(x_ref, w_ref, *refs):
    cast_src, (o_ref,), cast_dst = _split_side_refs(refs, 1)
    _side_cast(cast_src, cast_dst)
    for half in range(IN_TILES_PER_STEP):
        j = pl.program_id(1) * IN_TILES_PER_STEP + half
        cols = slice(half * TILE_N, (half + 1) * TILE_N)
        y = jnp.dot(x_ref[...], w_ref[:, cols], preferred_element_type=F32)
        is_gate = j >= _J_GATE
        is_silu = ((j >= _J_NA_G) & (j < _J_R_Q)) | ((j >= _J_R_G) & (j < _J_GATE))
        uses_sigmoid = is_gate | is_silu
        scale = jnp.where(j < _J_NA_K, NA_Q_SCALE, jnp.where(j == _J_R_K, QK_SCALE, 1.0))
        a = jnp.where(uses_sigmoid, 0.5, scale)
        b = jnp.where(uses_sigmoid, 0.5, 0.0)
        u = jnp.where(is_gate, 1.0, y)
        o_ref[:, cols] = (u * (a + b * jnp.tanh(0.5 * y))).astype(o_ref.dtype)


def _in_proj_grid(t, seq):
    tm = min(IN_TILE_M, seq)
    return tm, (t // tm, IN_WIDTH // (TILE_N * IN_TILES_PER_STEP))


def _in_proj(xn, w, seq, side_ws=(), side_layer=0):
    t = xn.shape[0]
    tm, grid = _in_proj_grid(t, seq)
    tn = TILE_N * IN_TILES_PER_STEP
    side_in, side_out, side_shapes = _side_cast_specs(side_ws, side_layer, *grid)
    return pl.pallas_call(
        _inproj_kernel,
        grid=grid,
        in_specs=[pl.BlockSpec((tm, D_MODEL), lambda i, j: (i, 0)),
                  pl.BlockSpec((D_MODEL, tn), lambda i, j: (0, j))] + side_in,
        out_specs=[pl.BlockSpec((tm, tn), lambda i, j: (i, j))] + side_out,
        out_shape=[jax.ShapeDtypeStruct((t, IN_WIDTH), BF16)] + side_shapes,
        compiler_params=_params(("arbitrary", "arbitrary")),
        name="in_proj",
    )(xn, w, *side_ws)


NA_PAIR_ROWS = 2
NA_PAIR_TOKENS = NA_PAIR_ROWS * GRID_W
NA_KEY_ROWS = NA_WIN_ROWS + NA_PAIR_ROWS
NA_KEY_CHUNKS = NA_KEY_ROWS // NA_PAIR_ROWS
NA_KEYS = NA_KEY_ROWS * GRID_W
NA_VARIANTS = 5
NA_PAIR_BLOCK = 32
NA_SUM_ROWS = 16


def _na_kernel(q_ref, k_ref, v_ref, g_ref, bias_ref, o_ref, vt_ref, s_ref, p_ref, *, rows):
    n_pairs = rows // NA_PAIR_ROWS
    n_chunks = n_pairs
    ct = NA_PAIR_TOKENS
    nt = (((1,), (1,)), ((), ()))
    pair_block = s_ref.shape[0]

    def transpose_chunk(c, carry):
        r0 = pl.multiple_of(c * ct, ct)
        vt_ref[c] = v_ref[pl.ds(r0, ct), :].astype(F32).T.astype(BF16)
        return carry

    lax.fori_loop(0, n_chunks, transpose_chunk, 0, unroll=pair_block)
    ones_rows = jnp.ones((NA_SUM_ROWS, NA_KEYS), BF16)

    def block(blk, carry):
        pairs = []
        for u in range(pair_block):
            pr = blk * pair_block + u
            c0 = jnp.clip(pr - NA_WIN_ROWS // 4, 0, n_chunks - NA_KEY_CHUNKS)
            var = jnp.where(pr < 2, pr, jnp.where(pr >= n_pairs - 2, pr - (n_pairs - 2) + 3, 2))
            q0 = pl.multiple_of(pr * ct, ct)
            k0 = pl.multiple_of(c0 * ct, ct)
            pairs.append((c0, q0))
            s = lax.dot_general(k_ref[pl.ds(k0, NA_KEYS), :], q_ref[pl.ds(q0, ct), :], nt,
                                preferred_element_type=F32)
            s_ref[u] = s + bias_ref[var]
        for u in range(pair_block):
            s = s_ref[u]
            m = jnp.max(s, axis=0, keepdims=True)
            p_ref[u] = jnp.exp2(s - m).astype(BF16)
        for u in range(pair_block):
            c0, q0 = pairs[u]
            vt = jnp.concatenate([vt_ref[c0 + c] for c in range(NA_KEY_CHUNKS)], axis=1)
            ot = jnp.dot(jnp.concatenate([vt, ones_rows], axis=0), p_ref[u], preferred_element_type=F32)
            ot = ot[:NA_HEAD_DIM] * (1.0 / ot[NA_HEAD_DIM:NA_HEAD_DIM + 1])
            o = ot.T * g_ref[pl.ds(q0, ct), :].astype(F32)
            o_ref[pl.ds(q0, ct), :] = o.astype(o_ref.dtype)
        return carry

    lax.fori_loop(0, n_pairs // pair_block, block, 0)


def _na_bias_table(rpb, rows):
    n_pairs = rows // NA_PAIR_ROWS
    cols = np.arange(GRID_W)
    col_start = np.clip(cols - NA_WIN_COLS // 2, 0, GRID_W - NA_WIN_COLS)
    col_valid = (cols[None, :] >= col_start[:, None]) & (cols[None, :] < col_start[:, None] + NA_WIN_COLS)
    pad = GRID_W - NA_WIN_COLS
    rp = jnp.pad(rpb.astype(F32), ((0, 0), (0, 0), (pad, pad)))
    toep = jnp.stack([rp[:, :, GRID_W - 1 - qc: 2 * GRID_W - 1 - qc] for qc in range(GRID_W)], axis=2)
    toep = jnp.where(col_valid[None, None], toep, -jnp.inf)
    toep_t = toep.transpose(0, 1, 3, 2)
    masked = jnp.full((NA_HEADS, GRID_W, GRID_W), -jnp.inf, F32)
    tables = []
    for pr in (0, 1, 2, n_pairs - 2, n_pairs - 1):
        base = int(np.clip(NA_PAIR_ROWS * pr - NA_WIN_ROWS // 2, 0, rows - NA_KEY_ROWS))
        slabs = []
        for w in range(NA_KEY_ROWS):
            per_row = []
            for i in range(NA_PAIR_ROWS):
                r_q = NA_PAIR_ROWS * pr + i
                rs = int(np.clip(r_q - NA_WIN_ROWS // 2, 0, rows - NA_WIN_ROWS))
                kr = base + w
                if rs <= kr < rs + NA_WIN_ROWS:
                    per_row.append(toep_t[:, kr - r_q + NA_WIN_ROWS - 1])
                else:
                    per_row.append(masked)
            slabs.append(jnp.concatenate(per_row, axis=-1))
        tables.append(jnp.concatenate(slabs, axis=1))
    return jnp.stack(tables, axis=1) * LOG2E


def _na_attention(proj, bias, batch, seq):
    rows = seq // GRID_W
    n_pairs = rows // NA_PAIR_ROWS
    pair_block = min(NA_PAIR_BLOCK, n_pairs)
    assert rows % NA_PAIR_ROWS == 0 and rows >= NA_KEY_ROWS + 2 and n_pairs % pair_block == 0
    hd = NA_HEAD_DIM
    blk = lambda off: pl.BlockSpec((seq, hd), lambda b, h, off=off: (b, off // hd + h))
    return pl.pallas_call(
        functools.partial(_na_kernel, rows=rows),
        grid=(batch, NA_HEADS),
        in_specs=[blk(OFF_NA_Q), blk(OFF_NA_K), blk(OFF_NA_V), blk(OFF_NA_G),
                  pl.BlockSpec((None, NA_VARIANTS, NA_KEYS, NA_PAIR_TOKENS), lambda b, h: (h, 0, 0, 0))],
        out_specs=pl.BlockSpec((seq, hd), lambda b, h: (b, h)),
        out_shape=jax.ShapeDtypeStruct((batch * seq, NA_WIDTH), BF16),
        scratch_shapes=[pltpu.VMEM((n_pairs, hd, NA_PAIR_TOKENS), BF16),
                        pltpu.VMEM((pair_block, NA_KEYS, NA_PAIR_TOKENS), F32),
                        pltpu.VMEM((pair_block, NA_KEYS, NA_PAIR_TOKENS), BF16)],
        compiler_params=_params(("arbitrary", "arbitrary")),
        name="na_attention",
    )(proj, proj, proj, proj, bias)


RET_UNROLL = 32


def _ret_kernel(ld_ref, q_ref, k_ref, v_ref, g_ref, gain_ref, cos_ref, sin_ref, o_ref,
                qr_ref, kr_ref, kv_ref, st_ref, dsum_ref, qd_ref, kd_ref, *, n_chunks):
    c_len, dk = RET_CHUNK, RET_QK_DIM
    unroll = min(RET_UNROLL, n_chunks)
    h = pl.program_id(1)
    ldf = -jnp.abs(ld_ref[0, h])
    ldb = -jnp.abs(ld_ref[1, h])
    row = lax.broadcasted_iota(jnp.int32, (c_len, c_len), 0).astype(F32)
    col = lax.broadcasted_iota(jnp.int32, (c_len, c_len), 1).astype(F32)
    diff = row - col
    dsum_ref[...] = (jnp.where(diff >= 0, jnp.exp(ldf * jnp.maximum(diff, 0.0)), 0.0)
                     + jnp.where(diff <= 0, jnp.exp(ldb * jnp.maximum(-diff, 0.0)), 0.0))
    qd_ref[:, :dk] = jnp.exp(ldf * (row + 1.0))
    qd_ref[:, dk:] = jnp.exp(ldb * (c_len - row))
    kd_ref[:, :dk] = jnp.exp(ldf * (c_len - 1.0 - row))
    kd_ref[:, dk:] = jnp.exp(ldb * row)
    zero_row = jnp.zeros((1, RET_V_DIM), F32)
    cdf = jnp.exp(zero_row + ldf * c_len)
    cdb = jnp.exp(zero_row + ldb * c_len)
    chunk = lambda c: pl.ds(pl.multiple_of(c * c_len, c_len), c_len)

    def rotate(c, carry):
        rows = chunk(c)
        cos, sin = cos_ref[rows, :], sin_ref[rows, :]
        for src, dst in ((q_ref, qr_ref), (k_ref, kr_ref)):
            t = src[rows, :].astype(F32)
            dst[rows, :] = t * cos + pltpu.roll(t, dk // 2, axis=1) * sin
        return carry

    lax.fori_loop(0, n_chunks, rotate, 0, unroll=unroll)

    def chunk_kv(c, carry):
        rows = chunk(c)
        k = kr_ref[rows, :]
        k2 = (jnp.concatenate([k, k], axis=1) * kd_ref[...]).astype(BF16)
        kv_ref[c] = lax.dot_general(k2, v_ref[rows, :], (((0,), (0,)), ((), ())),
                                    preferred_element_type=F32)
        return carry

    lax.fori_loop(0, n_chunks, chunk_kv, 0, unroll=unroll)

    def scan_fwd(c, s):
        st_ref[c, :dk, :] = s.astype(BF16)
        return s * cdf + kv_ref[c, :dk, :]

    def scan_bwd(t, s):
        c = n_chunks - 1 - t
        st_ref[c, dk:, :] = s.astype(BF16)
        return s * cdb + kv_ref[c, dk:, :]

    zero_state = jnp.zeros((dk, RET_V_DIM), F32)
    lax.fori_loop(0, n_chunks, scan_fwd, zero_state, unroll=unroll)
    lax.fori_loop(0, n_chunks, scan_bwd, zero_state, unroll=unroll)

    def chunk_out(c, carry):
        rows = chunk(c)
        q = qr_ref[rows, :]
        v = v_ref[rows, :]
        sc = lax.dot_general(q.astype(BF16), kr_ref[rows, :].astype(BF16), (((1,), (1,)), ((), ())),
                             preferred_element_type=F32)
        o = jnp.dot((sc * dsum_ref[...]).astype(BF16), v, preferred_element_type=F32)
        q2 = (jnp.concatenate([q, q], axis=1) * qd_ref[...]).astype(BF16)
        o = o + jnp.dot(q2, st_ref[c], preferred_element_type=F32)
        o = o * lax.rsqrt(jnp.mean(o * o, axis=-1, keepdims=True) + EPS)
        o = o * gain_ref[...] * g_ref[rows, :].astype(F32)
        o_ref[rows, :] = o.astype(o_ref.dtype)
        return carry

    lax.fori_loop(0, n_chunks, chunk_out, 0, unroll=unroll)


def _retention(proj, ld, gain, cos_t, sin_t, batch, seq):
    n_chunks = seq // RET_CHUNK
    dk, dv, c_len = RET_QK_DIM, RET_V_DIM, RET_CHUNK
    assert n_chunks % min(RET_UNROLL, n_chunks) == 0
    qk_blk = lambda off: pl.BlockSpec((seq, dk), lambda b, h, off=off: (b, off // dk + h))
    v_blk = lambda off: pl.BlockSpec((seq, dv), lambda b, h, off=off: (b, off // dv + h))
    rope_blk = pl.BlockSpec((seq, dk), lambda b, h: (0, 0))
    return pl.pallas_call(
        functools.partial(_ret_kernel, n_chunks=n_chunks),
        grid=(batch, RET_HEADS),
        in_specs=[pl.BlockSpec(memory_space=pltpu.SMEM),
                  qk_blk(OFF_R_Q), qk_blk(OFF_R_K), v_blk(OFF_R_V), v_blk(OFF_R_G),
                  pl.BlockSpec((None, 1, dv), lambda b, h: (h, 0, 0)), rope_blk, rope_blk],
        out_specs=pl.BlockSpec((seq, dv), lambda b, h: (b, h)),
        out_shape=jax.ShapeDtypeStruct((batch * seq, RET_V_WIDTH), BF16),
        scratch_shapes=[pltpu.VMEM((seq, dk), F32),
                        pltpu.VMEM((seq, dk), F32),
                        pltpu.VMEM((n_chunks, 2 * dk, dv), F32),
                        pltpu.VMEM((n_chunks, 2 * dk, dv), BF16),
                        pltpu.VMEM((c_len, c_len), F32),
                        pltpu.VMEM((c_len, 2 * dk), F32),
                        pltpu.VMEM((c_len, 2 * dk), F32)],
        compiler_params=_params(("arbitrary", "arbitrary")),
        name="retention",
    )(ld, proj, proj, proj, proj, gain.astype(F32).reshape(RET_HEADS, 1, dv), cos_t, sin_t)


def _merge_kernel(a_ref, b_ref, wa_ref, wb_ref, ga_ref, gb_ref, o_ref):
    ya = jnp.dot(a_ref[...], wa_ref[...], preferred_element_type=F32)
    yb = jnp.dot(b_ref[...], wb_ref[...], preferred_element_type=F32)
    o_ref[...] = (ga_ref[...].astype(F32) * ya + gb_ref[...].astype(F32) * yb).astype(o_ref.dtype)


def _merge(a, b, wa, wb, proj):
    t = a.shape[0]
    tm = min(MAX_TILE_M, t)
    ja, jb = OFF_GATE_A // TILE_N, OFF_GATE_B // TILE_N
    return pl.pallas_call(
        _merge_kernel,
        grid=(t // tm, D_MODEL // TILE_N),
        in_specs=[pl.BlockSpec((tm, NA_WIDTH), lambda i, j: (i, 0)),
                  pl.BlockSpec((tm, RET_V_WIDTH), lambda i, j: (i, 0)),
                  pl.BlockSpec((NA_WIDTH, TILE_N), lambda i, j: (0, j)),
                  pl.BlockSpec((RET_V_WIDTH, TILE_N), lambda i, j: (0, j)),
                  pl.BlockSpec((tm, TILE_N), lambda i, j: (i, ja + j)),
                  pl.BlockSpec((tm, TILE_N), lambda i, j: (i, jb + j))],
        out_specs=pl.BlockSpec((tm, TILE_N), lambda i, j: (i, j)),
        out_shape=jax.ShapeDtypeStruct((t, D_MODEL), BF16),
        compiler_params=_params(("arbitrary", "arbitrary")),
        name="merge",
    )(a, b, wa, wb, proj, proj)


def _matmul_kernel(x_ref, w_ref, *refs):
    cast_src, (o_ref,), cast_dst = _split_side_refs(refs, 1)
    _side_cast(cast_src, cast_dst)
    o_ref[...] = jnp.dot(x_ref[...], w_ref[...], preferred_element_type=F32).astype(o_ref.dtype)


def _out_proj(merged, w, side_ws=(), side_layer=0):
    t = merged.shape[0]
    tm = min(MAX_TILE_M, t)
    grid = (t // tm, D_MODEL // TILE_N)
    side_in, side_out, side_shapes = _side_cast_specs(side_ws, side_layer, *grid)
    return pl.pallas_call(
        _matmul_kernel,
        grid=grid,
        in_specs=[pl.BlockSpec((tm, D_MODEL), lambda i, j: (i, 0)),
                  pl.BlockSpec((D_MODEL, TILE_N), lambda i, j: (0, j))] + side_in,
        out_specs=[pl.BlockSpec((tm, TILE_N), lambda i, j: (i, j))] + side_out,
        out_shape=[jax.ShapeDtypeStruct((t, D_MODEL), BF16)] + side_shapes,
        compiler_params=_params(("arbitrary", "arbitrary")),
        name="out_proj",
    )(merged, w, *side_ws)


def _post_kernel(x_ref, y_ref, g_ref, ob_ref, rs_ref):
    y = y_ref[...].astype(F32)
    rs = lax.rsqrt(jnp.mean(y * y, axis=-1, keepdims=True) + EPS)
    ob_ref[...] = (x_ref[...] + y * rs * g_ref[...]).astype(ob_ref.dtype)
    rs_ref[...] = jnp.broadcast_to(rs, rs_ref.shape)


def _post_norm(x, y, g):
    t = x.shape[0]
    row = pl.BlockSpec((ROW_TILE, D_MODEL), lambda i: (i, 0))
    return pl.pallas_call(
        _post_kernel,
        grid=(t // ROW_TILE,),
        in_specs=[row, row, pl.BlockSpec((1, D_MODEL), lambda i: (0, 0))],
        out_specs=[row, pl.BlockSpec((ROW_TILE, LANES), lambda i: (i, 0))],
        out_shape=[jax.ShapeDtypeStruct((t, D_MODEL), BF16), jax.ShapeDtypeStruct((t, LANES), F32)],
        compiler_params=_params(("arbitrary",)),
        name="post_norm",
    )(x, y, g.reshape(1, D_MODEL))


PLE_TILE_N = 512


def _ple_kernel(xb_ref, wg_ref, p_ref, wp_ref, x_ref, y_ref, rs_ref, g_ref, *refs):
    cast_src, (o_ref,), cast_dst = _split_side_refs(refs, 1)
    _side_cast(cast_src, cast_dst)
    z = jnp.dot(xb_ref[...], wg_ref[...], preferred_element_type=F32)
    gate = 0.5 * jnp.tanh(0.5 * z) + 0.5
    upd = gate * jnp.dot(p_ref[...].astype(BF16), wp_ref[...], preferred_element_type=F32)
    rs = rs_ref[...]
    for c in range(PLE_TILE_N // LANES):
        sl = slice(c * LANES, (c + 1) * LANES)
        x1 = x_ref[:, sl] + y_ref[:, sl].astype(F32) * rs * g_ref[:, sl]
        o_ref[:, sl] = x1 + upd[:, sl]


def _ple(x, y, rs, g, x1b, p, wg, wp, side_ws=(), side_layer=0):
    t = x.shape[0]
    tm, tn = min(MAX_TILE_M, t), PLE_TILE_N
    grid = (t // tm, D_MODEL // tn)
    tile = pl.BlockSpec((tm, tn), lambda i, j: (i, j))
    side_in, side_out, side_shapes = _side_cast_specs(side_ws, side_layer, *grid)
    return pl.pallas_call(
        _ple_kernel,
        grid=grid,
        in_specs=[pl.BlockSpec((tm, D_MODEL), lambda i, j: (i, 0)),
                  pl.BlockSpec((D_MODEL, tn), lambda i, j: (0, j)),
                  pl.BlockSpec((tm, PLE_DIM), lambda i, j: (i, 0)),
                  pl.BlockSpec((PLE_DIM, tn), lambda i, j: (0, j)),
                  tile, tile,
                  pl.BlockSpec((tm, LANES), lambda i, j: (i, 0)),
                  pl.BlockSpec((1, tn), lambda i, j: (0, j))] + side_in,
        out_specs=[tile] + side_out,
        out_shape=[jax.ShapeDtypeStruct((t, D_MODEL), F32)] + side_shapes,
        compiler_params=_params(("arbitrary", "arbitrary")),
        name="ple",
    )(x1b, wg, p, wp, x, y, rs, g.reshape(1, D_MODEL), *side_ws)


def _rope_tables(seq):
    half = RET_QK_DIM // 2
    inv = ROPE_BASE ** (-jnp.arange(half, dtype=F32) / half)
    ang = jnp.arange(seq).astype(F32)[:, None] * inv[None, :]
    cos, sin = jnp.cos(ang), jnp.sin(ang)
    return jnp.concatenate([cos, cos], axis=-1), jnp.concatenate([-sin, sin], axis=-1)


SIDE_CAST_HOSTS = {"in_proj": ("w_in",), "out_proj": ("w_out", "w_proj_a", "w_proj_b"), "ple": ("w_ple_gate",)}


def _layer(x, p_i, batch, seq, wts, small, rope, stacked, next_layer):
    t = batch * seq
    grids = {"in_proj": _in_proj_grid(t, seq)[1],
             "out_proj": (t // min(MAX_TILE_M, t), D_MODEL // TILE_N),
             "ple": (t // min(MAX_TILE_M, t), D_MODEL // PLE_TILE_N)}
    hosted = {host: tuple(n for n in names if stacked is not None and _side_cast_fits(stacked[n], *grids[host]))
              for host, names in SIDE_CAST_HOSTS.items()}
    side = lambda host: dict(side_ws=tuple(stacked[n] for n in hosted[host]), side_layer=next_layer)
    casts = {}

    xn = _rmsnorm_pre(x, small["ln_pre"])
    proj, *cast = _in_proj(xn, wts["w_in"], seq, **side("in_proj"))
    casts.update(zip(hosted["in_proj"], cast))
    a = _na_attention(proj, small["na_bias"][seq // GRID_W], batch, seq)
    b = _retention(proj, small["ret_ld"], small["ret_gn"], *rope[seq], batch, seq)
    merged = _merge(a, b, wts["w_proj_a"], wts["w_proj_b"], proj)
    y, *cast = _out_proj(merged, wts["w_out"], **side("out_proj"))
    casts.update(zip(hosted["out_proj"], cast))
    x1b, rs = _post_norm(x, y, small["ln_post"])
    x, *cast = _ple(x, y, rs, small["ln_post"], x1b, p_i.reshape(t, PLE_DIM), wts["w_ple_gate"], wts["w_ple"],
                    **side("ple"))
    casts.update(zip(hosted["ple"], cast))
    return x, casts


def kernel(x_prompt, x_sample, p_prompt, p_sample, w_in, ln_pre, ln_post, na_rpb, ret_log_decay_fwd,
           ret_log_decay_bwd, ret_gn_gain, w_proj_a, w_proj_b, w_out, w_ple, w_ple_gate):
    depth = w_in.shape[0]
    stacked = {"w_in": w_in, "w_proj_a": w_proj_a, "w_proj_b": w_proj_b, "w_out": w_out,
               "w_ple": w_ple, "w_ple_gate": w_ple_gate}
    trunks = [(x_prompt, p_prompt), (x_sample, p_sample)]
    shapes = [x.shape[:2] for x, _ in trunks]
    xs = [x.reshape(-1, D_MODEL) for x, _ in trunks]
    rope = {seq: _rope_tables(seq) for _, seq in shapes}
    wts = {name: _cast_layer_bf16(w, 0) for name, w in stacked.items()}
    for i in range(depth):
        small = {"ln_pre": ln_pre[i].astype(F32),
                 "ln_post": ln_post[i].astype(F32),
                 "na_bias": {seq // GRID_W: _na_bias_table(na_rpb[i], seq // GRID_W) for _, seq in shapes},
                 "ret_ld": jnp.stack([ret_log_decay_fwd[i], ret_log_decay_bwd[i]]).astype(F32),
                 "ret_gn": ret_gn_gain[i]}
        next_wts = None
        for tr, ((batch, seq), (_, p)) in enumerate(zip(shapes, trunks)):
            host = i + 1 < depth and next_wts is None
            xs[tr], casts = _layer(xs[tr], p[i], batch, seq, wts, small, rope, stacked if host else None, i + 1)
            if host:
                next_wts = {name: casts[name] if name in casts else _cast_layer_bf16(w, i + 1)
                            for name, w in stacked.items()}
        wts = next_wts
    return tuple(x.reshape(batch, seq, D_MODEL) for x, (batch, seq) in zip(xs, shapes))
```

```python
import functools

import numpy as np
import jax
import jax.numpy as jnp
from jax import lax
from jax.experimental import pallas as pl
from jax.experimental.pallas import tpu as pltpu

D_MODEL = 4096
GRID_W = 64
PLE_DIM = 256
NA_HEADS = 16
NA_HEAD_DIM = 128
NA_WIDTH = NA_HEADS * NA_HEAD_DIM
NA_WIN_ROWS = 8
NA_WIN_COLS = 16
RET_HEADS = 8
RET_QK_DIM = 128
RET_V_DIM = 256
RET_QK_WIDTH = RET_HEADS * RET_QK_DIM
RET_V_WIDTH = RET_HEADS * RET_V_DIM
RET_CHUNK = 128
ROPE_BASE = 10000.0
EPS = 1e-6

OFF_NA_Q = 0
OFF_NA_K = OFF_NA_Q + NA_WIDTH
OFF_NA_V = OFF_NA_K + NA_WIDTH
OFF_NA_G = OFF_NA_V + NA_WIDTH
OFF_R_Q = OFF_NA_G + NA_WIDTH
OFF_R_K = OFF_R_Q + RET_QK_WIDTH
OFF_R_V = OFF_R_K + RET_QK_WIDTH
OFF_R_G = OFF_R_V + RET_V_WIDTH
OFF_GATE_A = OFF_R_G + RET_V_WIDTH
OFF_GATE_B = OFF_GATE_A + D_MODEL
IN_WIDTH = OFF_GATE_B + D_MODEL

TILE_N = 1024
MAX_TILE_M = 1024
ROW_TILE = 256
V7X_VMEM_LIMIT_BYTES = 56 * 1024 * 1024
LANES = 128

F32 = jnp.float32
BF16 = jnp.bfloat16


def _params(semantics, vmem=V7X_VMEM_LIMIT_BYTES):
    return pltpu.CompilerParams(dimension_semantics=semantics, vmem_limit_bytes=vmem)


CAST_BLOCK_BYTES = 8 * 1024 * 1024


def _cast_kernel(w_ref, o_ref):
    o_ref[...] = w_ref[...].astype(o_ref.dtype)


def _cast_layer_bf16(w, layer):
    _, k, n = w.shape
    rb = max(r for r in range(16, k + 1, 16) if k % r == 0 and (r * n * 4 <= CAST_BLOCK_BYTES or r == 16))
    return pl.pallas_call(
        _cast_kernel,
        grid=(k // rb,),
        in_specs=[pl.BlockSpec((None, rb, n), lambda i: (layer, i, 0))],
        out_specs=pl.BlockSpec((rb, n), lambda i: (i, 0)),
        out_shape=jax.ShapeDtypeStruct((k, n), BF16),
        compiler_params=_params(("arbitrary",)),
        name="cast_bf16",
    )(w)


SUBLANES_BF16 = 16


SIDE_CAST_MAX_BYTES = 1024 * 1024


def _side_cast_fits(w, ni, nj):
    _, k, n = w.shape
    return (k % ni == 0 and n % nj == 0 and (k // ni) % SUBLANES_BF16 == 0 and (n // nj) % LANES == 0
            and (k // ni) * (n // nj) * 4 <= SIDE_CAST_MAX_BYTES)


def _side_cast_specs(ws, layer, ni, nj):
    in_specs, out_specs, out_shapes = [], [], []
    for w in ws:
        _, k, n = w.shape
        blk = (k // ni, n // nj)
        in_specs.append(pl.BlockSpec((None,) + blk, lambda i, j: (layer, i, j)))
        out_specs.append(pl.BlockSpec(blk, lambda i, j: (i, j)))
        out_shapes.append(jax.ShapeDtypeStruct((k, n), BF16))
    return in_specs, out_specs, out_shapes


def _split_side_refs(refs, n_out):
    n = (len(refs) - n_out) // 2
    return refs[:n], refs[n:n + n_out], refs[n + n_out:]


def _side_cast(src_refs, dst_refs):
    for src, dst in zip(src_refs, dst_refs):
        dst[...] = src[...].astype(dst.dtype)


def _rmsnorm_kernel(x_ref, g_ref, o_ref):
    x = x_ref[...]
    ms = jnp.mean(x * x, axis=-1, keepdims=True)
    o_ref[...] = (x * lax.rsqrt(ms + EPS) * g_ref[...]).astype(o_ref.dtype)


def _rmsnorm_pre(x, g):
    t = x.shape[0]
    return pl.pallas_call(
        _rmsnorm_kernel,
        grid=(t // ROW_TILE,),
        in_specs=[pl.BlockSpec((ROW_TILE, D_MODEL), lambda i: (i, 0)),
                  pl.BlockSpec((1, D_MODEL), lambda i: (0, 0))],
        out_specs=pl.BlockSpec((ROW_TILE, D_MODEL), lambda i: (i, 0)),
        out_shape=jax.ShapeDtypeStruct((t, D_MODEL), BF16),
        compiler_params=_params(("arbitrary",)),
        name="rmsnorm_pre",
    )(x, g.reshape(1, D_MODEL))


_J_NA_K = OFF_NA_K // TILE_N
_J_NA_G = OFF_NA_G // TILE_N
_J_R_Q = OFF_R_Q // TILE_N
_J_R_K = OFF_R_K // TILE_N
_J_R_G = OFF_R_G // TILE_N
_J_GATE = OFF_GATE_A // TILE_N
assert NA_HEAD_DIM == RET_QK_DIM
QK_SCALE = NA_HEAD_DIM ** -0.5
LOG2E = float(np.log2(np.e))
NA_Q_SCALE = QK_SCALE * LOG2E


def _post_norm_rows(x_ref, y_ref, g_ref, ob_ref, rs_ref):
    y = y_ref[...].astype(F32)
    rs = lax.rsqrt(jnp.mean(y * y, axis=-1, keepdims=True) + EPS)
    x1b = (x_ref[...] + y * rs * g_ref[...]).astype(ob_ref.dtype)
    ob_ref[...] = x1b
    rs_ref[...] = jnp.broadcast_to(rs, rs_ref.shape)
    return x1b


def _post_kernel(x_ref, y_ref, g_ref, ob_ref, rs_ref):
    _post_norm_rows(x_ref, y_ref, g_ref, ob_ref, rs_ref)


def _post_slab_rows(t_post, n_steps):
    return min(r for r in range(SUBLANES_BF16, t_post + 1, SUBLANES_BF16)
               if t_post % r == 0 and t_post // r <= n_steps)


def _inproj_kernel(x_ref, w_ref, *refs, n_cast, has_post):
    n_in = n_cast + (3 if has_post else 0)
    cast_src, post_in = refs[:n_cast], refs[n_cast:n_in]
    o_ref, cast_dst, post_out = refs[n_in], refs[n_in + 1:n_in + 1 + n_cast], refs[n_in + 1 + n_cast:]
    _side_cast(cast_src, cast_dst)
    j = pl.program_id(1)
    is_gate = j >= _J_GATE
    is_silu = ((j >= _J_NA_G) & (j < _J_R_Q)) | ((j >= _J_R_G) & (j < _J_GATE))
    uses_sigmoid = is_gate | is_silu
    scale = jnp.where(j < _J_NA_K, NA_Q_SCALE, jnp.where(j == _J_R_K, QK_SCALE, 1.0))
    a = jnp.where(uses_sigmoid, 0.5, scale)
    b = jnp.where(uses_sigmoid, 0.5, 0.0)
    x = x_ref[...]
    for half in range(2):
        cols = slice(half * (TILE_N // 2), (half + 1) * (TILE_N // 2))
        if half == 1 and has_post:
            x1b = _post_norm_rows(*post_in, *post_out)
            never = pl.program_id(0) < 0
            x = jnp.concatenate([jnp.where(never, x1b, x[:x1b.shape[0]]), x[x1b.shape[0]:]], axis=0)
        y = jnp.dot(x, w_ref[:, cols], preferred_element_type=F32)
        u = jnp.where(is_gate, 1.0, y)
        o_ref[:, cols] = (u * (a + b * jnp.tanh(0.5 * y))).astype(o_ref.dtype)


def _in_proj_grid(t, seq):
    tm = min(MAX_TILE_M, seq)
    return tm, (t // tm, IN_WIDTH // TILE_N)


def _in_proj(xn, w, seq, side_ws=(), side_layer=0, post=None):
    t = xn.shape[0]
    tm, grid = _in_proj_grid(t, seq)
    tn = TILE_N
    side_in, side_out, side_shapes = _side_cast_specs(side_ws, side_layer, *grid)
    post_args, post_in, post_out, post_shapes = (), [], [], []
    if post is not None:
        px, py, pg = post
        t_post = px.shape[0]
        rows = _post_slab_rows(t_post, grid[0] * grid[1])
        slab = lambda i, j: (jnp.minimum(i * grid[1] + j, t_post // rows - 1), 0)
        post_args = (px, py, pg.reshape(1, D_MODEL))
        post_in = [pl.BlockSpec((rows, D_MODEL), slab), pl.BlockSpec((rows, D_MODEL), slab),
                   pl.BlockSpec((1, D_MODEL), lambda i, j: (0, 0))]
        post_out = [pl.BlockSpec((rows, D_MODEL), slab), pl.BlockSpec((rows, LANES), slab)]
        post_shapes = [jax.ShapeDtypeStruct((t_post, D_MODEL), BF16), jax.ShapeDtypeStruct((t_post, LANES), F32)]
    outs = pl.pallas_call(
        functools.partial(_inproj_kernel, n_cast=len(side_ws), has_post=post is not None),
        grid=grid,
        in_specs=[pl.BlockSpec((tm, D_MODEL), lambda i, j: (i, 0)),
                  pl.BlockSpec((D_MODEL, tn), lambda i, j: (0, j))] + side_in + post_in,
        out_specs=[pl.BlockSpec((tm, tn), lambda i, j: (i, j))] + side_out + post_out,
        out_shape=[jax.ShapeDtypeStruct((t, IN_WIDTH), BF16)] + side_shapes + post_shapes,
        compiler_params=_params(("arbitrary", "arbitrary")),
        name="in_proj",
    )(xn, w, *side_ws, *post_args)
    n_cast = len(side_ws)
    return outs[0], list(outs[1:1 + n_cast]), (tuple(outs[1 + n_cast:]) if post is not None else None)


NA_PAIR_ROWS = 2
NA_PAIR_TOKENS = NA_PAIR_ROWS * GRID_W
NA_KEY_ROWS = NA_WIN_ROWS + NA_PAIR_ROWS
NA_KEY_CHUNKS = NA_KEY_ROWS // NA_PAIR_ROWS
NA_KEYS = NA_KEY_ROWS * GRID_W
NA_VARIANTS = 5
NA_PAIR_BLOCK = 32
NA_SUM_ROWS = 16


def _na_kernel(q_ref, k_ref, v_ref, g_ref, bias_ref, o_ref, vt_ref, s_ref, p_ref, *, rows):
    n_pairs = rows // NA_PAIR_ROWS
    n_chunks = n_pairs
    ct = NA_PAIR_TOKENS
    nt = (((1,), (1,)), ((), ()))
    pair_block = s_ref.shape[0]

    def transpose_chunk(c, carry):
        r0 = pl.multiple_of(c * ct, ct)
        vt_ref[c] = v_ref[pl.ds(r0, ct), :].astype(F32).T.astype(BF16)
        return carry

    lax.fori_loop(0, n_chunks, transpose_chunk, 0, unroll=pair_block)
    ones_rows = jnp.ones((NA_SUM_ROWS, NA_KEYS), BF16)

    def block(blk, carry):
        pairs = []
        for u in range(pair_block):
            pr = blk * pair_block + u
            c0 = jnp.clip(pr - NA_WIN_ROWS // 4, 0, n_chunks - NA_KEY_CHUNKS)
            var = jnp.where(pr < 2, pr, jnp.where(pr >= n_pairs - 2, pr - (n_pairs - 2) + 3, 2))
            q0 = pl.multiple_of(pr * ct, ct)
            k0 = pl.multiple_of(c0 * ct, ct)
            pairs.append((c0, q0))
            s = lax.dot_general(k_ref[pl.ds(k0, NA_KEYS), :], q_ref[pl.ds(q0, ct), :], nt,
                                preferred_element_type=F32)
            s_ref[u] = s + bias_ref[var]
        for u in range(pair_block):
            s = s_ref[u]
            m = jnp.max(s, axis=0, keepdims=True)
            p_ref[u] = jnp.exp2(s - m).astype(BF16)
        for u in range(pair_block):
            c0, q0 = pairs[u]
            vt = jnp.concatenate([vt_ref[c0 + c] for c in range(NA_KEY_CHUNKS)], axis=1)
            ot = jnp.dot(jnp.concatenate([vt, ones_rows], axis=0), p_ref[u], preferred_element_type=F32)
            ot = ot[:NA_HEAD_DIM] * (1.0 / ot[NA_HEAD_DIM:NA_HEAD_DIM + 1])
            o = ot.T * g_ref[pl.ds(q0, ct), :].astype(F32)
            o_ref[pl.ds(q0, ct), :] = o.astype(o_ref.dtype)
        return carry

    lax.fori_loop(0, n_pairs // pair_block, block, 0)


def _na_bias_table(rpb, rows):
    n_pairs = rows // NA_PAIR_ROWS
    cols = np.arange(GRID_W)
    col_start = np.clip(cols - NA_WIN_COLS // 2, 0, GRID_W - NA_WIN_COLS)
    col_valid = (cols[None, :] >= col_start[:, None]) & (cols[None, :] < col_start[:, None] + NA_WIN_COLS)
    pad = GRID_W - NA_WIN_COLS
    rp = jnp.pad(rpb.astype(F32), ((0, 0), (0, 0), (pad, pad)))
    toep = jnp.stack([rp[:, :, GRID_W - 1 - qc: 2 * GRID_W - 1 - qc] for qc in range(GRID_W)], axis=2)
    toep = jnp.where(col_valid[None, None], toep, -jnp.inf)
    toep_t = toep.transpose(0, 1, 3, 2)
    masked = jnp.full((NA_HEADS, GRID_W, GRID_W), -jnp.inf, F32)
    tables = []
    for pr in (0, 1, 2, n_pairs - 2, n_pairs - 1):
        base = int(np.clip(NA_PAIR_ROWS * pr - NA_WIN_ROWS // 2, 0, rows - NA_KEY_ROWS))
        slabs = []
        for w in range(NA_KEY_ROWS):
            per_row = []
            for i in range(NA_PAIR_ROWS):
                r_q = NA_PAIR_ROWS * pr + i
                rs = int(np.clip(r_q - NA_WIN_ROWS // 2, 0, rows - NA_WIN_ROWS))
                kr = base + w
                if rs <= kr < rs + NA_WIN_ROWS:
                    per_row.append(toep_t[:, kr - r_q + NA_WIN_ROWS - 1])
                else:
                    per_row.append(masked)
            slabs.append(jnp.concatenate(per_row, axis=-1))
        tables.append(jnp.concatenate(slabs, axis=1))
    return jnp.stack(tables, axis=1) * LOG2E


def _na_attention(proj, bias, batch, seq):
    rows = seq // GRID_W
    n_pairs = rows // NA_PAIR_ROWS
    pair_block = min(NA_PAIR_BLOCK, n_pairs)
    assert rows % NA_PAIR_ROWS == 0 and rows >= NA_KEY_ROWS + 2 and n_pairs % pair_block == 0
    hd = NA_HEAD_DIM
    blk = lambda off: pl.BlockSpec((seq, hd), lambda b, h, off=off: (b, off // hd + h))
    return pl.pallas_call(
        functools.partial(_na_kernel, rows=rows),
        grid=(batch, NA_HEADS),
        in_specs=[blk(OFF_NA_Q), blk(OFF_NA_K), blk(OFF_NA_V), blk(OFF_NA_G),
                  pl.BlockSpec((None, NA_VARIANTS, NA_KEYS, NA_PAIR_TOKENS), lambda b, h: (h, 0, 0, 0))],
        out_specs=pl.BlockSpec((seq, hd), lambda b, h: (b, h)),
        out_shape=jax.ShapeDtypeStruct((batch * seq, NA_WIDTH), BF16),
        scratch_shapes=[pltpu.VMEM((n_pairs, hd, NA_PAIR_TOKENS), BF16),
                        pltpu.VMEM((pair_block, NA_KEYS, NA_PAIR_TOKENS), F32),
                        pltpu.VMEM((pair_block, NA_KEYS, NA_PAIR_TOKENS), BF16)],
        compiler_params=_params(("arbitrary", "arbitrary")),
        name="na_attention",
    )(proj, proj, proj, proj, bias)


RET_UNROLL = 32


def _ret_kernel(ld_ref, q_ref, k_ref, v_ref, g_ref, gain_ref, cos_ref, sin_ref, o_ref,
                qr_ref, kr_ref, kv_ref, st_ref, dsum_ref, qd_ref, kd_ref, *, n_chunks):
    c_len, dk = RET_CHUNK, RET_QK_DIM
    unroll = min(RET_UNROLL, n_chunks)
    h = pl.program_id(1)
    ldf = -jnp.abs(ld_ref[0, h])
    ldb = -jnp.abs(ld_ref[1, h])
    row = lax.broadcasted_iota(jnp.int32, (c_len, c_len), 0).astype(F32)
    col = lax.broadcasted_iota(jnp.int32, (c_len, c_len), 1).astype(F32)
    diff = row - col
    dsum_ref[...] = (jnp.where(diff >= 0, jnp.exp(ldf * jnp.maximum(diff, 0.0)), 0.0)
                     + jnp.where(diff <= 0, jnp.exp(ldb * jnp.maximum(-diff, 0.0)), 0.0))
    qd_ref[:, :dk] = jnp.exp(ldf * (row + 1.0))
    qd_ref[:, dk:] = jnp.exp(ldb * (c_len - row))
    kd_ref[:, :dk] = jnp.exp(ldf * (c_len - 1.0 - row))
    kd_ref[:, dk:] = jnp.exp(ldb * row)
    zero_row = jnp.zeros((1, RET_V_DIM), F32)
    cdf = jnp.exp(zero_row + ldf * c_len)
    cdb = jnp.exp(zero_row + ldb * c_len)
    chunk = lambda c: pl.ds(pl.multiple_of(c * c_len, c_len), c_len)

    def rotate(c, carry):
        rows = chunk(c)
        cos, sin = cos_ref[rows, :], sin_ref[rows, :]
        for src, dst in ((q_ref, qr_ref), (k_ref, kr_ref)):
            t = src[rows, :].astype(F32)
            dst[rows, :] = t * cos + pltpu.roll(t, dk // 2, axis=1) * sin
        return carry

    lax.fori_loop(0, n_chunks, rotate, 0, unroll=unroll)

    def chunk_kv(c, carry):
        rows = chunk(c)
        k = kr_ref[rows, :]
        k2 = (jnp.concatenate([k, k], axis=1) * kd_ref[...]).astype(BF16)
        kv_ref[c] = lax.dot_general(k2, v_ref[rows, :], (((0,), (0,)), ((), ())),
                                    preferred_element_type=F32)
        return carry

    lax.fori_loop(0, n_chunks, chunk_kv, 0, unroll=unroll)

    def scan_fwd(c, s):
        st_ref[c, :dk, :] = s.astype(BF16)
        return s * cdf + kv_ref[c, :dk, :]

    def scan_bwd(t, s):
        c = n_chunks - 1 - t
        st_ref[c, dk:, :] = s.astype(BF16)
        return s * cdb + kv_ref[c, dk:, :]

    zero_state = jnp.zeros((dk, RET_V_DIM), F32)
    lax.fori_loop(0, n_chunks, scan_fwd, zero_state, unroll=unroll)
    lax.fori_loop(0, n_chunks, scan_bwd, zero_state, unroll=unroll)

    def chunk_out(c, carry):
        rows = chunk(c)
        q = qr_ref[rows, :]
        v = v_ref[rows, :]
        sc = lax.dot_general(q.astype(BF16), kr_ref[rows, :].astype(BF16), (((1,), (1,)), ((), ())),
                             preferred_element_type=F32)
        o = jnp.dot((sc * dsum_ref[...]).astype(BF16), v, preferred_element_type=F32)
        q2 = (jnp.concatenate([q, q], axis=1) * qd_ref[...]).astype(BF16)
        o = o + jnp.dot(q2, st_ref[c], preferred_element_type=F32)
        o = o * lax.rsqrt(jnp.mean(o * o, axis=-1, keepdims=True) + EPS)
        o = o * gain_ref[...] * g_ref[rows, :].astype(F32)
        o_ref[rows, :] = o.astype(o_ref.dtype)
        return carry

    lax.fori_loop(0, n_chunks, chunk_out, 0, unroll=unroll)


def _retention(proj, ld, gain, cos_t, sin_t, batch, seq):
    n_chunks = seq // RET_CHUNK
    dk, dv, c_len = RET_QK_DIM, RET_V_DIM, RET_CHUNK
    assert n_chunks % min(RET_UNROLL, n_chunks) == 0
    qk_blk = lambda off: pl.BlockSpec((seq, dk), lambda b, h, off=off: (b, off // dk + h))
    v_blk = lambda off: pl.BlockSpec((seq, dv), lambda b, h, off=off: (b, off // dv + h))
    rope_blk = pl.BlockSpec((seq, dk), lambda b, h: (0, 0))
    return pl.pallas_call(
        functools.partial(_ret_kernel, n_chunks=n_chunks),
        grid=(batch, RET_HEADS),
        in_specs=[pl.BlockSpec(memory_space=pltpu.SMEM),
                  qk_blk(OFF_R_Q), qk_blk(OFF_R_K), v_blk(OFF_R_V), v_blk(OFF_R_G),
                  pl.BlockSpec((None, 1, dv), lambda b, h: (h, 0, 0)), rope_blk, rope_blk],
        out_specs=pl.BlockSpec((seq, dv), lambda b, h: (b, h)),
        out_shape=jax.ShapeDtypeStruct((batch * seq, RET_V_WIDTH), BF16),
        scratch_shapes=[pltpu.VMEM((seq, dk), F32),
                        pltpu.VMEM((seq, dk), F32),
                        pltpu.VMEM((n_chunks, 2 * dk, dv), F32),
                        pltpu.VMEM((n_chunks, 2 * dk, dv), BF16),
                        pltpu.VMEM((c_len, c_len), F32),
                        pltpu.VMEM((c_len, 2 * dk), F32),
                        pltpu.VMEM((c_len, 2 * dk), F32)],
        compiler_params=_params(("arbitrary", "arbitrary")),
        name="retention",
    )(ld, proj, proj, proj, proj, gain.astype(F32).reshape(RET_HEADS, 1, dv), cos_t, sin_t)


def _merge_kernel(a_ref, b_ref, wa_ref, wb_ref, ga_ref, gb_ref, o_ref):
    ya = jnp.dot(a_ref[...], wa_ref[...], preferred_element_type=F32)
    yb = jnp.dot(b_ref[...], wb_ref[...], preferred_element_type=F32)
    o_ref[...] = (ga_ref[...].astype(F32) * ya + gb_ref[...].astype(F32) * yb).astype(o_ref.dtype)


def _merge(a, b, wa, wb, proj):
    t = a.shape[0]
    tm = min(MAX_TILE_M, t)
    ja, jb = OFF_GATE_A // TILE_N, OFF_GATE_B // TILE_N
    return pl.pallas_call(
        _merge_kernel,
        grid=(t // tm, D_MODEL // TILE_N),
        in_specs=[pl.BlockSpec((tm, NA_WIDTH), lambda i, j: (i, 0)),
                  pl.BlockSpec((tm, RET_V_WIDTH), lambda i, j: (i, 0)),
                  pl.BlockSpec((NA_WIDTH, TILE_N), lambda i, j: (0, j)),
                  pl.BlockSpec((RET_V_WIDTH, TILE_N), lambda i, j: (0, j)),
                  pl.BlockSpec((tm, TILE_N), lambda i, j: (i, ja + j)),
                  pl.BlockSpec((tm, TILE_N), lambda i, j: (i, jb + j))],
        out_specs=pl.BlockSpec((tm, TILE_N), lambda i, j: (i, j)),
        out_shape=jax.ShapeDtypeStruct((t, D_MODEL), BF16),
        compiler_params=_params(("arbitrary", "arbitrary")),
        name="merge",
    )(a, b, wa, wb, proj, proj)


def _matmul_kernel(x_ref, w_ref, *refs):
    cast_src, (o_ref,), cast_dst = _split_side_refs(refs, 1)
    _side_cast(cast_src, cast_dst)
    o_ref[...] = jnp.dot(x_ref[...], w_ref[...], preferred_element_type=F32).astype(o_ref.dtype)


def _out_proj(merged, w, side_ws=(), side_layer=0):
    t = merged.shape[0]
    tm = min(MAX_TILE_M, t)
    grid = (t // tm, D_MODEL // TILE_N)
    side_in, side_out, side_shapes = _side_cast_specs(side_ws, side_layer, *grid)
    return pl.pallas_call(
        _matmul_kernel,
        grid=grid,
        in_specs=[pl.BlockSpec((tm, D_MODEL), lambda i, j: (i, 0)),
                  pl.BlockSpec((D_MODEL, TILE_N), lambda i, j: (0, j))] + side_in,
        out_specs=[pl.BlockSpec((tm, TILE_N), lambda i, j: (i, j))] + side_out,
        out_shape=[jax.ShapeDtypeStruct((t, D_MODEL), BF16)] + side_shapes,
        compiler_params=_params(("arbitrary", "arbitrary")),
        name="out_proj",
    )(merged, w, *side_ws)


def _post_norm(x, y, g):
    t = x.shape[0]
    row = pl.BlockSpec((ROW_TILE, D_MODEL), lambda i: (i, 0))
    return pl.pallas_call(
        _post_kernel,
        grid=(t // ROW_TILE,),
        in_specs=[row, row, pl.BlockSpec((1, D_MODEL), lambda i: (0, 0))],
        out_specs=[row, pl.BlockSpec((ROW_TILE, LANES), lambda i: (i, 0))],
        out_shape=[jax.ShapeDtypeStruct((t, D_MODEL), BF16), jax.ShapeDtypeStruct((t, LANES), F32)],
        compiler_params=_params(("arbitrary",)),
        name="post_norm",
    )(x, y, g.reshape(1, D_MODEL))


PLE_TILE_N = 512


def _ple_kernel(xb_ref, wg_ref, p_ref, wp_ref, x_ref, y_ref, rs_ref, g_ref, *refs):
    cast_src, (o_ref,), cast_dst = _split_side_refs(refs, 1)
    _side_cast(cast_src, cast_dst)
    z = jnp.dot(xb_ref[...], wg_ref[...], preferred_element_type=F32)
    gate = 0.5 * jnp.tanh(0.5 * z) + 0.5
    upd = gate * jnp.dot(p_ref[...].astype(BF16), wp_ref[...], preferred_element_type=F32)
    rs = rs_ref[...]
    for c in range(PLE_TILE_N // LANES):
        sl = slice(c * LANES, (c + 1) * LANES)
        x1 = x_ref[:, sl] + y_ref[:, sl].astype(F32) * rs * g_ref[:, sl]
        o_ref[:, sl] = x1 + upd[:, sl]


def _ple(x, y, rs, g, x1b, p, wg, wp, side_ws=(), side_layer=0):
    t = x.shape[0]
    tm, tn = min(MAX_TILE_M, t), PLE_TILE_N
    grid = (t // tm, D_MODEL // tn)
    tile = pl.BlockSpec((tm, tn), lambda i, j: (i, j))
    side_in, side_out, side_shapes = _side_cast_specs(side_ws, side_layer, *grid)
    return pl.pallas_call(
        _ple_kernel,
        grid=grid,
        in_specs=[pl.BlockSpec((tm, D_MODEL), lambda i, j: (i, 0)),
                  pl.BlockSpec((D_MODEL, tn), lambda i, j: (0, j)),
                  pl.BlockSpec((tm, PLE_DIM), lambda i, j: (i, 0)),
                  pl.BlockSpec((PLE_DIM, tn), lambda i, j: (0, j)),
                  tile, tile,
                  pl.BlockSpec((tm, LANES), lambda i, j: (i, 0)),
                  pl.BlockSpec((1, tn), lambda i, j: (0, j))] + side_in,
        out_specs=[tile] + side_out,
        out_shape=[jax.ShapeDtypeStruct((t, D_MODEL), F32)] + side_shapes,
        compiler_params=_params(("arbitrary", "arbitrary")),
        name="ple",
    )(x1b, wg, p, wp, x, y, rs, g.reshape(1, D_MODEL), *side_ws)


def _rope_tables(seq):
    half = RET_QK_DIM // 2
    inv = ROPE_BASE ** (-jnp.arange(half, dtype=F32) / half)
    ang = jnp.arange(seq).astype(F32)[:, None] * inv[None, :]
    cos, sin = jnp.cos(ang), jnp.sin(ang)
    return jnp.concatenate([cos, cos], axis=-1), jnp.concatenate([-sin, sin], axis=-1)


SIDE_CAST_HOSTS = {"in_proj": ("w_in",), "out_proj": ("w_out", "w_proj_a", "w_proj_b"), "ple": ("w_ple_gate",)}


def kernel(x_prompt, x_sample, p_prompt, p_sample, w_in, ln_pre, ln_post, na_rpb, ret_log_decay_fwd,
           ret_log_decay_bwd, ret_gn_gain, w_proj_a, w_proj_b, w_out, w_ple, w_ple_gate):
    depth = w_in.shape[0]
    stacked = {"w_in": w_in, "w_proj_a": w_proj_a, "w_proj_b": w_proj_b, "w_out": w_out,
               "w_ple": w_ple, "w_ple_gate": w_ple_gate}
    ps = [p_prompt, p_sample]
    shapes = [x.shape[:2] for x in (x_prompt, x_sample)]
    xs = [x.reshape(-1, D_MODEL) for x in (x_prompt, x_sample)]
    rope = {seq: _rope_tables(seq) for _, seq in shapes}
    small = [{"ln_pre": ln_pre[i].astype(F32),
              "ln_post": ln_post[i].astype(F32),
              "na_bias": {seq // GRID_W: _na_bias_table(na_rpb[i], seq // GRID_W) for _, seq in shapes},
              "ret_ld": jnp.stack([ret_log_decay_fwd[i], ret_log_decay_bwd[i]]).astype(F32),
              "ret_gn": ret_gn_gain[i]} for i in range(depth)]
    wts = [None] * depth
    wts[0] = {name: _cast_layer_bf16(w, 0) for name, w in stacked.items()}
    A, B = 0, 1

    def side(host, tr, i, hosting):
        (batch, seq), t = shapes[tr], shapes[tr][0] * shapes[tr][1]
        grid = {"in_proj": _in_proj_grid(t, seq)[1],
                "out_proj": (t // min(MAX_TILE_M, t), D_MODEL // TILE_N),
                "ple": (t // min(MAX_TILE_M, t), D_MODEL // PLE_TILE_N)}[host]
        names = tuple(n for n in SIDE_CAST_HOSTS[host] if hosting and _side_cast_fits(stacked[n], *grid))
        return names, dict(side_ws=tuple(stacked[n] for n in names), side_layer=i + 1)

    def in_stage(tr, i, hosting, post):
        names, kw = side("in_proj", tr, i, hosting)
        xn = _rmsnorm_pre(xs[tr], small[i]["ln_pre"])
        proj, cast, post_out = _in_proj(xn, wts[i]["w_in"], shapes[tr][1], post=post, **kw)
        return proj, dict(zip(names, cast)), post_out

    def mixer_stage(tr, i, proj, hosting):
        batch, seq = shapes[tr]
        names, kw = side("out_proj", tr, i, hosting)
        a = _na_attention(proj, small[i]["na_bias"][seq // GRID_W], batch, seq)
        b = _retention(proj, small[i]["ret_ld"], small[i]["ret_gn"], *rope[seq], batch, seq)
        merged = _merge(a, b, wts[i]["w_proj_a"], wts[i]["w_proj_b"], proj)
        y, *cast = _out_proj(merged, wts[i]["w_out"], **kw)
        return y, dict(zip(names, cast))

    def ple_stage(tr, i, y, x1b, rs, hosting):
        names, kw = side("ple", tr, i, hosting)
        t = xs[tr].shape[0]
        xs[tr], *cast = _ple(xs[tr], y, rs, small[i]["ln_post"], x1b, ps[tr][i].reshape(t, PLE_DIM),
                             wts[i]["w_ple_gate"], wts[i]["w_ple"], **kw)
        return dict(zip(names, cast))

    y_b = None
    for i in range(depth):
        hosting = i + 1 < depth
        post_b = None if y_b is None else (xs[B], y_b, small[i - 1]["ln_post"])
        proj_a, casts, post_out_b = in_stage(A, i, hosting, post_b)
        if y_b is not None:
            ple_stage(B, i - 1, y_b, *post_out_b, False)
        y_a, cast = mixer_stage(A, i, proj_a, hosting)
        casts.update(cast)
        proj_b, _, post_out_a = in_stage(B, i, False, (xs[A], y_a, small[i]["ln_post"]))
        casts.update(ple_stage(A, i, y_a, *post_out_a, hosting))
        y_b, _ = mixer_stage(B, i, proj_b, False)
        if hosting:
            wts[i + 1] = {name: casts[name] if name in casts else _cast_layer_bf16(w, i + 1)
                          for name, w in stacked.items()}
    x1b, rs = _post_norm(xs[B], y_b, small[depth - 1]["ln_post"])
    ple_stage(B, depth - 1, y_b, x1b, rs, False)
    return tuple(x.reshape(batch, seq, D_MODEL) for x, (batch, seq) in zip(xs, shapes))
```

```python
import functools

import numpy as np
import jax
import jax.numpy as jnp
from jax import lax
from jax.experimental import pallas as pl
from jax.experimental.pallas import tpu as pltpu

D_MODEL = 4096
GRID_W = 64
PLE_DIM = 256
NA_HEADS = 16
NA_HEAD_DIM = 128
NA_WIDTH = NA_HEADS * NA_HEAD_DIM
NA_WIN_ROWS = 8
NA_WIN_COLS = 16
RET_HEADS = 8
RET_QK_DIM = 128
RET_V_DIM = 256
RET_QK_WIDTH = RET_HEADS * RET_QK_DIM
RET_V_WIDTH = RET_HEADS * RET_V_DIM
RET_CHUNK = 128
ROPE_BASE = 10000.0
EPS = 1e-6

OFF_NA_Q = 0
OFF_NA_K = OFF_NA_Q + NA_WIDTH
OFF_NA_V = OFF_NA_K + NA_WIDTH
OFF_NA_G = OFF_NA_V + NA_WIDTH
OFF_R_Q = OFF_NA_G + NA_WIDTH
OFF_R_K = OFF_R_Q + RET_QK_WIDTH
OFF_R_V = OFF_R_K + RET_QK_WIDTH
OFF_R_G = OFF_R_V + RET_V_WIDTH
OFF_GATE_A = OFF_R_G + RET_V_WIDTH
OFF_GATE_B = OFF_GATE_A + D_MODEL
IN_WIDTH = OFF_GATE_B + D_MODEL

TILE_N = 1024
MAX_TILE_M = 1024
ROW_TILE = 256
V7X_VMEM_LIMIT_BYTES = 56 * 1024 * 1024
LANES = 128

F32 = jnp.float32
BF16 = jnp.bfloat16


def _params(semantics, vmem=V7X_VMEM_LIMIT_BYTES):
    return pltpu.CompilerParams(dimension_semantics=semantics, vmem_limit_bytes=vmem)


CAST_BLOCK_BYTES = 8 * 1024 * 1024


def _cast_kernel(w_ref, o_ref):
    o_ref[...] = w_ref[...].astype(o_ref.dtype)


def _cast_layer_bf16(w, layer):
    _, k, n = w.shape
    rb = max(r for r in range(16, k + 1, 16) if k % r == 0 and (r * n * 4 <= CAST_BLOCK_BYTES or r == 16))
    return pl.pallas_call(
        _cast_kernel,
        grid=(k // rb,),
        in_specs=[pl.BlockSpec((None, rb, n), lambda i: (layer, i, 0))],
        out_specs=pl.BlockSpec((rb, n), lambda i: (i, 0)),
        out_shape=jax.ShapeDtypeStruct((k, n), BF16),
        compiler_params=_params(("arbitrary",)),
        name="cast_bf16",
    )(w)


SUBLANES_BF16 = 16


SIDE_CAST_MAX_BYTES = 1024 * 1024


def _side_cast_fits(w, ni, nj):
    _, k, n = w.shape
    return (k % ni == 0 and n % nj == 0 and (k // ni) % SUBLANES_BF16 == 0 and (n // nj) % LANES == 0
            and (k // ni) * (n // nj) * 4 <= SIDE_CAST_MAX_BYTES)


def _side_cast_specs(ws, layer, ni, nj):
    in_specs, out_specs, out_shapes = [], [], []
    for w in ws:
        _, k, n = w.shape
        blk = (k // ni, n // nj)
        in_specs.append(pl.BlockSpec((None,) + blk, lambda i, j: (layer, i, j)))
        out_specs.append(pl.BlockSpec(blk, lambda i, j: (i, j)))
        out_shapes.append(jax.ShapeDtypeStruct((k, n), BF16))
    return in_specs, out_specs, out_shapes


def _split_side_refs(refs, n_out):
    n = (len(refs) - n_out) // 2
    return refs[:n], refs[n:n + n_out], refs[n + n_out:]


def _side_cast(src_refs, dst_refs):
    for src, dst in zip(src_refs, dst_refs):
        dst[...] = src[...].astype(dst.dtype)


def _rmsnorm_kernel(x_ref, g_ref, o_ref):
    x = x_ref[...]
    ms = jnp.mean(x * x, axis=-1, keepdims=True)
    o_ref[...] = (x * lax.rsqrt(ms + EPS) * g_ref[...]).astype(o_ref.dtype)


def _rmsnorm_pre(x, g):
    t = x.shape[0]
    return pl.pallas_call(
        _rmsnorm_kernel,
        grid=(t // ROW_TILE,),
        in_specs=[pl.BlockSpec((ROW_TILE, D_MODEL), lambda i: (i, 0)),
                  pl.BlockSpec((1, D_MODEL), lambda i: (0, 0))],
        out_specs=pl.BlockSpec((ROW_TILE, D_MODEL), lambda i: (i, 0)),
        out_shape=jax.ShapeDtypeStruct((t, D_MODEL), BF16),
        compiler_params=_params(("arbitrary",)),
        name="rmsnorm_pre",
    )(x, g.reshape(1, D_MODEL))


_J_NA_K = OFF_NA_K // TILE_N
_J_NA_G = OFF_NA_G // TILE_N
_J_R_Q = OFF_R_Q // TILE_N
_J_R_K = OFF_R_K // TILE_N
_J_R_G = OFF_R_G // TILE_N
_J_GATE = OFF_GATE_A // TILE_N
assert NA_HEAD_DIM == RET_QK_DIM
QK_SCALE = NA_HEAD_DIM ** -0.5
LOG2E = float(np.log2(np.e))
NA_Q_SCALE = QK_SCALE * LOG2E


def _post_norm_rows(x_ref, y_ref, g_ref, ob_ref, rs_ref):
    y = y_ref[...].astype(F32)
    rs = lax.rsqrt(jnp.mean(y * y, axis=-1, keepdims=True) + EPS)
    x1b = (x_ref[...] + y * rs * g_ref[...]).astype(ob_ref.dtype)
    ob_ref[...] = x1b
    rs_ref[...] = jnp.broadcast_to(rs, rs_ref.shape)
    return x1b


def _post_kernel(x_ref, y_ref, g_ref, ob_ref, rs_ref):
    _post_norm_rows(x_ref, y_ref, g_ref, ob_ref, rs_ref)


def _post_slab_rows(t_post, n_steps):
    return min(r for r in range(SUBLANES_BF16, t_post + 1, SUBLANES_BF16)
               if t_post % r == 0 and t_post // r <= n_steps)


def _inproj_kernel(x_ref, w_ref, *refs, n_cast, has_post):
    n_in = n_cast + (3 if has_post else 0)
    cast_src, post_in = refs[:n_cast], refs[n_cast:n_in]
    o_ref, cast_dst, post_out = refs[n_in], refs[n_in + 1:n_in + 1 + n_cast], refs[n_in + 1 + n_cast:]
    j = pl.program_id(1)
    is_gate = j >= _J_GATE
    is_silu = ((j >= _J_NA_G) & (j < _J_R_Q)) | ((j >= _J_R_G) & (j < _J_GATE))
    scale = jnp.where(j < _J_NA_K, NA_Q_SCALE, jnp.where(j == _J_R_K, QK_SCALE, 1.0))

    def step(epilogue):
        _side_cast(cast_src, cast_dst)
        x = x_ref[...]
        for half in range(2):
            cols = slice(half * (TILE_N // 2), (half + 1) * (TILE_N // 2))
            if half == 1 and has_post:
                x1b = _post_norm_rows(*post_in, *post_out)
                never = pl.program_id(0) < 0
                x = jnp.concatenate([jnp.where(never, x1b, x[:x1b.shape[0]]), x[x1b.shape[0]:]], axis=0)
            y = jnp.dot(x, w_ref[:, cols], preferred_element_type=F32)
            o_ref[:, cols] = epilogue(y).astype(o_ref.dtype)

    pl.when(is_gate)(lambda: step(lambda y: 0.5 * jnp.tanh(0.5 * y) + 0.5))
    pl.when(is_silu)(lambda: step(lambda y: y * (0.5 * jnp.tanh(0.5 * y) + 0.5)))
    pl.when(jnp.logical_not(is_gate | is_silu))(lambda: step(lambda y: y * scale))


def _in_proj_grid(t, seq):
    tm = min(MAX_TILE_M, seq)
    return tm, (t // tm, IN_WIDTH // TILE_N)


def _in_proj(xn, w, seq, side_ws=(), side_layer=0, post=None):
    t = xn.shape[0]
    tm, grid = _in_proj_grid(t, seq)
    tn = TILE_N
    side_in, side_out, side_shapes = _side_cast_specs(side_ws, side_layer, *grid)
    post_args, post_in, post_out, post_shapes = (), [], [], []
    if post is not None:
        px, py, pg = post
        t_post = px.shape[0]
        rows = _post_slab_rows(t_post, grid[0] * grid[1])
        slab = lambda i, j: (jnp.minimum(i * grid[1] + j, t_post // rows - 1), 0)
        post_args = (px, py, pg.reshape(1, D_MODEL))
        post_in = [pl.BlockSpec((rows, D_MODEL), slab), pl.BlockSpec((rows, D_MODEL), slab),
                   pl.BlockSpec((1, D_MODEL), lambda i, j: (0, 0))]
        post_out = [pl.BlockSpec((rows, D_MODEL), slab), pl.BlockSpec((rows, LANES), slab)]
        post_shapes = [jax.ShapeDtypeStruct((t_post, D_MODEL), BF16), jax.ShapeDtypeStruct((t_post, LANES), F32)]
    outs = pl.pallas_call(
        functools.partial(_inproj_kernel, n_cast=len(side_ws), has_post=post is not None),
        grid=grid,
        in_specs=[pl.BlockSpec((tm, D_MODEL), lambda i, j: (i, 0)),
                  pl.BlockSpec((D_MODEL, tn), lambda i, j: (0, j))] + side_in + post_in,
        out_specs=[pl.BlockSpec((tm, tn), lambda i, j: (i, j))] + side_out + post_out,
        out_shape=[jax.ShapeDtypeStruct((t, IN_WIDTH), BF16)] + side_shapes + post_shapes,
        compiler_params=_params(("arbitrary", "arbitrary")),
        name="in_proj",
    )(xn, w, *side_ws, *post_args)
    n_cast = len(side_ws)
    return outs[0], list(outs[1:1 + n_cast]), (tuple(outs[1 + n_cast:]) if post is not None else None)


NA_PAIR_ROWS = 2
NA_PAIR_TOKENS = NA_PAIR_ROWS * GRID_W
NA_KEY_ROWS = NA_WIN_ROWS + NA_PAIR_ROWS
NA_KEY_CHUNKS = NA_KEY_ROWS // NA_PAIR_ROWS
NA_KEYS = NA_KEY_ROWS * GRID_W
NA_VARIANTS = 5
NA_PAIR_BLOCK = 32
NA_SUM_ROWS = 16


def _na_kernel(q_ref, k_ref, v_ref, g_ref, bias_ref, o_ref, vt_ref, s_ref, p_ref, *, rows):
    n_pairs = rows // NA_PAIR_ROWS
    n_chunks = n_pairs
    ct = NA_PAIR_TOKENS
    nt = (((1,), (1,)), ((), ()))
    pair_block = s_ref.shape[0]

    def transpose_chunk(c, carry):
        r0 = pl.multiple_of(c * ct, ct)
        vt_ref[c] = v_ref[pl.ds(r0, ct), :].astype(F32).T.astype(BF16)
        return carry

    lax.fori_loop(0, n_chunks, transpose_chunk, 0, unroll=pair_block)
    ones_rows = jnp.ones((NA_SUM_ROWS, NA_KEYS), BF16)

    def block(blk, carry):
        pairs = []
        for u in range(pair_block):
            pr = blk * pair_block + u
            c0 = jnp.clip(pr - NA_WIN_ROWS // 4, 0, n_chunks - NA_KEY_CHUNKS)
            var = jnp.where(pr < 2, pr, jnp.where(pr >= n_pairs - 2, pr - (n_pairs - 2) + 3, 2))
            q0 = pl.multiple_of(pr * ct, ct)
            k0 = pl.multiple_of(c0 * ct, ct)
            pairs.append((c0, q0))
            s = lax.dot_general(k_ref[pl.ds(k0, NA_KEYS), :], q_ref[pl.ds(q0, ct), :], nt,
                                preferred_element_type=F32)
            s_ref[u] = s + bias_ref[var]
        for u in range(pair_block):
            s = s_ref[u]
            m = jnp.max(s, axis=0, keepdims=True)
            p_ref[u] = jnp.exp2(s - m).astype(BF16)
        for u in range(pair_block):
            c0, q0 = pairs[u]
            vt = jnp.concatenate([vt_ref[c0 + c] for c in range(NA_KEY_CHUNKS)], axis=1)
            ot = jnp.dot(jnp.concatenate([vt, ones_rows], axis=0), p_ref[u], preferred_element_type=F32)
            ot = ot[:NA_HEAD_DIM] * (1.0 / ot[NA_HEAD_DIM:NA_HEAD_DIM + 1])
            o = ot.T * g_ref[pl.ds(q0, ct), :].astype(F32)
            o_ref[pl.ds(q0, ct), :] = o.astype(o_ref.dtype)
        return carry

    lax.fori_loop(0, n_pairs // pair_block, block, 0)


def _na_bias_table(rpb, rows):
    n_pairs = rows // NA_PAIR_ROWS
    cols = np.arange(GRID_W)
    col_start = np.clip(cols - NA_WIN_COLS // 2, 0, GRID_W - NA_WIN_COLS)
    col_valid = (cols[None, :] >= col_start[:, None]) & (cols[None, :] < col_start[:, None] + NA_WIN_COLS)
    pad = GRID_W - NA_WIN_COLS
    rp = jnp.pad(rpb.astype(F32), ((0, 0), (0, 0), (pad, pad)))
    toep = jnp.stack([rp[:, :, GRID_W - 1 - qc: 2 * GRID_W - 1 - qc] for qc in range(GRID_W)], axis=2)
    toep = jnp.where(col_valid[None, None], toep, -jnp.inf)
    toep_t = toep.transpose(0, 1, 3, 2)
    masked = jnp.full((NA_HEADS, GRID_W, GRID_W), -jnp.inf, F32)
    tables = []
    for pr in (0, 1, 2, n_pairs - 2, n_pairs - 1):
        base = int(np.clip(NA_PAIR_ROWS * pr - NA_WIN_ROWS // 2, 0, rows - NA_KEY_ROWS))
        slabs = []
        for w in range(NA_KEY_ROWS):
            per_row = []
            for i in range(NA_PAIR_ROWS):
                r_q = NA_PAIR_ROWS * pr + i
                rs = int(np.clip(r_q - NA_WIN_ROWS // 2, 0, rows - NA_WIN_ROWS))
                kr = base + w
                if rs <= kr < rs + NA_WIN_ROWS:
                    per_row.append(toep_t[:, kr - r_q + NA_WIN_ROWS - 1])
                else:
                    per_row.append(masked)
            slabs.append(jnp.concatenate(per_row, axis=-1))
        tables.append(jnp.concatenate(slabs, axis=1))
    return jnp.stack(tables, axis=1) * LOG2E


def _na_attention(proj, bias, batch, seq):
    rows = seq // GRID_W
    n_pairs = rows // NA_PAIR_ROWS
    pair_block = min(NA_PAIR_BLOCK, n_pairs)
    assert rows % NA_PAIR_ROWS == 0 and rows >= NA_KEY_ROWS + 2 and n_pairs % pair_block == 0
    hd = NA_HEAD_DIM
    blk = lambda off: pl.BlockSpec((seq, hd), lambda b, h, off=off: (b, off // hd + h))
    return pl.pallas_call(
        functools.partial(_na_kernel, rows=rows),
        grid=(batch, NA_HEADS),
        in_specs=[blk(OFF_NA_Q), blk(OFF_NA_K), blk(OFF_NA_V), blk(OFF_NA_G),
                  pl.BlockSpec((None, NA_VARIANTS, NA_KEYS, NA_PAIR_TOKENS), lambda b, h: (h, 0, 0, 0))],
        out_specs=pl.BlockSpec((seq, hd), lambda b, h: (b, h)),
        out_shape=jax.ShapeDtypeStruct((batch * seq, NA_WIDTH), BF16),
        scratch_shapes=[pltpu.VMEM((n_pairs, hd, NA_PAIR_TOKENS), BF16),
                        pltpu.VMEM((pair_block, NA_KEYS, NA_PAIR_TOKENS), F32),
                        pltpu.VMEM((pair_block, NA_KEYS, NA_PAIR_TOKENS), BF16)],
        compiler_params=_params(("arbitrary", "arbitrary")),
        name="na_attention",
    )(proj, proj, proj, proj, bias)


RET_UNROLL = 32


def _ret_kernel(ld_ref, q_ref, k_ref, v_ref, g_ref, gain_ref, cos_ref, sin_ref, o_ref,
                qr_ref, kr_ref, kv_ref, st_ref, dsum_ref, qd_ref, kd_ref, *, n_chunks):
    c_len, dk = RET_CHUNK, RET_QK_DIM
    unroll = min(RET_UNROLL, n_chunks)
    h = pl.program_id(1)
    ldf = -jnp.abs(ld_ref[0, h])
    ldb = -jnp.abs(ld_ref[1, h])
    row = lax.broadcasted_iota(jnp.int32, (c_len, c_len), 0).astype(F32)
    col = lax.broadcasted_iota(jnp.int32, (c_len, c_len), 1).astype(F32)
    diff = row - col
    dsum_ref[...] = (jnp.where(diff >= 0, jnp.exp(ldf * jnp.maximum(diff, 0.0)), 0.0)
                     + jnp.where(diff <= 0, jnp.exp(ldb * jnp.maximum(-diff, 0.0)), 0.0))
    qd_ref[:, :dk] = jnp.exp(ldf * (row + 1.0))
    qd_ref[:, dk:] = jnp.exp(ldb * (c_len - row))
    kd_ref[:, :dk] = jnp.exp(ldf * (c_len - 1.0 - row))
    kd_ref[:, dk:] = jnp.exp(ldb * row)
    zero_row = jnp.zeros((1, RET_V_DIM), F32)
    cdf = jnp.exp(zero_row + ldf * c_len)
    cdb = jnp.exp(zero_row + ldb * c_len)
    chunk = lambda c: pl.ds(pl.multiple_of(c * c_len, c_len), c_len)

    def rotate(c, carry):
        rows = chunk(c)
        cos, sin = cos_ref[rows, :], sin_ref[rows, :]
        for src, dst in ((q_ref, qr_ref), (k_ref, kr_ref)):
            t = src[rows, :].astype(F32)
            dst[rows, :] = t * cos + pltpu.roll(t, dk // 2, axis=1) * sin
        return carry

    lax.fori_loop(0, n_chunks, rotate, 0, unroll=unroll)

    def chunk_kv(c, carry):
        rows = chunk(c)
        k = kr_ref[rows, :]
        k2 = (jnp.concatenate([k, k], axis=1) * kd_ref[...]).astype(BF16)
        kv_ref[c] = lax.dot_general(k2, v_ref[rows, :], (((0,), (0,)), ((), ())),
                                    preferred_element_type=F32)
        return carry

    lax.fori_loop(0, n_chunks, chunk_kv, 0, unroll=unroll)

    def scan_fwd(c, s):
        st_ref[c, :dk, :] = s.astype(BF16)
        return s * cdf + kv_ref[c, :dk, :]

    def scan_bwd(t, s):
        c = n_chunks - 1 - t
        st_ref[c, dk:, :] = s.astype(BF16)
        return s * cdb + kv_ref[c, dk:, :]

    zero_state = jnp.zeros((dk, RET_V_DIM), F32)
    lax.fori_loop(0, n_chunks, scan_fwd, zero_state, unroll=unroll)
    lax.fori_loop(0, n_chunks, scan_bwd, zero_state, unroll=unroll)

    def chunk_out(c, carry):
        rows = chunk(c)
        q = qr_ref[rows, :]
        v = v_ref[rows, :]
        sc = lax.dot_general(q.astype(BF16), kr_ref[rows, :].astype(BF16), (((1,), (1,)), ((), ())),
                             preferred_element_type=F32)
        o = jnp.dot((sc * dsum_ref[...]).astype(BF16), v, preferred_element_type=F32)
        q2 = (jnp.concatenate([q, q], axis=1) * qd_ref[...]).astype(BF16)
        o = o + jnp.dot(q2, st_ref[c], preferred_element_type=F32)
        o = o * lax.rsqrt(jnp.mean(o * o, axis=-1, keepdims=True) + EPS)
        o = o * gain_ref[...] * g_ref[rows, :].astype(F32)
        o_ref[rows, :] = o.astype(o_ref.dtype)
        return carry

    lax.fori_loop(0, n_chunks, chunk_out, 0, unroll=unroll)


def _retention(proj, ld, gain, cos_t, sin_t, batch, seq):
    n_chunks = seq // RET_CHUNK
    dk, dv, c_len = RET_QK_DIM, RET_V_DIM, RET_CHUNK
    assert n_chunks % min(RET_UNROLL, n_chunks) == 0
    qk_blk = lambda off: pl.BlockSpec((seq, dk), lambda b, h, off=off: (b, off // dk + h))
    v_blk = lambda off: pl.BlockSpec((seq, dv), lambda b, h, off=off: (b, off // dv + h))
    rope_blk = pl.BlockSpec((seq, dk), lambda b, h: (0, 0))
    return pl.pallas_call(
        functools.partial(_ret_kernel, n_chunks=n_chunks),
        grid=(batch, RET_HEADS),
        in_specs=[pl.BlockSpec(memory_space=pltpu.SMEM),
                  qk_blk(OFF_R_Q), qk_blk(OFF_R_K), v_blk(OFF_R_V), v_blk(OFF_R_G),
                  pl.BlockSpec((None, 1, dv), lambda b, h: (h, 0, 0)), rope_blk, rope_blk],
        out_specs=pl.BlockSpec((seq, dv), lambda b, h: (b, h)),
        out_shape=jax.ShapeDtypeStruct((batch * seq, RET_V_WIDTH), BF16),
        scratch_shapes=[pltpu.VMEM((seq, dk), F32),
                        pltpu.VMEM((seq, dk), F32),
                        pltpu.VMEM((n_chunks, 2 * dk, dv), F32),
                        pltpu.VMEM((n_chunks, 2 * dk, dv), BF16),
                        pltpu.VMEM((c_len, c_len), F32),
                        pltpu.VMEM((c_len, 2 * dk), F32),
                        pltpu.VMEM((c_len, 2 * dk), F32)],
        compiler_params=_params(("arbitrary", "arbitrary")),
        name="retention",
    )(ld, proj, proj, proj, proj, gain.astype(F32).reshape(RET_HEADS, 1, dv), cos_t, sin_t)


def _merge_kernel(a_ref, b_ref, wa_ref, wb_ref, ga_ref, gb_ref, o_ref):
    ya = jnp.dot(a_ref[...], wa_ref[...], preferred_element_type=F32)
    yb = jnp.dot(b_ref[...], wb_ref[...], preferred_element_type=F32)
    o_ref[...] = (ga_ref[...].astype(F32) * ya + gb_ref[...].astype(F32) * yb).astype(o_ref.dtype)


def _merge(a, b, wa, wb, proj):
    t = a.shape[0]
    tm = min(MAX_TILE_M, t)
    ja, jb = OFF_GATE_A // TILE_N, OFF_GATE_B // TILE_N
    return pl.pallas_call(
        _merge_kernel,
        grid=(t // tm, D_MODEL // TILE_N),
        in_specs=[pl.BlockSpec((tm, NA_WIDTH), lambda i, j: (i, 0)),
                  pl.BlockSpec((tm, RET_V_WIDTH), lambda i, j: (i, 0)),
                  pl.BlockSpec((NA_WIDTH, TILE_N), lambda i, j: (0, j)),
                  pl.BlockSpec((RET_V_WIDTH, TILE_N), lambda i, j: (0, j)),
                  pl.BlockSpec((tm, TILE_N), lambda i, j: (i, ja + j)),
                  pl.BlockSpec((tm, TILE_N), lambda i, j: (i, jb + j))],
        out_specs=pl.BlockSpec((tm, TILE_N), lambda i, j: (i, j)),
        out_shape=jax.ShapeDtypeStruct((t, D_MODEL), BF16),
        compiler_params=_params(("arbitrary", "arbitrary")),
        name="merge",
    )(a, b, wa, wb, proj, proj)


def _matmul_kernel(x_ref, w_ref, *refs):
    cast_src, (o_ref,), cast_dst = _split_side_refs(refs, 1)
    _side_cast(cast_src, cast_dst)
    o_ref[...] = jnp.dot(x_ref[...], w_ref[...], preferred_element_type=F32).astype(o_ref.dtype)


def _out_proj(merged, w, side_ws=(), side_layer=0):
    t = merged.shape[0]
    tm = min(MAX_TILE_M, t)
    grid = (t // tm, D_MODEL // TILE_N)
    side_in, side_out, side_shapes = _side_cast_specs(side_ws, side_layer, *grid)
    return pl.pallas_call(
        _matmul_kernel,
        grid=grid,
        in_specs=[pl.BlockSpec((tm, D_MODEL), lambda i, j: (i, 0)),
                  pl.BlockSpec((D_MODEL, TILE_N), lambda i, j: (0, j))] + side_in,
        out_specs=[pl.BlockSpec((tm, TILE_N), lambda i, j: (i, j))] + side_out,
        out_shape=[jax.ShapeDtypeStruct((t, D_MODEL), BF16)] + side_shapes,
        compiler_params=_params(("arbitrary", "arbitrary")),
        name="out_proj",
    )(merged, w, *side_ws)


def _post_norm(x, y, g):
    t = x.shape[0]
    row = pl.BlockSpec((ROW_TILE, D_MODEL), lambda i: (i, 0))
    return pl.pallas_call(
        _post_kernel,
        grid=(t // ROW_TILE,),
        in_specs=[row, row, pl.BlockSpec((1, D_MODEL), lambda i: (0, 0))],
        out_specs=[row, pl.BlockSpec((ROW_TILE, LANES), lambda i: (i, 0))],
        out_shape=[jax.ShapeDtypeStruct((t, D_MODEL), BF16), jax.ShapeDtypeStruct((t, LANES), F32)],
        compiler_params=_params(("arbitrary",)),
        name="post_norm",
    )(x, y, g.reshape(1, D_MODEL))


PLE_TILE_N = 512


def _ple_kernel(xb_ref, wg_ref, p_ref, wp_ref, x_ref, y_ref, rs_ref, g_ref, *refs):
    cast_src, (o_ref,), cast_dst = _split_side_refs(refs, 1)
    _side_cast(cast_src, cast_dst)
    z = jnp.dot(xb_ref[...], wg_ref[...], preferred_element_type=F32)
    gate = 0.5 * jnp.tanh(0.5 * z) + 0.5
    upd = gate * jnp.dot(p_ref[...].astype(BF16), wp_ref[...], preferred_element_type=F32)
    rs = rs_ref[...]
    for c in range(PLE_TILE_N // LANES):
        sl = slice(c * LANES, (c + 1) * LANES)
        x1 = x_ref[:, sl] + y_ref[:, sl].astype(F32) * rs * g_ref[:, sl]
        o_ref[:, sl] = x1 + upd[:, sl]


def _ple(x, y, rs, g, x1b, p, wg, wp, side_ws=(), side_layer=0):
    t = x.shape[0]
    tm, tn = min(MAX_TILE_M, t), PLE_TILE_N
    grid = (t // tm, D_MODEL // tn)
    tile = pl.BlockSpec((tm, tn), lambda i, j: (i, j))
    side_in, side_out, side_shapes = _side_cast_specs(side_ws, side_layer, *grid)
    return pl.pallas_call(
        _ple_kernel,
        grid=grid,
        in_specs=[pl.BlockSpec((tm, D_MODEL), lambda i, j: (i, 0)),
                  pl.BlockSpec((D_MODEL, tn), lambda i, j: (0, j)),
                  pl.BlockSpec((tm, PLE_DIM), lambda i, j: (i, 0)),
                  pl.BlockSpec((PLE_DIM, tn), lambda i, j: (0, j)),
                  tile, tile,
                  pl.BlockSpec((tm, LANES), lambda i, j: (i, 0)),
                  pl.BlockSpec((1, tn), lambda i, j: (0, j))] + side_in,
        out_specs=[tile] + side_out,
        out_shape=[jax.ShapeDtypeStruct((t, D_MODEL), F32)] + side_shapes,
        compiler_params=_params(("arbitrary", "arbitrary")),
        name="ple",
    )(x1b, wg, p, wp, x, y, rs, g.reshape(1, D_MODEL), *side_ws)


def _rope_tables(seq):
    half = RET_QK_DIM // 2
    inv = ROPE_BASE ** (-jnp.arange(half, dtype=F32) / half)
    ang = jnp.arange(seq).astype(F32)[:, None] * inv[None, :]
    cos, sin = jnp.cos(ang), jnp.sin(ang)
    return jnp.concatenate([cos, cos], axis=-1), jnp.concatenate([-sin, sin], axis=-1)


SIDE_CAST_HOSTS = {"in_proj": ("w_in",), "out_proj": ("w_out", "w_proj_a", "w_proj_b"), "ple": ("w_ple_gate",)}


def kernel(x_prompt, x_sample, p_prompt, p_sample, w_in, ln_pre, ln_post, na_rpb, ret_log_decay_fwd,
           ret_log_decay_bwd, ret_gn_gain, w_proj_a, w_proj_b, w_out, w_ple, w_ple_gate):
    depth = w_in.shape[0]
    stacked = {"w_in": w_in, "w_proj_a": w_proj_a, "w_proj_b": w_proj_b, "w_out": w_out,
               "w_ple": w_ple, "w_ple_gate": w_ple_gate}
    ps = [p_prompt, p_sample]
    shapes = [x.shape[:2] for x in (x_prompt, x_sample)]
    xs = [x.reshape(-1, D_MODEL) for x in (x_prompt, x_sample)]
    rope = {seq: _rope_tables(seq) for _, seq in shapes}
    small = [{"ln_pre": ln_pre[i].astype(F32),
              "ln_post": ln_post[i].astype(F32),
              "na_bias": {seq // GRID_W: _na_bias_table(na_rpb[i], seq // GRID_W) for _, seq in shapes},
              "ret_ld": jnp.stack([ret_log_decay_fwd[i], ret_log_decay_bwd[i]]).astype(F32),
              "ret_gn": ret_gn_gain[i]} for i in range(depth)]
    wts = [None] * depth
    wts[0] = {name: _cast_layer_bf16(w, 0) for name, w in stacked.items()}
    A, B = 0, 1

    def side(host, tr, i, hosting):
        (batch, seq), t = shapes[tr], shapes[tr][0] * shapes[tr][1]
        grid = {"in_proj": _in_proj_grid(t, seq)[1],
                "out_proj": (t // min(MAX_TILE_M, t), D_MODEL // TILE_N),
                "ple": (t // min(MAX_TILE_M, t), D_MODEL // PLE_TILE_N)}[host]
        names = tuple(n for n in SIDE_CAST_HOSTS[host] if hosting and _side_cast_fits(stacked[n], *grid))
        return names, dict(side_ws=tuple(stacked[n] for n in names), side_layer=i + 1)

    def in_stage(tr, i, hosting, post):
        names, kw = side("in_proj", tr, i, hosting)
        xn = _rmsnorm_pre(xs[tr], small[i]["ln_pre"])
        proj, cast, post_out = _in_proj(xn, wts[i]["w_in"], shapes[tr][1], post=post, **kw)
        return proj, dict(zip(names, cast)), post_out

    def mixer_stage(tr, i, proj, hosting):
        batch, seq = shapes[tr]
        names, kw = side("out_proj", tr, i, hosting)
        a = _na_attention(proj, small[i]["na_bias"][seq // GRID_W], batch, seq)
        b = _retention(proj, small[i]["ret_ld"], small[i]["ret_gn"], *rope[seq], batch, seq)
        merged = _merge(a, b, wts[i]["w_proj_a"], wts[i]["w_proj_b"], proj)
        y, *cast = _out_proj(merged, wts[i]["w_out"], **kw)
        return y, dict(zip(names, cast))

    def ple_stage(tr, i, y, x1b, rs, hosting):
        names, kw = side("ple", tr, i, hosting)
        t = xs[tr].shape[0]
        xs[tr], *cast = _ple(xs[tr], y, rs, small[i]["ln_post"], x1b, ps[tr][i].reshape(t, PLE_DIM),
                             wts[i]["w_ple_gate"], wts[i]["w_ple"], **kw)
        return dict(zip(names, cast))

    y_b = None
    for i in range(depth):
        hosting = i + 1 < depth
        post_b = None if y_b is None else (xs[B], y_b, small[i - 1]["ln_post"])
        proj_a, casts, post_out_b = in_stage(A, i, hosting, post_b)
        if y_b is not None:
            ple_stage(B, i - 1, y_b, *post_out_b, False)
        y_a, cast = mixer_stage(A, i, proj_a, hosting)
        casts.update(cast)
        proj_b, _, post_out_a = in_stage(B, i, False, (xs[A], y_a, small[i]["ln_post"]))
        casts.update(ple_stage(A, i, y_a, *post_out_a, hosting))
        y_b, _ = mixer_stage(B, i, proj_b, False)
        if hosting:
            wts[i + 1] = {name: casts[name] if name in casts else _cast_layer_bf16(w, i + 1)
                          for name, w in stacked.items()}
    x1b, rs = _post_norm(xs[B], y_b, small[depth - 1]["ln_post"])
    ple_stage(B, depth - 1, y_b, x1b, rs, False)
    return tuple(x.reshape(batch, seq, D_MODEL) for x, (batch, seq) in zip(xs, shapes))
```

```python
import functools

import numpy as np
import jax
import jax.numpy as jnp
from jax import lax
from jax.experimental import pallas as pl
from jax.experimental.pallas import tpu as pltpu

D_MODEL = 4096
GRID_W = 64
PLE_DIM = 256
NA_HEADS = 16
NA_HEAD_DIM = 128
NA_WIDTH = NA_HEADS * NA_HEAD_DIM
NA_WIN_ROWS = 8
NA_WIN_COLS = 16
RET_HEADS = 8
RET_QK_DIM = 128
RET_V_DIM = 256
RET_QK_WIDTH = RET_HEADS * RET_QK_DIM
RET_V_WIDTH = RET_HEADS * RET_V_DIM
RET_CHUNK = 128
ROPE_BASE = 10000.0
EPS = 1e-6

OFF_NA_Q = 0
OFF_NA_K = OFF_NA_Q + NA_WIDTH
OFF_NA_V = OFF_NA_K + NA_WIDTH
OFF_NA_G = OFF_NA_V + NA_WIDTH
OFF_R_Q = OFF_NA_G + NA_WIDTH
OFF_R_K = OFF_R_Q + RET_QK_WIDTH
OFF_R_V = OFF_R_K + RET_QK_WIDTH
OFF_R_G = OFF_R_V + RET_V_WIDTH
OFF_GATE_A = OFF_R_G + RET_V_WIDTH
OFF_GATE_B = OFF_GATE_A + D_MODEL
IN_WIDTH = OFF_GATE_B + D_MODEL

TILE_N = 1024
MAX_TILE_M = 1024
ROW_TILE = 256
V7X_VMEM_LIMIT_BYTES = 56 * 1024 * 1024
LANES = 128

F32 = jnp.float32
BF16 = jnp.bfloat16


def _params(semantics, vmem=V7X_VMEM_LIMIT_BYTES):
    return pltpu.CompilerParams(dimension_semantics=semantics, vmem_limit_bytes=vmem)


CAST_BLOCK_BYTES = 8 * 1024 * 1024


def _cast_kernel(w_ref, o_ref):
    o_ref[...] = w_ref[...].astype(o_ref.dtype)


def _cast_layer_bf16(w, layer):
    _, k, n = w.shape
    rb = max(r for r in range(16, k + 1, 16) if k % r == 0 and (r * n * 4 <= CAST_BLOCK_BYTES or r == 16))
    return pl.pallas_call(
        _cast_kernel,
        grid=(k // rb,),
        in_specs=[pl.BlockSpec((None, rb, n), lambda i: (layer, i, 0))],
        out_specs=pl.BlockSpec((rb, n), lambda i: (i, 0)),
        out_shape=jax.ShapeDtypeStruct((k, n), BF16),
        compiler_params=_params(("arbitrary",)),
        name="cast_bf16",
    )(w)


SUBLANES_BF16 = 16


SIDE_CAST_MAX_BYTES = 1024 * 1024


def _side_cast_fits(w, ni, nj):
    _, k, n = w.shape
    return (k % ni == 0 and n % nj == 0 and (k // ni) % SUBLANES_BF16 == 0 and (n // nj) % LANES == 0
            and (k // ni) * (n // nj) * 4 <= SIDE_CAST_MAX_BYTES)


def _side_cast_specs(ws, layer, ni, nj):
    in_specs, out_specs, out_shapes = [], [], []
    for w in ws:
        _, k, n = w.shape
        blk = (k // ni, n // nj)
        in_specs.append(pl.BlockSpec((None,) + blk, lambda i, j: (layer, i, j)))
        out_specs.append(pl.BlockSpec(blk, lambda i, j: (i, j)))
        out_shapes.append(jax.ShapeDtypeStruct((k, n), BF16))
    return in_specs, out_specs, out_shapes


def _split_side_refs(refs, n_out):
    n = (len(refs) - n_out) // 2
    return refs[:n], refs[n:n + n_out], refs[n + n_out:]


def _side_cast(src_refs, dst_refs):
    for src, dst in zip(src_refs, dst_refs):
        dst[...] = src[...].astype(dst.dtype)


def _rmsnorm_rows(x_ref, g_ref, o_ref):
    x = x_ref[...]
    ms = jnp.mean(x * x, axis=-1, keepdims=True)
    xn = (x * lax.rsqrt(ms + EPS) * g_ref[...]).astype(o_ref.dtype)
    o_ref[...] = xn
    return xn


def _rmsnorm_kernel(x_ref, g_ref, o_ref):
    _rmsnorm_rows(x_ref, g_ref, o_ref)


def _rmsnorm_pre(x, g):
    t = x.shape[0]
    return pl.pallas_call(
        _rmsnorm_kernel,
        grid=(t // ROW_TILE,),
        in_specs=[pl.BlockSpec((ROW_TILE, D_MODEL), lambda i: (i, 0)),
                  pl.BlockSpec((1, D_MODEL), lambda i: (0, 0))],
        out_specs=pl.BlockSpec((ROW_TILE, D_MODEL), lambda i: (i, 0)),
        out_shape=jax.ShapeDtypeStruct((t, D_MODEL), BF16),
        compiler_params=_params(("arbitrary",)),
        name="rmsnorm_pre",
    )(x, g.reshape(1, D_MODEL))


_J_NA_K = OFF_NA_K // TILE_N
_J_NA_G = OFF_NA_G // TILE_N
_J_R_Q = OFF_R_Q // TILE_N
_J_R_K = OFF_R_K // TILE_N
_J_R_G = OFF_R_G // TILE_N
_J_GATE = OFF_GATE_A // TILE_N
assert NA_HEAD_DIM == RET_QK_DIM
QK_SCALE = NA_HEAD_DIM ** -0.5
LOG2E = float(np.log2(np.e))
NA_Q_SCALE = QK_SCALE * LOG2E


def _post_norm_rows(x_ref, y_ref, g_ref, ob_ref, rs_ref):
    y = y_ref[...].astype(F32)
    rs = lax.rsqrt(jnp.mean(y * y, axis=-1, keepdims=True) + EPS)
    x1b = (x_ref[...] + y * rs * g_ref[...]).astype(ob_ref.dtype)
    ob_ref[...] = x1b
    rs_ref[...] = jnp.broadcast_to(rs, rs_ref.shape)
    return x1b


def _post_kernel(x_ref, y_ref, g_ref, ob_ref, rs_ref):
    _post_norm_rows(x_ref, y_ref, g_ref, ob_ref, rs_ref)


def _hosted_slab_rows(t_post, n_steps):
    return min(r for r in range(SUBLANES_BF16, t_post + 1, SUBLANES_BF16)
               if t_post % r == 0 and t_post // r <= n_steps)


def _inproj_kernel(x_ref, w_ref, *refs, n_cast, has_post):
    n_in = n_cast + (3 if has_post else 0)
    cast_src, post_in = refs[:n_cast], refs[n_cast:n_in]
    o_ref, cast_dst, post_out = refs[n_in], refs[n_in + 1:n_in + 1 + n_cast], refs[n_in + 1 + n_cast:]
    j = pl.program_id(1)
    is_gate = j >= _J_GATE
    is_silu = ((j >= _J_NA_G) & (j < _J_R_Q)) | ((j >= _J_R_G) & (j < _J_GATE))
    scale = jnp.where(j < _J_NA_K, NA_Q_SCALE, jnp.where(j == _J_R_K, QK_SCALE, 1.0))

    def step(epilogue):
        _side_cast(cast_src, cast_dst)
        x = x_ref[...]
        for half in range(2):
            cols = slice(half * (TILE_N // 2), (half + 1) * (TILE_N // 2))
            if half == 1 and has_post:
                x1b = _post_norm_rows(*post_in, *post_out)
                never = pl.program_id(0) < 0
                x = jnp.concatenate([jnp.where(never, x1b, x[:x1b.shape[0]]), x[x1b.shape[0]:]], axis=0)
            y = jnp.dot(x, w_ref[:, cols], preferred_element_type=F32)
            o_ref[:, cols] = epilogue(y).astype(o_ref.dtype)

    pl.when(is_gate)(lambda: step(lambda y: 0.5 * jnp.tanh(0.5 * y) + 0.5))
    pl.when(is_silu)(lambda: step(lambda y: y * (0.5 * jnp.tanh(0.5 * y) + 0.5)))
    pl.when(jnp.logical_not(is_gate | is_silu))(lambda: step(lambda y: y * scale))


def _in_proj_grid(t, seq):
    tm = min(MAX_TILE_M, seq)
    return tm, (t // tm, IN_WIDTH // TILE_N)


def _in_proj(xn, w, seq, side_ws=(), side_layer=0, post=None):
    t = xn.shape[0]
    tm, grid = _in_proj_grid(t, seq)
    tn = TILE_N
    side_in, side_out, side_shapes = _side_cast_specs(side_ws, side_layer, *grid)
    post_args, post_in, post_out, post_shapes = (), [], [], []
    if post is not None:
        px, py, pg = post
        t_post = px.shape[0]
        rows = _hosted_slab_rows(t_post, grid[0] * grid[1])
        slab = lambda i, j: (jnp.minimum(i * grid[1] + j, t_post // rows - 1), 0)
        post_args = (px, py, pg.reshape(1, D_MODEL))
        post_in = [pl.BlockSpec((rows, D_MODEL), slab), pl.BlockSpec((rows, D_MODEL), slab),
                   pl.BlockSpec((1, D_MODEL), lambda i, j: (0, 0))]
        post_out = [pl.BlockSpec((rows, D_MODEL), slab), pl.BlockSpec((rows, LANES), slab)]
        post_shapes = [jax.ShapeDtypeStruct((t_post, D_MODEL), BF16), jax.ShapeDtypeStruct((t_post, LANES), F32)]
    outs = pl.pallas_call(
        functools.partial(_inproj_kernel, n_cast=len(side_ws), has_post=post is not None),
        grid=grid,
        in_specs=[pl.BlockSpec((tm, D_MODEL), lambda i, j: (i, 0)),
                  pl.BlockSpec((D_MODEL, tn), lambda i, j: (0, j))] + side_in + post_in,
        out_specs=[pl.BlockSpec((tm, tn), lambda i, j: (i, j))] + side_out + post_out,
        out_shape=[jax.ShapeDtypeStruct((t, IN_WIDTH), BF16)] + side_shapes + post_shapes,
        compiler_params=_params(("arbitrary", "arbitrary")),
        name="in_proj",
    )(xn, w, *side_ws, *post_args)
    n_cast = len(side_ws)
    return outs[0], list(outs[1:1 + n_cast]), (tuple(outs[1 + n_cast:]) if post is not None else None)


NA_PAIR_ROWS = 2
NA_PAIR_TOKENS = NA_PAIR_ROWS * GRID_W
NA_KEY_ROWS = NA_WIN_ROWS + NA_PAIR_ROWS
NA_KEY_CHUNKS = NA_KEY_ROWS // NA_PAIR_ROWS
NA_KEYS = NA_KEY_ROWS * GRID_W
NA_VARIANTS = 5
NA_PAIR_BLOCK = 32
NA_SUM_ROWS = 16


def _na_kernel(q_ref, k_ref, v_ref, g_ref, bias_ref, o_ref, vt_ref, s_ref, p_ref, *, rows):
    n_pairs = rows // NA_PAIR_ROWS
    n_chunks = n_pairs
    ct = NA_PAIR_TOKENS
    nt = (((1,), (1,)), ((), ()))
    pair_block = s_ref.shape[0]

    def transpose_chunk(c, carry):
        r0 = pl.multiple_of(c * ct, ct)
        vt_ref[c] = v_ref[pl.ds(r0, ct), :].astype(F32).T.astype(BF16)
        return carry

    lax.fori_loop(0, n_chunks, transpose_chunk, 0, unroll=pair_block)
    ones_rows = jnp.ones((NA_SUM_ROWS, NA_KEYS), BF16)

    def block(blk, carry):
        pairs = []
        for u in range(pair_block):
            pr = blk * pair_block + u
            c0 = jnp.clip(pr - NA_WIN_ROWS // 4, 0, n_chunks - NA_KEY_CHUNKS)
            var = jnp.where(pr < 2, pr, jnp.where(pr >= n_pairs - 2, pr - (n_pairs - 2) + 3, 2))
            q0 = pl.multiple_of(pr * ct, ct)
            k0 = pl.multiple_of(c0 * ct, ct)
            pairs.append((c0, q0))
            s = lax.dot_general(k_ref[pl.ds(k0, NA_KEYS), :], q_ref[pl.ds(q0, ct), :], nt,
                                preferred_element_type=F32)
            s_ref[u] = s + bias_ref[var]
        for u in range(pair_block):
            s = s_ref[u]
            m = jnp.max(s, axis=0, keepdims=True)
            p_ref[u] = jnp.exp2(s - m).astype(BF16)
        for u in range(pair_block):
            c0, q0 = pairs[u]
            vt = jnp.concatenate([vt_ref[c0 + c] for c in range(NA_KEY_CHUNKS)], axis=1)
            ot = jnp.dot(jnp.concatenate([vt, ones_rows], axis=0), p_ref[u], preferred_element_type=F32)
            ot = ot[:NA_HEAD_DIM] * (1.0 / ot[NA_HEAD_DIM:NA_HEAD_DIM + 1])
            o = ot.T * g_ref[pl.ds(q0, ct), :].astype(F32)
            o_ref[pl.ds(q0, ct), :] = o.astype(o_ref.dtype)
        return carry

    lax.fori_loop(0, n_pairs // pair_block, block, 0)


def _na_bias_table(rpb, rows):
    n_pairs = rows // NA_PAIR_ROWS
    cols = np.arange(GRID_W)
    col_start = np.clip(cols - NA_WIN_COLS // 2, 0, GRID_W - NA_WIN_COLS)
    col_valid = (cols[None, :] >= col_start[:, None]) & (cols[None, :] < col_start[:, None] + NA_WIN_COLS)
    pad = GRID_W - NA_WIN_COLS
    rp = jnp.pad(rpb.astype(F32), ((0, 0), (0, 0), (pad, pad)))
    toep = jnp.stack([rp[:, :, GRID_W - 1 - qc: 2 * GRID_W - 1 - qc] for qc in range(GRID_W)], axis=2)
    toep = jnp.where(col_valid[None, None], toep, -jnp.inf)
    toep_t = toep.transpose(0, 1, 3, 2)
    masked = jnp.full((NA_HEADS, GRID_W, GRID_W), -jnp.inf, F32)
    tables = []
    for pr in (0, 1, 2, n_pairs - 2, n_pairs - 1):
        base = int(np.clip(NA_PAIR_ROWS * pr - NA_WIN_ROWS // 2, 0, rows - NA_KEY_ROWS))
        slabs = []
        for w in range(NA_KEY_ROWS):
            per_row = []
            for i in range(NA_PAIR_ROWS):
                r_q = NA_PAIR_ROWS * pr + i
                rs = int(np.clip(r_q - NA_WIN_ROWS // 2, 0, rows - NA_WIN_ROWS))
                kr = base + w
                if rs <= kr < rs + NA_WIN_ROWS:
                    per_row.append(toep_t[:, kr - r_q + NA_WIN_ROWS - 1])
                else:
                    per_row.append(masked)
            slabs.append(jnp.concatenate(per_row, axis=-1))
        tables.append(jnp.concatenate(slabs, axis=1))
    return jnp.stack(tables, axis=1) * LOG2E


def _na_attention(proj, bias, batch, seq):
    rows = seq // GRID_W
    n_pairs = rows // NA_PAIR_ROWS
    pair_block = min(NA_PAIR_BLOCK, n_pairs)
    assert rows % NA_PAIR_ROWS == 0 and rows >= NA_KEY_ROWS + 2 and n_pairs % pair_block == 0
    hd = NA_HEAD_DIM
    blk = lambda off: pl.BlockSpec((seq, hd), lambda b, h, off=off: (b, off // hd + h))
    return pl.pallas_call(
        functools.partial(_na_kernel, rows=rows),
        grid=(batch, NA_HEADS),
        in_specs=[blk(OFF_NA_Q), blk(OFF_NA_K), blk(OFF_NA_V), blk(OFF_NA_G),
                  pl.BlockSpec((None, NA_VARIANTS, NA_KEYS, NA_PAIR_TOKENS), lambda b, h: (h, 0, 0, 0))],
        out_specs=pl.BlockSpec((seq, hd), lambda b, h: (b, h)),
        out_shape=jax.ShapeDtypeStruct((batch * seq, NA_WIDTH), BF16),
        scratch_shapes=[pltpu.VMEM((n_pairs, hd, NA_PAIR_TOKENS), BF16),
                        pltpu.VMEM((pair_block, NA_KEYS, NA_PAIR_TOKENS), F32),
                        pltpu.VMEM((pair_block, NA_KEYS, NA_PAIR_TOKENS), BF16)],
        compiler_params=_params(("arbitrary", "arbitrary")),
        name="na_attention",
    )(proj, proj, proj, proj, bias)


RET_UNROLL = 32


def _ret_kernel(ld_ref, q_ref, k_ref, v_ref, g_ref, gain_ref, cos_ref, sin_ref, o_ref,
                qr_ref, kr_ref, kv_ref, st_ref, dsum_ref, qd_ref, kd_ref, *, n_chunks):
    c_len, dk = RET_CHUNK, RET_QK_DIM
    unroll = min(RET_UNROLL, n_chunks)
    h = pl.program_id(1)
    ldf = -jnp.abs(ld_ref[0, h])
    ldb = -jnp.abs(ld_ref[1, h])
    row = lax.broadcasted_iota(jnp.int32, (c_len, c_len), 0).astype(F32)
    col = lax.broadcasted_iota(jnp.int32, (c_len, c_len), 1).astype(F32)
    diff = row - col
    dsum_ref[...] = (jnp.where(diff >= 0, jnp.exp(ldf * jnp.maximum(diff, 0.0)), 0.0)
                     + jnp.where(diff <= 0, jnp.exp(ldb * jnp.maximum(-diff, 0.0)), 0.0))
    qd_ref[:, :dk] = jnp.exp(ldf * (row + 1.0))
    qd_ref[:, dk:] = jnp.exp(ldb * (c_len - row))
    kd_ref[:, :dk] = jnp.exp(ldf * (c_len - 1.0 - row))
    kd_ref[:, dk:] = jnp.exp(ldb * row)
    zero_row = jnp.zeros((1, RET_V_DIM), F32)
    cdf = jnp.exp(zero_row + ldf * c_len)
    cdb = jnp.exp(zero_row + ldb * c_len)
    chunk = lambda c: pl.ds(pl.multiple_of(c * c_len, c_len), c_len)

    def rotate(c, carry):
        rows = chunk(c)
        cos, sin = cos_ref[rows, :], sin_ref[rows, :]
        for src, dst in ((q_ref, qr_ref), (k_ref, kr_ref)):
            t = src[rows, :].astype(F32)
            dst[rows, :] = t * cos + pltpu.roll(t, dk // 2, axis=1) * sin
        return carry

    lax.fori_loop(0, n_chunks, rotate, 0, unroll=unroll)

    def chunk_kv(c, carry):
        rows = chunk(c)
        k = kr_ref[rows, :]
        k2 = (jnp.concatenate([k, k], axis=1) * kd_ref[...]).astype(BF16)
        kv_ref[c] = lax.dot_general(k2, v_ref[rows, :], (((0,), (0,)), ((), ())),
                                    preferred_element_type=F32)
        return carry

    lax.fori_loop(0, n_chunks, chunk_kv, 0, unroll=unroll)

    def scan_fwd(c, s):
        st_ref[c, :dk, :] = s.astype(BF16)
        return s * cdf + kv_ref[c, :dk, :]

    def scan_bwd(t, s):
        c = n_chunks - 1 - t
        st_ref[c, dk:, :] = s.astype(BF16)
        return s * cdb + kv_ref[c, dk:, :]

    zero_state = jnp.zeros((dk, RET_V_DIM), F32)
    lax.fori_loop(0, n_chunks, scan_fwd, zero_state, unroll=unroll)
    lax.fori_loop(0, n_chunks, scan_bwd, zero_state, unroll=unroll)

    def chunk_out(c, carry):
        rows = chunk(c)
        q = qr_ref[rows, :]
        v = v_ref[rows, :]
        sc = lax.dot_general(q.astype(BF16), kr_ref[rows, :].astype(BF16), (((1,), (1,)), ((), ())),
                             preferred_element_type=F32)
        o = jnp.dot((sc * dsum_ref[...]).astype(BF16), v, preferred_element_type=F32)
        q2 = (jnp.concatenate([q, q], axis=1) * qd_ref[...]).astype(BF16)
        o = o + jnp.dot(q2, st_ref[c], preferred_element_type=F32)
        o = o * lax.rsqrt(jnp.mean(o * o, axis=-1, keepdims=True) + EPS)
        o = o * gain_ref[...] * g_ref[rows, :].astype(F32)
        o_ref[rows, :] = o.astype(o_ref.dtype)
        return carry

    lax.fori_loop(0, n_chunks, chunk_out, 0, unroll=unroll)


def _retention(proj, ld, gain, cos_t, sin_t, batch, seq):
    n_chunks = seq // RET_CHUNK
    dk, dv, c_len = RET_QK_DIM, RET_V_DIM, RET_CHUNK
    assert n_chunks % min(RET_UNROLL, n_chunks) == 0
    qk_blk = lambda off: pl.BlockSpec((seq, dk), lambda b, h, off=off: (b, off // dk + h))
    v_blk = lambda off: pl.BlockSpec((seq, dv), lambda b, h, off=off: (b, off // dv + h))
    rope_blk = pl.BlockSpec((seq, dk), lambda b, h: (0, 0))
    return pl.pallas_call(
        functools.partial(_ret_kernel, n_chunks=n_chunks),
        grid=(batch, RET_HEADS),
        in_specs=[pl.BlockSpec(memory_space=pltpu.SMEM),
                  qk_blk(OFF_R_Q), qk_blk(OFF_R_K), v_blk(OFF_R_V), v_blk(OFF_R_G),
                  pl.BlockSpec((None, 1, dv), lambda b, h: (h, 0, 0)), rope_blk, rope_blk],
        out_specs=pl.BlockSpec((seq, dv), lambda b, h: (b, h)),
        out_shape=jax.ShapeDtypeStruct((batch * seq, RET_V_WIDTH), BF16),
        scratch_shapes=[pltpu.VMEM((seq, dk), F32),
                        pltpu.VMEM((seq, dk), F32),
                        pltpu.VMEM((n_chunks, 2 * dk, dv), F32),
                        pltpu.VMEM((n_chunks, 2 * dk, dv), BF16),
                        pltpu.VMEM((c_len, c_len), F32),
                        pltpu.VMEM((c_len, 2 * dk), F32),
                        pltpu.VMEM((c_len, 2 * dk), F32)],
        compiler_params=_params(("arbitrary", "arbitrary")),
        name="retention",
    )(ld, proj, proj, proj, proj, gain.astype(F32).reshape(RET_HEADS, 1, dv), cos_t, sin_t)


def _merge_kernel(a_ref, b_ref, wa_ref, wb_ref, ga_ref, gb_ref, o_ref):
    ya = jnp.dot(a_ref[...], wa_ref[...], preferred_element_type=F32).astype(BF16)
    yb = jnp.dot(b_ref[...], wb_ref[...], preferred_element_type=F32).astype(BF16)
    o_ref[...] = ga_ref[...] * ya + gb_ref[...] * yb


def _merge(a, b, wa, wb, proj):
    t = a.shape[0]
    tm = min(MAX_TILE_M, t)
    ja, jb = OFF_GATE_A // TILE_N, OFF_GATE_B // TILE_N
    return pl.pallas_call(
        _merge_kernel,
        grid=(t // tm, D_MODEL // TILE_N),
        in_specs=[pl.BlockSpec((tm, NA_WIDTH), lambda i, j: (i, 0)),
                  pl.BlockSpec((tm, RET_V_WIDTH), lambda i, j: (i, 0)),
                  pl.BlockSpec((NA_WIDTH, TILE_N), lambda i, j: (0, j)),
                  pl.BlockSpec((RET_V_WIDTH, TILE_N), lambda i, j: (0, j)),
                  pl.BlockSpec((tm, TILE_N), lambda i, j: (i, ja + j)),
                  pl.BlockSpec((tm, TILE_N), lambda i, j: (i, jb + j))],
        out_specs=pl.BlockSpec((tm, TILE_N), lambda i, j: (i, j)),
        out_shape=jax.ShapeDtypeStruct((t, D_MODEL), BF16),
        compiler_params=_params(("arbitrary", "arbitrary")),
        name="merge",
    )(a, b, wa, wb, proj, proj)


def _outproj_kernel(x_ref, w_ref, *refs, n_cast, has_pre):
    n_in = n_cast + (2 if has_pre else 0)
    cast_src, pre_in = refs[:n_cast], refs[n_cast:n_in]
    o_ref, cast_dst, pre_out = refs[n_in], refs[n_in + 1:n_in + 1 + n_cast], refs[n_in + 1 + n_cast:]
    _side_cast(cast_src, cast_dst)
    x = x_ref[...]
    for half in range(2):
        cols = slice(half * (TILE_N // 2), (half + 1) * (TILE_N // 2))
        if half == 1 and has_pre:
            xn = _rmsnorm_rows(*pre_in, *pre_out)
            never = pl.program_id(0) < 0
            x = jnp.concatenate([jnp.where(never, xn, x[:xn.shape[0]]), x[xn.shape[0]:]], axis=0)
        o_ref[:, cols] = jnp.dot(x, w_ref[:, cols], preferred_element_type=F32).astype(o_ref.dtype)


def _out_proj(merged, w, side_ws=(), side_layer=0, pre=None):
    t = merged.shape[0]
    tm = min(MAX_TILE_M, t)
    grid = (t // tm, D_MODEL // TILE_N)
    side_in, side_out, side_shapes = _side_cast_specs(side_ws, side_layer, *grid)
    pre_args, pre_in, pre_out, pre_shapes = (), [], [], []
    if pre is not None:
        px, pg = pre
        t_pre = px.shape[0]
        rows = _hosted_slab_rows(t_pre, grid[0] * grid[1])
        slab = lambda i, j: (jnp.minimum(i * grid[1] + j, t_pre // rows - 1), 0)
        pre_args = (px, pg.reshape(1, D_MODEL))
        pre_in = [pl.BlockSpec((rows, D_MODEL), slab), pl.BlockSpec((1, D_MODEL), lambda i, j: (0, 0))]
        pre_out = [pl.BlockSpec((rows, D_MODEL), slab)]
        pre_shapes = [jax.ShapeDtypeStruct((t_pre, D_MODEL), BF16)]
    outs = pl.pallas_call(
        functools.partial(_outproj_kernel, n_cast=len(side_ws), has_pre=pre is not None),
        grid=grid,
        in_specs=[pl.BlockSpec((tm, D_MODEL), lambda i, j: (i, 0)),
                  pl.BlockSpec((D_MODEL, TILE_N), lambda i, j: (0, j))] + side_in + pre_in,
        out_specs=[pl.BlockSpec((tm, TILE_N), lambda i, j: (i, j))] + side_out + pre_out,
        out_shape=[jax.ShapeDtypeStruct((t, D_MODEL), BF16)] + side_shapes + pre_shapes,
        compiler_params=_params(("arbitrary", "arbitrary")),
        name="out_proj",
    )(merged, w, *side_ws, *pre_args)
    n_cast = len(side_ws)
    return outs[0], list(outs[1:1 + n_cast]), (outs[1 + n_cast] if pre is not None else None)


def _post_norm(x, y, g):
    t = x.shape[0]
    row = pl.BlockSpec((ROW_TILE, D_MODEL), lambda i: (i, 0))
    return pl.pallas_call(
        _post_kernel,
        grid=(t // ROW_TILE,),
        in_specs=[row, row, pl.BlockSpec((1, D_MODEL), lambda i: (0, 0))],
        out_specs=[row, pl.BlockSpec((ROW_TILE, LANES), lambda i: (i, 0))],
        out_shape=[jax.ShapeDtypeStruct((t, D_MODEL), BF16), jax.ShapeDtypeStruct((t, LANES), F32)],
        compiler_params=_params(("arbitrary",)),
        name="post_norm",
    )(x, y, g.reshape(1, D_MODEL))


PLE_TILE_N = 512


def _ple_kernel(xb_ref, wg_ref, p_ref, wp_ref, x_ref, y_ref, rs_ref, g_ref, *refs):
    cast_src, (o_ref,), cast_dst = _split_side_refs(refs, 1)
    _side_cast(cast_src, cast_dst)
    z = jnp.dot(xb_ref[...], wg_ref[...], preferred_element_type=F32)
    gate = 0.5 * jnp.tanh(0.5 * z) + 0.5
    upd = gate * jnp.dot(p_ref[...].astype(BF16), wp_ref[...], preferred_element_type=F32)
    rs = rs_ref[...]
    for c in range(PLE_TILE_N // LANES):
        sl = slice(c * LANES, (c + 1) * LANES)
        x1 = x_ref[:, sl] + y_ref[:, sl].astype(F32) * rs * g_ref[:, sl]
        o_ref[:, sl] = x1 + upd[:, sl]


def _ple(x, y, rs, g, x1b, p, wg, wp, side_ws=(), side_layer=0):
    t = x.shape[0]
    tm, tn = min(MAX_TILE_M, t), PLE_TILE_N
    grid = (t // tm, D_MODEL // tn)
    tile = pl.BlockSpec((tm, tn), lambda i, j: (i, j))
    side_in, side_out, side_shapes = _side_cast_specs(side_ws, side_layer, *grid)
    return pl.pallas_call(
        _ple_kernel,
        grid=grid,
        in_specs=[pl.BlockSpec((tm, D_MODEL), lambda i, j: (i, 0)),
                  pl.BlockSpec((D_MODEL, tn), lambda i, j: (0, j)),
                  pl.BlockSpec((tm, PLE_DIM), lambda i, j: (i, 0)),
                  pl.BlockSpec((PLE_DIM, tn), lambda i, j: (0, j)),
                  tile, tile,
                  pl.BlockSpec((tm, LANES), lambda i, j: (i, 0)),
                  pl.BlockSpec((1, tn), lambda i, j: (0, j))] + side_in,
        out_specs=[tile] + side_out,
        out_shape=[jax.ShapeDtypeStruct((t, D_MODEL), F32)] + side_shapes,
        compiler_params=_params(("arbitrary", "arbitrary")),
        name="ple",
    )(x1b, wg, p, wp, x, y, rs, g.reshape(1, D_MODEL), *side_ws)


def _rope_tables(seq):
    half = RET_QK_DIM // 2
    inv = ROPE_BASE ** (-jnp.arange(half, dtype=F32) / half)
    ang = jnp.arange(seq).astype(F32)[:, None] * inv[None, :]
    cos, sin = jnp.cos(ang), jnp.sin(ang)
    return jnp.concatenate([cos, cos], axis=-1), jnp.concatenate([-sin, sin], axis=-1)


SIDE_CAST_HOSTS = {"in_proj": ("w_in",), "out_proj": ("w_out",), "ple": ("w_ple_gate", "w_proj_a", "w_proj_b")}


def kernel(x_prompt, x_sample, p_prompt, p_sample, w_in, ln_pre, ln_post, na_rpb, ret_log_decay_fwd,
           ret_log_decay_bwd, ret_gn_gain, w_proj_a, w_proj_b, w_out, w_ple, w_ple_gate):
    depth = w_in.shape[0]
    stacked = {"w_in": w_in, "w_proj_a": w_proj_a, "w_proj_b": w_proj_b, "w_out": w_out,
               "w_ple": w_ple, "w_ple_gate": w_ple_gate}
    ps = [p_prompt, p_sample]
    shapes = [x.shape[:2] for x in (x_prompt, x_sample)]
    xs = [x.reshape(-1, D_MODEL) for x in (x_prompt, x_sample)]
    rope = {seq: _rope_tables(seq) for _, seq in shapes}
    small = [{"ln_pre": ln_pre[i].astype(F32),
              "ln_post": ln_post[i].astype(F32),
              "na_bias": {seq // GRID_W: _na_bias_table(na_rpb[i], seq // GRID_W) for _, seq in shapes},
              "ret_ld": jnp.stack([ret_log_decay_fwd[i], ret_log_decay_bwd[i]]).astype(F32),
              "ret_gn": ret_gn_gain[i]} for i in range(depth)]
    wts = [None] * depth
    wts[0] = {name: _cast_layer_bf16(w, 0) for name, w in stacked.items()}
    A, B = 0, 1

    def side(host, tr, i, hosting):
        (batch, seq), t = shapes[tr], shapes[tr][0] * shapes[tr][1]
        grid = {"in_proj": _in_proj_grid(t, seq)[1],
                "out_proj": (t // min(MAX_TILE_M, t), D_MODEL // TILE_N),
                "ple": (t // min(MAX_TILE_M, t), D_MODEL // PLE_TILE_N)}[host]
        names = tuple(n for n in SIDE_CAST_HOSTS[host] if hosting and _side_cast_fits(stacked[n], *grid))
        return names, dict(side_ws=tuple(stacked[n] for n in names), side_layer=i + 1)

    def in_stage(tr, i, hosting, post, xn=None):
        names, kw = side("in_proj", tr, i, hosting)
        if xn is None:
            xn = _rmsnorm_pre(xs[tr], small[i]["ln_pre"])
        proj, cast, post_out = _in_proj(xn, wts[i]["w_in"], shapes[tr][1], post=post, **kw)
        return proj, dict(zip(names, cast)), post_out

    def mixer_stage(tr, i, proj, hosting, pre=None):
        batch, seq = shapes[tr]
        names, kw = side("out_proj", tr, i, hosting)
        a = _na_attention(proj, small[i]["na_bias"][seq // GRID_W], batch, seq)
        b = _retention(proj, small[i]["ret_ld"], small[i]["ret_gn"], *rope[seq], batch, seq)
        merged = _merge(a, b, wts[i]["w_proj_a"], wts[i]["w_proj_b"], proj)
        y, cast, xn_other = _out_proj(merged, wts[i]["w_out"], pre=pre, **kw)
        return y, dict(zip(names, cast)), xn_other

    def ple_stage(tr, i, y, x1b, rs, hosting):
        names, kw = side("ple", tr, i, hosting)
        t = xs[tr].shape[0]
        xs[tr], *cast = _ple(xs[tr], y, rs, small[i]["ln_post"], x1b, ps[tr][i].reshape(t, PLE_DIM),
                             wts[i]["w_ple_gate"], wts[i]["w_ple"], **kw)
        return dict(zip(names, cast))

    y_b = None
    for i in range(depth):
        hosting = i + 1 < depth
        post_b = None if y_b is None else (xs[B], y_b, small[i - 1]["ln_post"])
        proj_a, casts, post_out_b = in_stage(A, i, hosting, post_b)
        if y_b is not None:
            ple_stage(B, i - 1, y_b, *post_out_b, False)
        y_a, cast, xn_b = mixer_stage(A, i, proj_a, hosting, pre=(xs[B], small[i]["ln_pre"]))
        casts.update(cast)
        proj_b, _, post_out_a = in_stage(B, i, False, (xs[A], y_a, small[i]["ln_post"]), xn=xn_b)
        casts.update(ple_stage(A, i, y_a, *post_out_a, hosting))
        y_b, _, _ = mixer_stage(B, i, proj_b, False)
        if hosting:
            wts[i + 1] = {name: casts[name] if name in casts else _cast_layer_bf16(w, i + 1)
                          for name, w in stacked.items()}
    x1b, rs = _post_norm(xs[B], y_b, small[depth - 1]["ln_post"])
    ple_stage(B, depth - 1, y_b, x1b, rs, False)
    return tuple(x.reshape(batch, seq, D_MODEL) for x, (batch, seq) in zip(xs, shapes))
```

```python
import functools

import numpy as np
import jax
import jax.numpy as jnp
from jax import lax
from jax.experimental import pallas as pl
from jax.experimental.pallas import tpu as pltpu

D_MODEL = 4096
GRID_W = 64
PLE_DIM = 256
NA_HEADS = 16
NA_HEAD_DIM = 128
NA_WIDTH = NA_HEADS * NA_HEAD_DIM
NA_WIN_ROWS = 8
NA_WIN_COLS = 16
RET_HEADS = 8
RET_QK_DIM = 128
RET_V_DIM = 256
RET_QK_WIDTH = RET_HEADS * RET_QK_DIM
RET_V_WIDTH = RET_HEADS * RET_V_DIM
RET_CHUNK = 128
ROPE_BASE = 10000.0
EPS = 1e-6

OFF_NA_Q = 0
OFF_NA_K = OFF_NA_Q + NA_WIDTH
OFF_NA_V = OFF_NA_K + NA_WIDTH
OFF_NA_G = OFF_NA_V + NA_WIDTH
OFF_R_Q = OFF_NA_G + NA_WIDTH
OFF_R_K = OFF_R_Q + RET_QK_WIDTH
OFF_R_V = OFF_R_K + RET_QK_WIDTH
OFF_R_G = OFF_R_V + RET_V_WIDTH
OFF_GATE_A = OFF_R_G + RET_V_WIDTH
OFF_GATE_B = OFF_GATE_A + D_MODEL
IN_WIDTH = OFF_GATE_B + D_MODEL

TILE_N = 1024
MAX_TILE_M = 1024
ROW_TILE = 256
V7X_VMEM_LIMIT_BYTES = 56 * 1024 * 1024
LANES = 128

F32 = jnp.float32
BF16 = jnp.bfloat16


def _params(semantics, vmem=V7X_VMEM_LIMIT_BYTES):
    return pltpu.CompilerParams(dimension_semantics=semantics, vmem_limit_bytes=vmem)


CAST_BLOCK_BYTES = 8 * 1024 * 1024


def _cast_kernel(w_ref, o_ref):
    o_ref[...] = w_ref[...].astype(o_ref.dtype)


def _cast_layer_bf16(w, layer):
    _, k, n = w.shape
    rb = max(r for r in range(16, k + 1, 16) if k % r == 0 and (r * n * 4 <= CAST_BLOCK_BYTES or r == 16))
    return pl.pallas_call(
        _cast_kernel,
        grid=(k // rb,),
        in_specs=[pl.BlockSpec((None, rb, n), lambda i: (layer, i, 0))],
        out_specs=pl.BlockSpec((rb, n), lambda i: (i, 0)),
        out_shape=jax.ShapeDtypeStruct((k, n), BF16),
        compiler_params=_params(("arbitrary",)),
        name="cast_bf16",
    )(w)


SUBLANES_BF16 = 16


SIDE_CAST_MAX_BYTES = 1024 * 1024


def _side_cast_fits(w, ni, nj):
    _, k, n = w.shape
    return (k % ni == 0 and n % nj == 0 and (k // ni) % SUBLANES_BF16 == 0 and (n // nj) % LANES == 0
            and (k // ni) * (n // nj) * 4 <= SIDE_CAST_MAX_BYTES)


def _side_cast_specs(ws, layer, ni, nj):
    in_specs, out_specs, out_shapes = [], [], []
    for w in ws:
        _, k, n = w.shape
        blk = (k // ni, n // nj)
        in_specs.append(pl.BlockSpec((None,) + blk, lambda i, j: (layer, i, j)))
        out_specs.append(pl.BlockSpec(blk, lambda i, j: (i, j)))
        out_shapes.append(jax.ShapeDtypeStruct((k, n), BF16))
    return in_specs, out_specs, out_shapes


def _split_side_refs(refs, n_out):
    n = (len(refs) - n_out) // 2
    return refs[:n], refs[n:n + n_out], refs[n + n_out:]


def _side_cast(src_refs, dst_refs):
    for src, dst in zip(src_refs, dst_refs):
        dst[...] = src[...].astype(dst.dtype)


def _rmsnorm_rows(x_ref, g_ref, o_ref):
    x = x_ref[...]
    ms = jnp.mean(x * x, axis=-1, keepdims=True)
    xn = (x * lax.rsqrt(ms + EPS) * g_ref[...]).astype(o_ref.dtype)
    o_ref[...] = xn
    return xn


def _rmsnorm_kernel(x_ref, g_ref, o_ref):
    _rmsnorm_rows(x_ref, g_ref, o_ref)


def _rmsnorm_pre(x, g):
    t = x.shape[0]
    return pl.pallas_call(
        _rmsnorm_kernel,
        grid=(t // ROW_TILE,),
        in_specs=[pl.BlockSpec((ROW_TILE, D_MODEL), lambda i: (i, 0)),
                  pl.BlockSpec((1, D_MODEL), lambda i: (0, 0))],
        out_specs=pl.BlockSpec((ROW_TILE, D_MODEL), lambda i: (i, 0)),
        out_shape=jax.ShapeDtypeStruct((t, D_MODEL), BF16),
        compiler_params=_params(("arbitrary",)),
        name="rmsnorm_pre",
    )(x, g.reshape(1, D_MODEL))


_J_NA_K = OFF_NA_K // TILE_N
_J_NA_G = OFF_NA_G // TILE_N
_J_R_Q = OFF_R_Q // TILE_N
_J_R_K = OFF_R_K // TILE_N
_J_R_G = OFF_R_G // TILE_N
_J_GATE = OFF_GATE_A // TILE_N
assert NA_HEAD_DIM == RET_QK_DIM
QK_SCALE = NA_HEAD_DIM ** -0.5
LOG2E = float(np.log2(np.e))
NA_Q_SCALE = QK_SCALE * LOG2E


def _post_norm_rows(x_ref, y_ref, g_ref, ob_ref, rs_ref):
    y = y_ref[...].astype(F32)
    rs = lax.rsqrt(jnp.mean(y * y, axis=-1, keepdims=True) + EPS)
    x1b = (x_ref[...] + y * rs * g_ref[...]).astype(ob_ref.dtype)
    ob_ref[...] = x1b
    rs_ref[...] = jnp.broadcast_to(rs, rs_ref.shape)
    return x1b


def _post_kernel(x_ref, y_ref, g_ref, ob_ref, rs_ref):
    _post_norm_rows(x_ref, y_ref, g_ref, ob_ref, rs_ref)


def _hosted_slab_rows(t_post, n_steps):
    return min(r for r in range(SUBLANES_BF16, t_post + 1, SUBLANES_BF16)
               if t_post % r == 0 and t_post // r <= n_steps)


def _inproj_kernel(x_ref, w_ref, *refs, n_cast, has_post):
    n_in = n_cast + (3 if has_post else 0)
    cast_src, post_in = refs[:n_cast], refs[n_cast:n_in]
    o_ref, cast_dst, post_out = refs[n_in], refs[n_in + 1:n_in + 1 + n_cast], refs[n_in + 1 + n_cast:]
    j = pl.program_id(1)
    is_gate = j >= _J_GATE
    is_silu = ((j >= _J_NA_G) & (j < _J_R_Q)) | ((j >= _J_R_G) & (j < _J_GATE))
    scale = jnp.where(j < _J_NA_K, NA_Q_SCALE, jnp.where(j == _J_R_K, QK_SCALE, 1.0))

    def step(epilogue):
        _side_cast(cast_src, cast_dst)
        x = x_ref[...]
        for half in range(2):
            cols = slice(half * (TILE_N // 2), (half + 1) * (TILE_N // 2))
            if half == 1 and has_post:
                x1b = _post_norm_rows(*post_in, *post_out)
                never = pl.program_id(0) < 0
                x = jnp.concatenate([jnp.where(never, x1b, x[:x1b.shape[0]]), x[x1b.shape[0]:]], axis=0)
            y = jnp.dot(x, w_ref[:, cols], preferred_element_type=F32)
            o_ref[:, cols] = epilogue(y).astype(o_ref.dtype)

    pl.when(is_gate)(lambda: step(lambda y: 0.5 * jnp.tanh(0.5 * y) + 0.5))
    pl.when(is_silu)(lambda: step(lambda y: y * (0.5 * jnp.tanh(0.5 * y) + 0.5)))
    pl.when(jnp.logical_not(is_gate | is_silu))(lambda: step(lambda y: y * scale))


def _in_proj_grid(t, seq):
    tm = min(MAX_TILE_M, seq)
    return tm, (t // tm, IN_WIDTH // TILE_N)


def _in_proj(xn, w, seq, side_ws=(), side_layer=0, post=None):
    t = xn.shape[0]
    tm, grid = _in_proj_grid(t, seq)
    tn = TILE_N
    side_in, side_out, side_shapes = _side_cast_specs(side_ws, side_layer, *grid)
    post_args, post_in, post_out, post_shapes = (), [], [], []
    if post is not None:
        px, py, pg = post
        t_post = px.shape[0]
        rows = _hosted_slab_rows(t_post, grid[0] * grid[1])
        slab = lambda i, j: (jnp.minimum(i * grid[1] + j, t_post // rows - 1), 0)
        post_args = (px, py, pg.reshape(1, D_MODEL))
        post_in = [pl.BlockSpec((rows, D_MODEL), slab), pl.BlockSpec((rows, D_MODEL), slab),
                   pl.BlockSpec((1, D_MODEL), lambda i, j: (0, 0))]
        post_out = [pl.BlockSpec((rows, D_MODEL), slab), pl.BlockSpec((rows, LANES), slab)]
        post_shapes = [jax.ShapeDtypeStruct((t_post, D_MODEL), BF16), jax.ShapeDtypeStruct((t_post, LANES), F32)]
    outs = pl.pallas_call(
        functools.partial(_inproj_kernel, n_cast=len(side_ws), has_post=post is not None),
        grid=grid,
        in_specs=[pl.BlockSpec((tm, D_MODEL), lambda i, j: (i, 0)),
                  pl.BlockSpec((D_MODEL, tn), lambda i, j: (0, j))] + side_in + post_in,
        out_specs=[pl.BlockSpec((tm, tn), lambda i, j: (i, j))] + side_out + post_out,
        out_shape=[jax.ShapeDtypeStruct((t, IN_WIDTH), BF16)] + side_shapes + post_shapes,
        compiler_params=_params(("arbitrary", "arbitrary")),
        name="in_proj",
    )(xn, w, *side_ws, *post_args)
    n_cast = len(side_ws)
    return outs[0], list(outs[1:1 + n_cast]), (tuple(outs[1 + n_cast:]) if post is not None else None)


NA_PAIR_ROWS = 2
NA_PAIR_TOKENS = NA_PAIR_ROWS * GRID_W
NA_KEY_ROWS = NA_WIN_ROWS + NA_PAIR_ROWS
NA_KEY_CHUNKS = NA_KEY_ROWS // NA_PAIR_ROWS
NA_KEYS = NA_KEY_ROWS * GRID_W
NA_VARIANTS = 5
NA_QUAD_CHUNKS = NA_KEY_CHUNKS + 1
NA_SUM_ROWS = 16


def _na_kernel(q_ref, k_ref, v_ref, g_ref, bias_ref, o_ref, vt_ref, s_ref, p_ref, *, rows):
    n_pairs = rows // NA_PAIR_ROWS
    n_chunks = n_pairs
    ct = NA_PAIR_TOKENS
    nt = (((1,), (1,)), ((), ()))
    n_quads = n_pairs // 2

    for c in range(n_chunks):
        vt_ref[c] = v_ref[c * ct:(c + 1) * ct, :].astype(F32).T.astype(BF16)
    ones_rows = jnp.ones((NA_SUM_ROWS, NA_QUAD_CHUNKS * ct), BF16)
    zero_block = jnp.zeros((ct, ct), BF16)

    def pair_window(pr):
        c0 = min(max(pr - NA_WIN_ROWS // 4, 0), n_chunks - NA_KEY_CHUNKS)
        var = pr if pr < 2 else (pr - (n_pairs - 2) + 3 if pr >= n_pairs - 2 else 2)
        return c0, var

    quad_chunk0 = [min(max(2 * qd - NA_WIN_ROWS // 4, 0), n_chunks - NA_QUAD_CHUNKS) for qd in range(n_quads)]
    for qd in range(n_quads):
        k0, q0 = quad_chunk0[qd] * ct, 2 * qd * ct
        s_ref[qd] = lax.dot_general(k_ref[k0:k0 + NA_QUAD_CHUNKS * ct, :], q_ref[q0:q0 + 2 * ct, :], nt,
                                    preferred_element_type=F32)
    for qd in range(n_quads):
        for half in range(2):
            c0, var = pair_window(2 * qd + half)
            off = (c0 - quad_chunk0[qd]) * ct
            lanes = slice(half * ct, (half + 1) * ct)
            s = s_ref[qd, off:off + NA_KEYS, lanes] + bias_ref[var]
            m = jnp.max(s, axis=0, keepdims=True)
            p_ref[qd, off:off + NA_KEYS, lanes] = jnp.exp2(s - m).astype(BF16)
            rest = 0 if off else NA_KEYS
            p_ref[qd, rest:rest + ct, lanes] = zero_block
    for qd in range(n_quads):
        q0 = 2 * qd * ct
        vt = jnp.concatenate([vt_ref[quad_chunk0[qd] + c] for c in range(NA_QUAD_CHUNKS)], axis=1)
        ot = jnp.dot(jnp.concatenate([vt, ones_rows], axis=0), p_ref[qd], preferred_element_type=F32)
        ot = ot[:NA_HEAD_DIM] * (1.0 / ot[NA_HEAD_DIM:NA_HEAD_DIM + 1])
        for half in range(2):
            rows_q = slice(q0 + half * ct, q0 + (half + 1) * ct)
            o = ot[:, half * ct:(half + 1) * ct].T * g_ref[rows_q, :].astype(F32)
            o_ref[rows_q, :] = o.astype(o_ref.dtype)


def _na_bias_table(rpb, rows):
    n_pairs = rows // NA_PAIR_ROWS
    cols = np.arange(GRID_W)
    col_start = np.clip(cols - NA_WIN_COLS // 2, 0, GRID_W - NA_WIN_COLS)
    col_valid = (cols[None, :] >= col_start[:, None]) & (cols[None, :] < col_start[:, None] + NA_WIN_COLS)
    pad = GRID_W - NA_WIN_COLS
    rp = jnp.pad(rpb.astype(F32), ((0, 0), (0, 0), (pad, pad)))
    toep = jnp.stack([rp[:, :, GRID_W - 1 - qc: 2 * GRID_W - 1 - qc] for qc in range(GRID_W)], axis=2)
    toep = jnp.where(col_valid[None, None], toep, -jnp.inf)
    toep_t = toep.transpose(0, 1, 3, 2)
    masked = jnp.full((NA_HEADS, GRID_W, GRID_W), -jnp.inf, F32)
    tables = []
    for pr in (0, 1, 2, n_pairs - 2, n_pairs - 1):
        base = int(np.clip(NA_PAIR_ROWS * pr - NA_WIN_ROWS // 2, 0, rows - NA_KEY_ROWS))
        slabs = []
        for w in range(NA_KEY_ROWS):
            per_row = []
            for i in range(NA_PAIR_ROWS):
                r_q = NA_PAIR_ROWS * pr + i
                rs = int(np.clip(r_q - NA_WIN_ROWS // 2, 0, rows - NA_WIN_ROWS))
                kr = base + w
                if rs <= kr < rs + NA_WIN_ROWS:
                    per_row.append(toep_t[:, kr - r_q + NA_WIN_ROWS - 1])
                else:
                    per_row.append(masked)
            slabs.append(jnp.concatenate(per_row, axis=-1))
        tables.append(jnp.concatenate(slabs, axis=1))
    return jnp.stack(tables, axis=1) * LOG2E


def _na_attention(proj, bias, batch, seq):
    rows = seq // GRID_W
    n_pairs = rows // NA_PAIR_ROWS
    n_quads = n_pairs // 2
    assert rows % (2 * NA_PAIR_ROWS) == 0 and rows >= NA_KEY_ROWS + 2 * NA_PAIR_ROWS
    hd = NA_HEAD_DIM
    blk = lambda off: pl.BlockSpec((seq, hd), lambda b, h, off=off: (b, off // hd + h))
    return pl.pallas_call(
        functools.partial(_na_kernel, rows=rows),
        grid=(batch, NA_HEADS),
        in_specs=[blk(OFF_NA_Q), blk(OFF_NA_K), blk(OFF_NA_V), blk(OFF_NA_G),
                  pl.BlockSpec((None, NA_VARIANTS, NA_KEYS, NA_PAIR_TOKENS), lambda b, h: (h, 0, 0, 0))],
        out_specs=pl.BlockSpec((seq, hd), lambda b, h: (b, h)),
        out_shape=jax.ShapeDtypeStruct((batch * seq, NA_WIDTH), BF16),
        scratch_shapes=[pltpu.VMEM((n_pairs, hd, NA_PAIR_TOKENS), BF16),
                        pltpu.VMEM((n_quads, NA_QUAD_CHUNKS * NA_PAIR_TOKENS, 2 * NA_PAIR_TOKENS), F32),
                        pltpu.VMEM((n_quads, NA_QUAD_CHUNKS * NA_PAIR_TOKENS, 2 * NA_PAIR_TOKENS), BF16)],
        compiler_params=_params(("arbitrary", "arbitrary")),
        name="na_attention",
    )(proj, proj, proj, proj, bias)


RET_UNROLL = 32


def _ret_kernel(ld_ref, q_ref, k_ref, v_ref, g_ref, gain_ref, cos_ref, sin_ref, o_ref,
                qr_ref, kr_ref, kv_ref, st_ref, dsum_ref, qd_ref, kd_ref, *, n_chunks):
    c_len, dk = RET_CHUNK, RET_QK_DIM
    unroll = min(RET_UNROLL, n_chunks)
    h = pl.program_id(1)
    ldf = -jnp.abs(ld_ref[0, h])
    ldb = -jnp.abs(ld_ref[1, h])
    row = lax.broadcasted_iota(jnp.int32, (c_len, c_len), 0).astype(F32)
    col = lax.broadcasted_iota(jnp.int32, (c_len, c_len), 1).astype(F32)
    diff = row - col
    dsum_ref[...] = (jnp.where(diff >= 0, jnp.exp(ldf * jnp.maximum(diff, 0.0)), 0.0)
                     + jnp.where(diff <= 0, jnp.exp(ldb * jnp.maximum(-diff, 0.0)), 0.0))
    qd_ref[:, :dk] = jnp.exp(ldf * (row + 1.0))
    qd_ref[:, dk:] = jnp.exp(ldb * (c_len - row))
    kd_ref[:, :dk] = jnp.exp(ldf * (c_len - 1.0 - row))
    kd_ref[:, dk:] = jnp.exp(ldb * row)
    zero_row = jnp.zeros((1, RET_V_DIM), F32)
    cdf = jnp.exp(zero_row + ldf * c_len)
    cdb = jnp.exp(zero_row + ldb * c_len)
    chunk = lambda c: pl.ds(pl.multiple_of(c * c_len, c_len), c_len)

    def rotate(c, carry):
        rows = chunk(c)
        cos, sin = cos_ref[rows, :], sin_ref[rows, :]
        for src, dst in ((q_ref, qr_ref), (k_ref, kr_ref)):
            t = src[rows, :].astype(F32)
            dst[rows, :] = t * cos + pltpu.roll(t, dk // 2, axis=1) * sin
        return carry

    lax.fori_loop(0, n_chunks, rotate, 0, unroll=unroll)

    def chunk_kv(c, carry):
        rows = chunk(c)
        k = kr_ref[rows, :]
        k2 = (jnp.concatenate([k, k], axis=1) * kd_ref[...]).astype(BF16)
        kv_ref[c] = lax.dot_general(k2, v_ref[rows, :], (((0,), (0,)), ((), ())),
                                    preferred_element_type=F32)
        return carry

    lax.fori_loop(0, n_chunks, chunk_kv, 0, unroll=unroll)

    def scan_fwd(c, s):
        st_ref[c, :dk, :] = s.astype(BF16)
        return s * cdf + kv_ref[c, :dk, :]

    def scan_bwd(t, s):
        c = n_chunks - 1 - t
        st_ref[c, dk:, :] = s.astype(BF16)
        return s * cdb + kv_ref[c, dk:, :]

    zero_state = jnp.zeros((dk, RET_V_DIM), F32)
    lax.fori_loop(0, n_chunks, scan_fwd, zero_state, unroll=unroll)
    lax.fori_loop(0, n_chunks, scan_bwd, zero_state, unroll=unroll)

    def chunk_out(c, carry):
        rows = chunk(c)
        q = qr_ref[rows, :]
        v = v_ref[rows, :]
        sc = lax.dot_general(q.astype(BF16), kr_ref[rows, :].astype(BF16), (((1,), (1,)), ((), ())),
                             preferred_element_type=F32)
        o = jnp.dot((sc * dsum_ref[...]).astype(BF16), v, preferred_element_type=F32)
        q2 = (jnp.concatenate([q, q], axis=1) * qd_ref[...]).astype(BF16)
        o = o + jnp.dot(q2, st_ref[c], preferred_element_type=F32)
        o = o * lax.rsqrt(jnp.mean(o * o, axis=-1, keepdims=True) + EPS)
        o = o * gain_ref[...] * g_ref[rows, :].astype(F32)
        o_ref[rows, :] = o.astype(o_ref.dtype)
        return carry

    lax.fori_loop(0, n_chunks, chunk_out, 0, unroll=unroll)


def _retention(proj, ld, gain, cos_t, sin_t, batch, seq):
    n_chunks = seq // RET_CHUNK
    dk, dv, c_len = RET_QK_DIM, RET_V_DIM, RET_CHUNK
    assert n_chunks % min(RET_UNROLL, n_chunks) == 0
    qk_blk = lambda off: pl.BlockSpec((seq, dk), lambda b, h, off=off: (b, off // dk + h))
    v_blk = lambda off: pl.BlockSpec((seq, dv), lambda b, h, off=off: (b, off // dv + h))
    rope_blk = pl.BlockSpec((seq, dk), lambda b, h: (0, 0))
    return pl.pallas_call(
        functools.partial(_ret_kernel, n_chunks=n_chunks),
        grid=(batch, RET_HEADS),
        in_specs=[pl.BlockSpec(memory_space=pltpu.SMEM),
                  qk_blk(OFF_R_Q), qk_blk(OFF_R_K), v_blk(OFF_R_V), v_blk(OFF_R_G),
                  pl.BlockSpec((None, 1, dv), lambda b, h: (h, 0, 0)), rope_blk, rope_blk],
        out_specs=pl.BlockSpec((seq, dv), lambda b, h: (b, h)),
        out_shape=jax.ShapeDtypeStruct((batch * seq, RET_V_WIDTH), BF16),
        scratch_shapes=[pltpu.VMEM((seq, dk), F32),
                        pltpu.VMEM((seq, dk), F32),
                        pltpu.VMEM((n_chunks, 2 * dk, dv), F32),
                        pltpu.VMEM((n_chunks, 2 * dk, dv), BF16),
                        pltpu.VMEM((c_len, c_len), F32),
                        pltpu.VMEM((c_len, 2 * dk), F32),
                        pltpu.VMEM((c_len, 2 * dk), F32)],
        compiler_params=_params(("arbitrary", "arbitrary")),
        name="retention",
    )(ld, proj, proj, proj, proj, gain.astype(F32).reshape(RET_HEADS, 1, dv), cos_t, sin_t)


def _merge_kernel(a_ref, b_ref, wa_ref, wb_ref, ga_ref, gb_ref, o_ref):
    ya = jnp.dot(a_ref[...], wa_ref[...], preferred_element_type=F32).astype(BF16)
    yb = jnp.dot(b_ref[...], wb_ref[...], preferred_element_type=F32).astype(BF16)
    o_ref[...] = ga_ref[...] * ya + gb_ref[...] * yb


def _merge(a, b, wa, wb, proj):
    t = a.shape[0]
    tm = min(MAX_TILE_M, t)
    ja, jb = OFF_GATE_A // TILE_N, OFF_GATE_B // TILE_N
    return pl.pallas_call(
        _merge_kernel,
        grid=(t // tm, D_MODEL // TILE_N),
        in_specs=[pl.BlockSpec((tm, NA_WIDTH), lambda i, j: (i, 0)),
                  pl.BlockSpec((tm, RET_V_WIDTH), lambda i, j: (i, 0)),
                  pl.BlockSpec((NA_WIDTH, TILE_N), lambda i, j: (0, j)),
                  pl.BlockSpec((RET_V_WIDTH, TILE_N), lambda i, j: (0, j)),
                  pl.BlockSpec((tm, TILE_N), lambda i, j: (i, ja + j)),
                  pl.BlockSpec((tm, TILE_N), lambda i, j: (i, jb + j))],
        out_specs=pl.BlockSpec((tm, TILE_N), lambda i, j: (i, j)),
        out_shape=jax.ShapeDtypeStruct((t, D_MODEL), BF16),
        compiler_params=_params(("arbitrary", "arbitrary")),
        name="merge",
    )(a, b, wa, wb, proj, proj)


def _outproj_kernel(x_ref, w_ref, *refs, n_cast, has_pre):
    n_in = n_cast + (2 if has_pre else 0)
    cast_src, pre_in = refs[:n_cast], refs[n_cast:n_in]
    o_ref, cast_dst, pre_out = refs[n_in], refs[n_in + 1:n_in + 1 + n_cast], refs[n_in + 1 + n_cast:]
    _side_cast(cast_src, cast_dst)
    x = x_ref[...]
    for half in range(2):
        cols = slice(half * (TILE_N // 2), (half + 1) * (TILE_N // 2))
        if half == 1 and has_pre:
            xn = _rmsnorm_rows(*pre_in, *pre_out)
            never = pl.program_id(0) < 0
            x = jnp.concatenate([jnp.where(never, xn, x[:xn.shape[0]]), x[xn.shape[0]:]], axis=0)
        o_ref[:, cols] = jnp.dot(x, w_ref[:, cols], preferred_element_type=F32).astype(o_ref.dtype)


def _out_proj(merged, w, side_ws=(), side_layer=0, pre=None):
    t = merged.shape[0]
    tm = min(MAX_TILE_M, t)
    grid = (t // tm, D_MODEL // TILE_N)
    side_in, side_out, side_shapes = _side_cast_specs(side_ws, side_layer, *grid)
    pre_args, pre_in, pre_out, pre_shapes = (), [], [], []
    if pre is not None:
        px, pg = pre
        t_pre = px.shape[0]
        rows = _hosted_slab_rows(t_pre, grid[0] * grid[1])
        slab = lambda i, j: (jnp.minimum(i * grid[1] + j, t_pre // rows - 1), 0)
        pre_args = (px, pg.reshape(1, D_MODEL))
        pre_in = [pl.BlockSpec((rows, D_MODEL), slab), pl.BlockSpec((1, D_MODEL), lambda i, j: (0, 0))]
        pre_out = [pl.BlockSpec((rows, D_MODEL), slab)]
        pre_shapes = [jax.ShapeDtypeStruct((t_pre, D_MODEL), BF16)]
    outs = pl.pallas_call(
        functools.partial(_outproj_kernel, n_cast=len(side_ws), has_pre=pre is not None),
        grid=grid,
        in_specs=[pl.BlockSpec((tm, D_MODEL), lambda i, j: (i, 0)),
                  pl.BlockSpec((D_MODEL, TILE_N), lambda i, j: (0, j))] + side_in + pre_in,
        out_specs=[pl.BlockSpec((tm, TILE_N), lambda i, j: (i, j))] + side_out + pre_out,
        out_shape=[jax.ShapeDtypeStruct((t, D_MODEL), BF16)] + side_shapes + pre_shapes,
        compiler_params=_params(("arbitrary", "arbitrary")),
        name="out_proj",
    )(merged, w, *side_ws, *pre_args)
    n_cast = len(side_ws)
    return outs[0], list(outs[1:1 + n_cast]), (outs[1 + n_cast] if pre is not None else None)


def _post_norm(x, y, g):
    t = x.shape[0]
    row = pl.BlockSpec((ROW_TILE, D_MODEL), lambda i: (i, 0))
    return pl.pallas_call(
        _post_kernel,
        grid=(t // ROW_TILE,),
        in_specs=[row, row, pl.BlockSpec((1, D_MODEL), lambda i: (0, 0))],
        out_specs=[row, pl.BlockSpec((ROW_TILE, LANES), lambda i: (i, 0))],
        out_shape=[jax.ShapeDtypeStruct((t, D_MODEL), BF16), jax.ShapeDtypeStruct((t, LANES), F32)],
        compiler_params=_params(("arbitrary",)),
        name="post_norm",
    )(x, y, g.reshape(1, D_MODEL))


PLE_TILE_N = 512


def _ple_kernel(xb_ref, wg_ref, p_ref, wp_ref, x_ref, y_ref, rs_ref, g_ref, *refs):
    cast_src, (o_ref,), cast_dst = _split_side_refs(refs, 1)
    _side_cast(cast_src, cast_dst)
    z = jnp.dot(xb_ref[...], wg_ref[...], preferred_element_type=F32)
    gate = 0.5 * jnp.tanh(0.5 * z) + 0.5
    upd = gate * jnp.dot(p_ref[...].astype(BF16), wp_ref[...], preferred_element_type=F32)
    rs = rs_ref[...]
    for c in range(PLE_TILE_N // LANES):
        sl = slice(c * LANES, (c + 1) * LANES)
        x1 = x_ref[:, sl] + y_ref[:, sl].astype(F32) * rs * g_ref[:, sl]
        o_ref[:, sl] = x1 + upd[:, sl]


def _ple(x, y, rs, g, x1b, p, wg, wp, side_ws=(), side_layer=0):
    t = x.shape[0]
    tm, tn = min(MAX_TILE_M, t), PLE_TILE_N
    grid = (t // tm, D_MODEL // tn)
    tile = pl.BlockSpec((tm, tn), lambda i, j: (i, j))
    side_in, side_out, side_shapes = _side_cast_specs(side_ws, side_layer, *grid)
    return pl.pallas_call(
        _ple_kernel,
        grid=grid,
        in_specs=[pl.BlockSpec((tm, D_MODEL), lambda i, j: (i, 0)),
                  pl.BlockSpec((D_MODEL, tn), lambda i, j: (0, j)),
                  pl.BlockSpec((tm, PLE_DIM), lambda i, j: (i, 0)),
                  pl.BlockSpec((PLE_DIM, tn), lambda i, j: (0, j)),
                  tile, tile,
                  pl.BlockSpec((tm, LANES), lambda i, j: (i, 0)),
                  pl.BlockSpec((1, tn), lambda i, j: (0, j))] + side_in,
        out_specs=[tile] + side_out,
        out_shape=[jax.ShapeDtypeStruct((t, D_MODEL), F32)] + side_shapes,
        compiler_params=_params(("arbitrary", "arbitrary")),
        name="ple",
    )(x1b, wg, p, wp, x, y, rs, g.reshape(1, D_MODEL), *side_ws)


def _rope_tables(seq):
    half = RET_QK_DIM // 2
    inv = ROPE_BASE ** (-jnp.arange(half, dtype=F32) / half)
    ang = jnp.arange(seq).astype(F32)[:, None] * inv[None, :]
    cos, sin = jnp.cos(ang), jnp.sin(ang)
    return jnp.concatenate([cos, cos], axis=-1), jnp.concatenate([-sin, sin], axis=-1)


SIDE_CAST_HOSTS = {"in_proj": ("w_in",), "out_proj": ("w_out",), "ple": ("w_ple_gate", "w_proj_a", "w_proj_b")}


def kernel(x_prompt, x_sample, p_prompt, p_sample, w_in, ln_pre, ln_post, na_rpb, ret_log_decay_fwd,
           ret_log_decay_bwd, ret_gn_gain, w_proj_a, w_proj_b, w_out, w_ple, w_ple_gate):
    depth = w_in.shape[0]
    stacked = {"w_in": w_in, "w_proj_a": w_proj_a, "w_proj_b": w_proj_b, "w_out": w_out,
               "w_ple": w_ple, "w_ple_gate": w_ple_gate}
    ps = [p_prompt, p_sample]
    shapes = [x.shape[:2] for x in (x_prompt, x_sample)]
    xs = [x.reshape(-1, D_MODEL) for x in (x_prompt, x_sample)]
    rope = {seq: _rope_tables(seq) for _, seq in shapes}
    small = [{"ln_pre": ln_pre[i].astype(F32),
              "ln_post": ln_post[i].astype(F32),
              "na_bias": {seq // GRID_W: _na_bias_table(na_rpb[i], seq // GRID_W) for _, seq in shapes},
              "ret_ld": jnp.stack([ret_log_decay_fwd[i], ret_log_decay_bwd[i]]).astype(F32),
              "ret_gn": ret_gn_gain[i]} for i in range(depth)]
    wts = [None] * depth
    wts[0] = {name: _cast_layer_bf16(w, 0) for name, w in stacked.items()}
    A, B = 0, 1

    def side(host, tr, i, hosting):
        (batch, seq), t = shapes[tr], shapes[tr][0] * shapes[tr][1]
        grid = {"in_proj": _in_proj_grid(t, seq)[1],
                "out_proj": (t // min(MAX_TILE_M, t), D_MODEL // TILE_N),
                "ple": (t // min(MAX_TILE_M, t), D_MODEL // PLE_TILE_N)}[host]
        names = tuple(n for n in SIDE_CAST_HOSTS[host] if hosting and _side_cast_fits(stacked[n], *grid))
        return names, dict(side_ws=tuple(stacked[n] for n in names), side_layer=i + 1)

    def in_stage(tr, i, hosting, post, xn=None):
        names, kw = side("in_proj", tr, i, hosting)
        if xn is None:
            xn = _rmsnorm_pre(xs[tr], small[i]["ln_pre"])
        proj, cast, post_out = _in_proj(xn, wts[i]["w_in"], shapes[tr][1], post=post, **kw)
        return proj, dict(zip(names, cast)), post_out

    def mixer_stage(tr, i, proj, hosting, pre=None):
        batch, seq = shapes[tr]
        names, kw = side("out_proj", tr, i, hosting)
        a = _na_attention(proj, small[i]["na_bias"][seq // GRID_W], batch, seq)
        b = _retention(proj, small[i]["ret_ld"], small[i]["ret_gn"], *rope[seq], batch, seq)
        merged = _merge(a, b, wts[i]["w_proj_a"], wts[i]["w_proj_b"], proj)
        y, cast, xn_other = _out_proj(merged, wts[i]["w_out"], pre=pre, **kw)
        return y, dict(zip(names, cast)), xn_other

    def ple_stage(tr, i, y, x1b, rs, hosting):
        names, kw = side("ple", tr, i, hosting)
        t = xs[tr].shape[0]
        xs[tr], *cast = _ple(xs[tr], y, rs, small[i]["ln_post"], x1b, ps[tr][i].reshape(t, PLE_DIM),
                             wts[i]["w_ple_gate"], wts[i]["w_ple"], **kw)
        return dict(zip(names, cast))

    y_b = None
    for i in range(depth):
        hosting = i + 1 < depth
        post_b = None if y_b is None else (xs[B], y_b, small[i - 1]["ln_post"])
        proj_a, casts, post_out_b = in_stage(A, i, hosting, post_b)
        if y_b is not None:
            ple_stage(B, i - 1, y_b, *post_out_b, False)
        y_a, cast, xn_b = mixer_stage(A, i, proj_a, hosting, pre=(xs[B], small[i]["ln_pre"]))
        casts.update(cast)
        proj_b, _, post_out_a = in_stage(B, i, False, (xs[A], y_a, small[i]["ln_post"]), xn=xn_b)
        casts.update(ple_stage(A, i, y_a, *post_out_a, hosting))
        y_b, _, _ = mixer_stage(B, i, proj_b, False)
        if hosting:
            wts[i + 1] = {name: casts[name] if name in casts else _cast_layer_bf16(w, i + 1)
                          for name, w in stacked.items()}
    x1b, rs = _post_norm(xs[B], y_b, small[depth - 1]["ln_post"])
    ple_stage(B, depth - 1, y_b, x1b, rs, False)
    return tuple(x.reshape(batch, seq, D_MODEL) for x, (batch, seq) in zip(xs, shapes))
```

```python
import functools

import numpy as np
import jax
import jax.numpy as jnp
from jax import lax
from jax.experimental import pallas as pl
from jax.experimental.pallas import tpu as pltpu

D_MODEL = 4096
GRID_W = 64
PLE_DIM = 256
NA_HEADS = 16
NA_HEAD_DIM = 128
NA_WIDTH = NA_HEADS * NA_HEAD_DIM
NA_WIN_ROWS = 8
NA_WIN_COLS = 16
RET_HEADS = 8
RET_QK_DIM = 128
RET_V_DIM = 256
RET_QK_WIDTH = RET_HEADS * RET_QK_DIM
RET_V_WIDTH = RET_HEADS * RET_V_DIM
RET_CHUNK = 128
ROPE_BASE = 10000.0
EPS = 1e-6

OFF_NA_Q = 0
OFF_NA_K = OFF_NA_Q + NA_WIDTH
OFF_NA_V = OFF_NA_K + NA_WIDTH
OFF_NA_G = OFF_NA_V + NA_WIDTH
OFF_R_Q = OFF_NA_G + NA_WIDTH
OFF_R_K = OFF_R_Q + RET_QK_WIDTH
OFF_R_V = OFF_R_K + RET_QK_WIDTH
OFF_R_G = OFF_R_V + RET_V_WIDTH
OFF_GATE_A = OFF_R_G + RET_V_WIDTH
OFF_GATE_B = OFF_GATE_A + D_MODEL
IN_WIDTH = OFF_GATE_B + D_MODEL

TILE_N = 1024
MAX_TILE_M = 1024
ROW_TILE = 256
V7X_VMEM_LIMIT_BYTES = 56 * 1024 * 1024
LANES = 128

F32 = jnp.float32
BF16 = jnp.bfloat16


def _params(semantics, vmem=V7X_VMEM_LIMIT_BYTES):
    return pltpu.CompilerParams(dimension_semantics=semantics, vmem_limit_bytes=vmem)


CAST_BLOCK_BYTES = 8 * 1024 * 1024


def _scaled_cast(w, scale, dtype):
    return (w if scale == 1.0 else w * scale).astype(dtype)


def _cast_kernel(w_ref, o_ref, *, scale):
    o_ref[...] = _scaled_cast(w_ref[...], scale, o_ref.dtype)


def _cast_layer_bf16(w, layer, scale=1.0):
    _, k, n = w.shape
    rb = max(r for r in range(16, k + 1, 16) if k % r == 0 and (r * n * 4 <= CAST_BLOCK_BYTES or r == 16))
    return pl.pallas_call(
        functools.partial(_cast_kernel, scale=scale),
        grid=(k // rb,),
        in_specs=[pl.BlockSpec((None, rb, n), lambda i: (layer, i, 0))],
        out_specs=pl.BlockSpec((rb, n), lambda i: (i, 0)),
        out_shape=jax.ShapeDtypeStruct((k, n), BF16),
        compiler_params=_params(("arbitrary",)),
        name="cast_bf16",
    )(w)


SUBLANES_BF16 = 16


SIDE_CAST_MAX_BYTES = 1024 * 1024


def _side_cast_grid_slab_fits(w, ni, nj):
    _, k, n = w.shape
    return (k % ni == 0 and n % nj == 0 and (k // ni) % SUBLANES_BF16 == 0 and (n // nj) % LANES == 0
            and (k // ni) * (n // nj) * 4 <= SIDE_CAST_MAX_BYTES)


def _side_cast_row_slab(w, n_steps):
    _, k, n = w.shape
    rows = [r for r in range(SUBLANES_BF16, k + 1, SUBLANES_BF16) if k % r == 0 and k // r <= n_steps]
    return min(rows) if rows and min(rows) * n * 4 <= SIDE_CAST_MAX_BYTES else None


def _side_cast_fits(w, ni, nj):
    return _side_cast_grid_slab_fits(w, ni, nj) or _side_cast_row_slab(w, ni * nj) is not None


def _side_cast_specs(side_casts, ni, nj):
    in_specs, out_specs, out_shapes = [], [], []
    for w, layer, _ in side_casts:
        _, k, n = w.shape
        if _side_cast_grid_slab_fits(w, ni, nj):
            blk = (k // ni, n // nj)
            in_specs.append(pl.BlockSpec((None,) + blk, lambda i, j, layer=layer: (layer, i, j)))
            out_specs.append(pl.BlockSpec(blk, lambda i, j: (i, j)))
        else:
            rows = _side_cast_row_slab(w, ni * nj)
            last = k // rows - 1
            in_specs.append(pl.BlockSpec(
                (None, rows, n), lambda i, j, layer=layer, last=last: (layer, jnp.minimum(i * nj + j, last), 0)))
            out_specs.append(pl.BlockSpec((rows, n), lambda i, j, last=last: (jnp.minimum(i * nj + j, last), 0)))
        out_shapes.append(jax.ShapeDtypeStruct((k, n), BF16))
    return in_specs, out_specs, out_shapes


def _split_side_refs(refs, n_out):
    n = (len(refs) - n_out) // 2
    return refs[:n], refs[n:n + n_out], refs[n + n_out:]


def _side_cast(src_refs, dst_refs, scales):
    for src, dst, scale in zip(src_refs, dst_refs, scales):
        dst[...] = _scaled_cast(src[...], scale, dst.dtype)


def _rmsnorm_rows(x_ref, g_ref, o_ref):
    x = x_ref[...]
    ms = jnp.mean(x * x, axis=-1, keepdims=True)
    xn = (x * lax.rsqrt(ms + EPS) * g_ref[...]).astype(o_ref.dtype)
    o_ref[...] = xn
    return xn


def _rmsnorm_kernel(x_ref, g_ref, o_ref):
    _rmsnorm_rows(x_ref, g_ref, o_ref)


def _rmsnorm_pre(x, g):
    t = x.shape[0]
    return pl.pallas_call(
        _rmsnorm_kernel,
        grid=(t // ROW_TILE,),
        in_specs=[pl.BlockSpec((ROW_TILE, D_MODEL), lambda i: (i, 0)),
                  pl.BlockSpec((1, D_MODEL), lambda i: (0, 0))],
        out_specs=pl.BlockSpec((ROW_TILE, D_MODEL), lambda i: (i, 0)),
        out_shape=jax.ShapeDtypeStruct((t, D_MODEL), BF16),
        compiler_params=_params(("arbitrary",)),
        name="rmsnorm_pre",
    )(x, g.reshape(1, D_MODEL))


_J_NA_K = OFF_NA_K // TILE_N
_J_NA_G = OFF_NA_G // TILE_N
_J_R_Q = OFF_R_Q // TILE_N
_J_R_K = OFF_R_K // TILE_N
_J_R_G = OFF_R_G // TILE_N
_J_GATE = OFF_GATE_A // TILE_N
assert NA_HEAD_DIM == RET_QK_DIM
QK_SCALE = NA_HEAD_DIM ** -0.5
LOG2E = float(np.log2(np.e))
NA_Q_SCALE = QK_SCALE * LOG2E


def _post_norm_rows(x_ref, y_ref, g_ref, ob_ref, rs_ref):
    y = y_ref[...].astype(F32)
    rs = lax.rsqrt(jnp.mean(y * y, axis=-1, keepdims=True) + EPS)
    x1b = (x_ref[...] + y * rs * g_ref[...]).astype(ob_ref.dtype)
    ob_ref[...] = x1b
    rs_ref[...] = jnp.broadcast_to(rs, rs_ref.shape)
    return x1b


def _post_kernel(x_ref, y_ref, g_ref, ob_ref, rs_ref):
    _post_norm_rows(x_ref, y_ref, g_ref, ob_ref, rs_ref)


def _hosted_slab_rows(t_post, n_steps):
    return min(r for r in range(SUBLANES_BF16, t_post + 1, SUBLANES_BF16)
               if t_post % r == 0 and t_post // r <= n_steps)


def _inproj_kernel(x_ref, w_ref, *refs, cast_scales, has_post):
    n_cast = len(cast_scales)
    n_in = n_cast + (3 if has_post else 0)
    cast_src, post_in = refs[:n_cast], refs[n_cast:n_in]
    o_ref, cast_dst, post_out = refs[n_in], refs[n_in + 1:n_in + 1 + n_cast], refs[n_in + 1 + n_cast:]
    j = pl.program_id(1)
    is_gate = j >= _J_GATE
    is_silu = ((j >= _J_NA_G) & (j < _J_R_Q)) | ((j >= _J_R_G) & (j < _J_GATE))
    scale = jnp.where(j < _J_NA_K, NA_Q_SCALE, jnp.where(j == _J_R_K, QK_SCALE, 1.0))

    def step(epilogue):
        _side_cast(cast_src, cast_dst, cast_scales)
        x = x_ref[...]
        for half in range(2):
            cols = slice(half * (TILE_N // 2), (half + 1) * (TILE_N // 2))
            if half == 1 and has_post:
                x1b = _post_norm_rows(*post_in, *post_out)
                never = pl.program_id(0) < 0
                x = jnp.concatenate([jnp.where(never, x1b, x[:x1b.shape[0]]), x[x1b.shape[0]:]], axis=0)
            y = jnp.dot(x, w_ref[:, cols], preferred_element_type=F32)
            o_ref[:, cols] = epilogue(y).astype(o_ref.dtype)

    pl.when(is_gate)(lambda: step(lambda y: 0.5 * jnp.tanh(0.5 * y) + 0.5))
    pl.when(is_silu)(lambda: step(lambda y: y * (0.5 * jnp.tanh(0.5 * y) + 0.5)))
    pl.when(jnp.logical_not(is_gate | is_silu))(lambda: step(lambda y: y * scale))


def _in_proj_grid(t, seq):
    tm = min(MAX_TILE_M, seq)
    return tm, (t // tm, IN_WIDTH // TILE_N)


def _in_proj(xn, w, seq, side_casts=(), post=None):
    t = xn.shape[0]
    tm, grid = _in_proj_grid(t, seq)
    tn = TILE_N
    side_in, side_out, side_shapes = _side_cast_specs(side_casts, *grid)
    post_args, post_in, post_out, post_shapes = (), [], [], []
    if post is not None:
        px, py, pg = post
        t_post = px.shape[0]
        rows = _hosted_slab_rows(t_post, grid[0] * grid[1])
        slab = lambda i, j: (jnp.minimum(i * grid[1] + j, t_post // rows - 1), 0)
        post_args = (px, py, pg.reshape(1, D_MODEL))
        post_in = [pl.BlockSpec((rows, D_MODEL), slab), pl.BlockSpec((rows, D_MODEL), slab),
                   pl.BlockSpec((1, D_MODEL), lambda i, j: (0, 0))]
        post_out = [pl.BlockSpec((rows, D_MODEL), slab), pl.BlockSpec((rows, LANES), slab)]
        post_shapes = [jax.ShapeDtypeStruct((t_post, D_MODEL), BF16), jax.ShapeDtypeStruct((t_post, LANES), F32)]
    outs = pl.pallas_call(
        functools.partial(_inproj_kernel, cast_scales=tuple(c[2] for c in side_casts), has_post=post is not None),
        grid=grid,
        in_specs=[pl.BlockSpec((tm, D_MODEL), lambda i, j: (i, 0)),
                  pl.BlockSpec((D_MODEL, tn), lambda i, j: (0, j))] + side_in + post_in,
        out_specs=[pl.BlockSpec((tm, tn), lambda i, j: (i, j))] + side_out + post_out,
        out_shape=[jax.ShapeDtypeStruct((t, IN_WIDTH), BF16)] + side_shapes + post_shapes,
        compiler_params=_params(("arbitrary", "arbitrary")),
        name="in_proj",
    )(xn, w, *(c[0] for c in side_casts), *post_args)
    n_cast = len(side_casts)
    return outs[0], list(outs[1:1 + n_cast]), (tuple(outs[1 + n_cast:]) if post is not None else None)


NA_PAIR_ROWS = 2
NA_PAIR_TOKENS = NA_PAIR_ROWS * GRID_W
NA_KEY_ROWS = NA_WIN_ROWS + NA_PAIR_ROWS
NA_KEY_CHUNKS = NA_KEY_ROWS // NA_PAIR_ROWS
NA_KEYS = NA_KEY_ROWS * GRID_W
NA_VARIANTS = 5
NA_QUAD_CHUNKS = NA_KEY_CHUNKS + 1
NA_SUM_ROWS = 16


def _na_kernel(q_ref, k_ref, v_ref, g_ref, bias_ref, o_ref, vt_ref, s_ref, p_ref, *, rows):
    n_pairs = rows // NA_PAIR_ROWS
    n_chunks = n_pairs
    ct = NA_PAIR_TOKENS
    nt = (((1,), (1,)), ((), ()))
    n_quads = n_pairs // 2

    for c in range(n_chunks):
        vt_ref[c] = v_ref[c * ct:(c + 1) * ct, :].astype(F32).T.astype(BF16)
    ones_rows = jnp.ones((NA_SUM_ROWS, NA_QUAD_CHUNKS * ct), BF16)
    zero_block = jnp.zeros((ct, ct), BF16)

    def pair_window(pr):
        c0 = min(max(pr - NA_WIN_ROWS // 4, 0), n_chunks - NA_KEY_CHUNKS)
        var = pr if pr < 2 else (pr - (n_pairs - 2) + 3 if pr >= n_pairs - 2 else 2)
        return c0, var

    quad_chunk0 = [min(max(2 * qd - NA_WIN_ROWS // 4, 0), n_chunks - NA_QUAD_CHUNKS) for qd in range(n_quads)]
    for qd in range(n_quads):
        k0, q0 = quad_chunk0[qd] * ct, 2 * qd * ct
        s_ref[qd] = lax.dot_general(k_ref[k0:k0 + NA_QUAD_CHUNKS * ct, :], q_ref[q0:q0 + 2 * ct, :], nt,
                                    preferred_element_type=F32)
    for qd in range(n_quads):
        for half in range(2):
            c0, var = pair_window(2 * qd + half)
            off = (c0 - quad_chunk0[qd]) * ct
            lanes = slice(half * ct, (half + 1) * ct)
            s = s_ref[qd, off:off + NA_KEYS, lanes] + bias_ref[var]
            m = jnp.max(s, axis=0, keepdims=True)
            p_ref[qd, off:off + NA_KEYS, lanes] = jnp.exp2(s - m).astype(BF16)
            rest = 0 if off else NA_KEYS
            p_ref[qd, rest:rest + ct, lanes] = zero_block
    for qd in range(n_quads):
        q0 = 2 * qd * ct
        vt = jnp.concatenate([vt_ref[quad_chunk0[qd] + c] for c in range(NA_QUAD_CHUNKS)], axis=1)
        ot = jnp.dot(jnp.concatenate([vt, ones_rows], axis=0), p_ref[qd], preferred_element_type=F32)
        ot = ot[:NA_HEAD_DIM] * (1.0 / ot[NA_HEAD_DIM:NA_HEAD_DIM + 1])
        for half in range(2):
            rows_q = slice(q0 + half * ct, q0 + (half + 1) * ct)
            o = ot[:, half * ct:(half + 1) * ct].T * g_ref[rows_q, :].astype(F32)
            o_ref[rows_q, :] = o.astype(o_ref.dtype)


def _na_bias_table(rpb, rows):
    n_pairs = rows // NA_PAIR_ROWS
    cols = np.arange(GRID_W)
    col_start = np.clip(cols - NA_WIN_COLS // 2, 0, GRID_W - NA_WIN_COLS)
    col_valid = (cols[None, :] >= col_start[:, None]) & (cols[None, :] < col_start[:, None] + NA_WIN_COLS)
    pad = GRID_W - NA_WIN_COLS
    rp = jnp.pad(rpb.astype(F32), ((0, 0), (0, 0), (pad, pad)))
    toep = jnp.stack([rp[:, :, GRID_W - 1 - qc: 2 * GRID_W - 1 - qc] for qc in range(GRID_W)], axis=2)
    toep = jnp.where(col_valid[None, None], toep, -jnp.inf)
    toep_t = toep.transpose(0, 1, 3, 2)
    masked = jnp.full((NA_HEADS, GRID_W, GRID_W), -jnp.inf, F32)
    tables = []
    for pr in (0, 1, 2, n_pairs - 2, n_pairs - 1):
        base = int(np.clip(NA_PAIR_ROWS * pr - NA_WIN_ROWS // 2, 0, rows - NA_KEY_ROWS))
        slabs = []
        for w in range(NA_KEY_ROWS):
            per_row = []
            for i in range(NA_PAIR_ROWS):
                r_q = NA_PAIR_ROWS * pr + i
                rs = int(np.clip(r_q - NA_WIN_ROWS // 2, 0, rows - NA_WIN_ROWS))
                kr = base + w
                if rs <= kr < rs + NA_WIN_ROWS:
                    per_row.append(toep_t[:, kr - r_q + NA_WIN_ROWS - 1])
                else:
                    per_row.append(masked)
            slabs.append(jnp.concatenate(per_row, axis=-1))
        tables.append(jnp.concatenate(slabs, axis=1))
    return jnp.stack(tables, axis=1) * LOG2E


def _na_attention(proj, bias, batch, seq):
    rows = seq // GRID_W
    n_pairs = rows // NA_PAIR_ROWS
    n_quads = n_pairs // 2
    assert rows % (2 * NA_PAIR_ROWS) == 0 and rows >= NA_KEY_ROWS + 2 * NA_PAIR_ROWS
    hd = NA_HEAD_DIM
    blk = lambda off: pl.BlockSpec((seq, hd), lambda b, h, off=off: (b, off // hd + h))
    return pl.pallas_call(
        functools.partial(_na_kernel, rows=rows),
        grid=(batch, NA_HEADS),
        in_specs=[blk(OFF_NA_Q), blk(OFF_NA_K), blk(OFF_NA_V), blk(OFF_NA_G),
                  pl.BlockSpec((None, NA_VARIANTS, NA_KEYS, NA_PAIR_TOKENS), lambda b, h: (h, 0, 0, 0))],
        out_specs=pl.BlockSpec((seq, hd), lambda b, h: (b, h)),
        out_shape=jax.ShapeDtypeStruct((batch * seq, NA_WIDTH), BF16),
        scratch_shapes=[pltpu.VMEM((n_pairs, hd, NA_PAIR_TOKENS), BF16),
                        pltpu.VMEM((n_quads, NA_QUAD_CHUNKS * NA_PAIR_TOKENS, 2 * NA_PAIR_TOKENS), F32),
                        pltpu.VMEM((n_quads, NA_QUAD_CHUNKS * NA_PAIR_TOKENS, 2 * NA_PAIR_TOKENS), BF16)],
        compiler_params=_params(("arbitrary", "arbitrary")),
        name="na_attention",
    )(proj, proj, proj, proj, bias)


RET_UNROLL = 32


def _ret_kernel(ld_ref, q_ref, k_ref, v_ref, g_ref, gain_ref, cos_ref, sin_ref, o_ref,
                qr_ref, kr_ref, kv_ref, st_ref, dsum_ref, qd_ref, kd_ref, *, n_chunks):
    c_len, dk = RET_CHUNK, RET_QK_DIM
    unroll = min(RET_UNROLL, n_chunks)
    h = pl.program_id(1)
    ldf = -jnp.abs(ld_ref[0, h])
    ldb = -jnp.abs(ld_ref[1, h])
    row = lax.broadcasted_iota(jnp.int32, (c_len, c_len), 0).astype(F32)
    col = lax.broadcasted_iota(jnp.int32, (c_len, c_len), 1).astype(F32)
    diff = row - col
    dsum_ref[...] = (jnp.where(diff >= 0, jnp.exp(ldf * jnp.maximum(diff, 0.0)), 0.0)
                     + jnp.where(diff <= 0, jnp.exp(ldb * jnp.maximum(-diff, 0.0)), 0.0))
    qd_ref[:, :dk] = jnp.exp(ldf * (row + 1.0))
    qd_ref[:, dk:] = jnp.exp(ldb * (c_len - row))
    kd_ref[:, :dk] = jnp.exp(ldf * (c_len - 1.0 - row))
    kd_ref[:, dk:] = jnp.exp(ldb * row)
    zero_row = jnp.zeros((1, RET_V_DIM), F32)
    cdf = jnp.exp(zero_row + ldf * c_len)
    cdb = jnp.exp(zero_row + ldb * c_len)
    chunk = lambda c: pl.ds(pl.multiple_of(c * c_len, c_len), c_len)

    def rotate(c, carry):
        rows = chunk(c)
        cos, sin = cos_ref[rows, :], sin_ref[rows, :]
        for src, dst in ((q_ref, qr_ref), (k_ref, kr_ref)):
            t = src[rows, :].astype(F32)
            dst[rows, :] = t * cos + pltpu.roll(t, dk // 2, axis=1) * sin
        return carry

    lax.fori_loop(0, n_chunks, rotate, 0, unroll=unroll)

    def chunk_kv(c, carry):
        rows = chunk(c)
        k = kr_ref[rows, :]
        k2 = (jnp.concatenate([k, k], axis=1) * kd_ref[...]).astype(BF16)
        kv_ref[c] = lax.dot_general(k2, v_ref[rows, :], (((0,), (0,)), ((), ())),
                                    preferred_element_type=F32)
        return carry

    lax.fori_loop(0, n_chunks, chunk_kv, 0, unroll=unroll)

    def scan_fwd(c, s):
        st_ref[c, :dk, :] = s.astype(BF16)
        return s * cdf + kv_ref[c, :dk, :]

    def scan_bwd(t, s):
        c = n_chunks - 1 - t
        st_ref[c, dk:, :] = s.astype(BF16)
        return s * cdb + kv_ref[c, dk:, :]

    zero_state = jnp.zeros((dk, RET_V_DIM), F32)
    lax.fori_loop(0, n_chunks, scan_fwd, zero_state, unroll=unroll)
    lax.fori_loop(0, n_chunks, scan_bwd, zero_state, unroll=unroll)

    def chunk_out(c, carry):
        rows = chunk(c)
        q = qr_ref[rows, :]
        v = v_ref[rows, :]
        sc = lax.dot_general(q.astype(BF16), kr_ref[rows, :].astype(BF16), (((1,), (1,)), ((), ())),
                             preferred_element_type=F32)
        o = jnp.dot((sc * dsum_ref[...]).astype(BF16), v, preferred_element_type=F32)
        q2 = (jnp.concatenate([q, q], axis=1) * qd_ref[...]).astype(BF16)
        o = o + jnp.dot(q2, st_ref[c], preferred_element_type=F32)
        o = o * lax.rsqrt(jnp.mean(o * o, axis=-1, keepdims=True) + EPS)
        o = o * gain_ref[...] * g_ref[rows, :].astype(F32)
        o_ref[rows, :] = o.astype(o_ref.dtype)
        return carry

    lax.fori_loop(0, n_chunks, chunk_out, 0, unroll=unroll)


def _retention(proj, ld, gain, cos_t, sin_t, batch, seq):
    n_chunks = seq // RET_CHUNK
    dk, dv, c_len = RET_QK_DIM, RET_V_DIM, RET_CHUNK
    assert n_chunks % min(RET_UNROLL, n_chunks) == 0
    qk_blk = lambda off: pl.BlockSpec((seq, dk), lambda b, h, off=off: (b, off // dk + h))
    v_blk = lambda off: pl.BlockSpec((seq, dv), lambda b, h, off=off: (b, off // dv + h))
    rope_blk = pl.BlockSpec((seq, dk), lambda b, h: (0, 0))
    return pl.pallas_call(
        functools.partial(_ret_kernel, n_chunks=n_chunks),
        grid=(batch, RET_HEADS),
        in_specs=[pl.BlockSpec(memory_space=pltpu.SMEM),
                  qk_blk(OFF_R_Q), qk_blk(OFF_R_K), v_blk(OFF_R_V), v_blk(OFF_R_G),
                  pl.BlockSpec((None, 1, dv), lambda b, h: (h, 0, 0)), rope_blk, rope_blk],
        out_specs=pl.BlockSpec((seq, dv), lambda b, h: (b, h)),
        out_shape=jax.ShapeDtypeStruct((batch * seq, RET_V_WIDTH), BF16),
        scratch_shapes=[pltpu.VMEM((seq, dk), F32),
                        pltpu.VMEM((seq, dk), F32),
                        pltpu.VMEM((n_chunks, 2 * dk, dv), F32),
                        pltpu.VMEM((n_chunks, 2 * dk, dv), BF16),
                        pltpu.VMEM((c_len, c_len), F32),
                        pltpu.VMEM((c_len, 2 * dk), F32),
                        pltpu.VMEM((c_len, 2 * dk), F32)],
        compiler_params=_params(("arbitrary", "arbitrary")),
        name="retention",
    )(ld, proj, proj, proj, proj, gain.astype(F32).reshape(RET_HEADS, 1, dv), cos_t, sin_t)


def _merge_kernel(a_ref, b_ref, wa_ref, wb_ref, ga_ref, gb_ref, o_ref):
    ya = jnp.dot(a_ref[...], wa_ref[...], preferred_element_type=F32).astype(BF16)
    yb = jnp.dot(b_ref[...], wb_ref[...], preferred_element_type=F32).astype(BF16)
    o_ref[...] = ga_ref[...] * ya + gb_ref[...] * yb


def _merge(a, b, wa, wb, proj):
    t = a.shape[0]
    tm = min(MAX_TILE_M, t)
    ja, jb = OFF_GATE_A // TILE_N, OFF_GATE_B // TILE_N
    return pl.pallas_call(
        _merge_kernel,
        grid=(t // tm, D_MODEL // TILE_N),
        in_specs=[pl.BlockSpec((tm, NA_WIDTH), lambda i, j: (i, 0)),
                  pl.BlockSpec((tm, RET_V_WIDTH), lambda i, j: (i, 0)),
                  pl.BlockSpec((NA_WIDTH, TILE_N), lambda i, j: (0, j)),
                  pl.BlockSpec((RET_V_WIDTH, TILE_N), lambda i, j: (0, j)),
                  pl.BlockSpec((tm, TILE_N), lambda i, j: (i, ja + j)),
                  pl.BlockSpec((tm, TILE_N), lambda i, j: (i, jb + j))],
        out_specs=pl.BlockSpec((tm, TILE_N), lambda i, j: (i, j)),
        out_shape=jax.ShapeDtypeStruct((t, D_MODEL), BF16),
        compiler_params=_params(("arbitrary", "arbitrary")),
        name="merge",
    )(a, b, wa, wb, proj, proj)


def _outproj_kernel(x_ref, w_ref, *refs, cast_scales, has_pre):
    n_cast = len(cast_scales)
    n_in = n_cast + (2 if has_pre else 0)
    cast_src, pre_in = refs[:n_cast], refs[n_cast:n_in]
    o_ref, cast_dst, pre_out = refs[n_in], refs[n_in + 1:n_in + 1 + n_cast], refs[n_in + 1 + n_cast:]
    _side_cast(cast_src, cast_dst, cast_scales)
    x = x_ref[...]
    for half in range(2):
        cols = slice(half * (TILE_N // 2), (half + 1) * (TILE_N // 2))
        if half == 1 and has_pre:
            xn = _rmsnorm_rows(*pre_in, *pre_out)
            never = pl.program_id(0) < 0
            x = jnp.concatenate([jnp.where(never, xn, x[:xn.shape[0]]), x[xn.shape[0]:]], axis=0)
        o_ref[:, cols] = jnp.dot(x, w_ref[:, cols], preferred_element_type=F32).astype(o_ref.dtype)


def _out_proj(merged, w, side_casts=(), pre=None):
    t = merged.shape[0]
    tm = min(MAX_TILE_M, t)
    grid = (t // tm, D_MODEL // TILE_N)
    side_in, side_out, side_shapes = _side_cast_specs(side_casts, *grid)
    pre_args, pre_in, pre_out, pre_shapes = (), [], [], []
    if pre is not None:
        px, pg = pre
        t_pre = px.shape[0]
        rows = _hosted_slab_rows(t_pre, grid[0] * grid[1])
        slab = lambda i, j: (jnp.minimum(i * grid[1] + j, t_pre // rows - 1), 0)
        pre_args = (px, pg.reshape(1, D_MODEL))
        pre_in = [pl.BlockSpec((rows, D_MODEL), slab), pl.BlockSpec((1, D_MODEL), lambda i, j: (0, 0))]
        pre_out = [pl.BlockSpec((rows, D_MODEL), slab)]
        pre_shapes = [jax.ShapeDtypeStruct((t_pre, D_MODEL), BF16)]
    outs = pl.pallas_call(
        functools.partial(_outproj_kernel, cast_scales=tuple(c[2] for c in side_casts), has_pre=pre is not None),
        grid=grid,
        in_specs=[pl.BlockSpec((tm, D_MODEL), lambda i, j: (i, 0)),
                  pl.BlockSpec((D_MODEL, TILE_N), lambda i, j: (0, j))] + side_in + pre_in,
        out_specs=[pl.BlockSpec((tm, TILE_N), lambda i, j: (i, j))] + side_out + pre_out,
        out_shape=[jax.ShapeDtypeStruct((t, D_MODEL), BF16)] + side_shapes + pre_shapes,
        compiler_params=_params(("arbitrary", "arbitrary")),
        name="out_proj",
    )(merged, w, *(c[0] for c in side_casts), *pre_args)
    n_cast = len(side_casts)
    return outs[0], list(outs[1:1 + n_cast]), (outs[1 + n_cast] if pre is not None else None)


def _post_norm(x, y, g):
    t = x.shape[0]
    row = pl.BlockSpec((ROW_TILE, D_MODEL), lambda i: (i, 0))
    return pl.pallas_call(
        _post_kernel,
        grid=(t // ROW_TILE,),
        in_specs=[row, row, pl.BlockSpec((1, D_MODEL), lambda i: (0, 0))],
        out_specs=[row, pl.BlockSpec((ROW_TILE, LANES), lambda i: (i, 0))],
        out_shape=[jax.ShapeDtypeStruct((t, D_MODEL), BF16), jax.ShapeDtypeStruct((t, LANES), F32)],
        compiler_params=_params(("arbitrary",)),
        name="post_norm",
    )(x, y, g.reshape(1, D_MODEL))


PLE_TILE_N = 512
PLE_WEIGHT_SCALE = 0.5


def _ple_kernel(xb_ref, wg_ref, p_ref, wp_ref, x_ref, y_ref, rs_ref, g_ref, *refs, cast_scales):
    cast_src, (o_ref,), cast_dst = _split_side_refs(refs, 1)
    _side_cast(cast_src, cast_dst, cast_scales)
    t = jnp.tanh(jnp.dot(xb_ref[...], wg_ref[...], preferred_element_type=F32))
    h = jnp.dot(p_ref[...].astype(BF16), wp_ref[...], preferred_element_type=F32)
    upd = h + h * t
    rs = rs_ref[...]
    for c in range(PLE_TILE_N // LANES):
        sl = slice(c * LANES, (c + 1) * LANES)
        x1 = x_ref[:, sl] + y_ref[:, sl].astype(F32) * rs * g_ref[:, sl]
        o_ref[:, sl] = x1 + upd[:, sl]


def _ple(x, y, rs, g, x1b, p, wg, wp, side_casts=()):
    t = x.shape[0]
    tm, tn = min(MAX_TILE_M, t), PLE_TILE_N
    grid = (t // tm, D_MODEL // tn)
    tile = pl.BlockSpec((tm, tn), lambda i, j: (i, j))
    side_in, side_out, side_shapes = _side_cast_specs(side_casts, *grid)
    return pl.pallas_call(
        functools.partial(_ple_kernel, cast_scales=tuple(c[2] for c in side_casts)),
        grid=grid,
        in_specs=[pl.BlockSpec((tm, D_MODEL), lambda i, j: (i, 0)),
                  pl.BlockSpec((D_MODEL, tn), lambda i, j: (0, j)),
                  pl.BlockSpec((tm, PLE_DIM), lambda i, j: (i, 0)),
                  pl.BlockSpec((PLE_DIM, tn), lambda i, j: (0, j)),
                  tile, tile,
                  pl.BlockSpec((tm, LANES), lambda i, j: (i, 0)),
                  pl.BlockSpec((1, tn), lambda i, j: (0, j))] + side_in,
        out_specs=[tile] + side_out,
        out_shape=[jax.ShapeDtypeStruct((t, D_MODEL), F32)] + side_shapes,
        compiler_params=_params(("arbitrary", "arbitrary")),
        name="ple",
    )(x1b, wg, p, wp, x, y, rs, g.reshape(1, D_MODEL), *(c[0] for c in side_casts))


def _rope_tables(seq):
    half = RET_QK_DIM // 2
    inv = ROPE_BASE ** (-jnp.arange(half, dtype=F32) / half)
    ang = jnp.arange(seq).astype(F32)[:, None] * inv[None, :]
    cos, sin = jnp.cos(ang), jnp.sin(ang)
    return jnp.concatenate([cos, cos], axis=-1), jnp.concatenate([-sin, sin], axis=-1)


SIDE_CAST_HOSTS = {"in_proj": ("w_in",), "out_proj": ("w_out",), "ple": ("w_ple_gate", "w_proj_a", "w_proj_b")}
FIRST_IN_PROJ_CASTS = ("w_proj_a", "w_proj_b", "w_out", "w_ple_gate")


def kernel(x_prompt, x_sample, p_prompt, p_sample, w_in, ln_pre, ln_post, na_rpb, ret_log_decay_fwd,
           ret_log_decay_bwd, ret_gn_gain, w_proj_a, w_proj_b, w_out, w_ple, w_ple_gate):
    depth = w_in.shape[0]
    stacked = {"w_in": w_in, "w_proj_a": w_proj_a, "w_proj_b": w_proj_b, "w_out": w_out,
               "w_ple": w_ple, "w_ple_gate": w_ple_gate}
    ps = [p_prompt, p_sample]
    shapes = [x.shape[:2] for x in (x_prompt, x_sample)]
    xs = [x.reshape(-1, D_MODEL) for x in (x_prompt, x_sample)]
    rope = {seq: _rope_tables(seq) for _, seq in shapes}
    small = [{"ln_pre": ln_pre[i].astype(F32),
              "ln_post": ln_post[i].astype(F32),
              "na_bias": {seq // GRID_W: _na_bias_table(na_rpb[i], seq // GRID_W) for _, seq in shapes},
              "ret_ld": jnp.stack([ret_log_decay_fwd[i], ret_log_decay_bwd[i]]).astype(F32),
              "ret_gn": ret_gn_gain[i]} for i in range(depth)]
    scale_of = lambda name: PLE_WEIGHT_SCALE if name in ("w_ple", "w_ple_gate") else 1.0
    wts = [dict() for _ in range(depth)]
    A, B = 0, 1

    def side(host, tr, i, hosting):
        (batch, seq), t = shapes[tr], shapes[tr][0] * shapes[tr][1]
        grid = {"in_proj": _in_proj_grid(t, seq)[1],
                "out_proj": (t // min(MAX_TILE_M, t), D_MODEL // TILE_N),
                "ple": (t // min(MAX_TILE_M, t), D_MODEL // PLE_TILE_N)}[host]
        wanted = [(n, i + 1) for n in SIDE_CAST_HOSTS[host]] if hosting else []
        if host == "in_proj" and tr == A and i == 0:
            wanted += [(n, 0) for n in FIRST_IN_PROJ_CASTS]
        keys = [(n, layer) for n, layer in wanted if _side_cast_fits(stacked[n], *grid)]
        return keys, tuple((stacked[n], layer, scale_of(n)) for n, layer in keys)

    def keep(keys, casts):
        for (name, layer), w in zip(keys, casts):
            wts[layer][name] = w

    def weight(i, name):
        if name not in wts[i]:
            wts[i][name] = _cast_layer_bf16(stacked[name], i, scale_of(name))
        return wts[i][name]

    def in_stage(tr, i, hosting, post, xn=None):
        keys, side_casts = side("in_proj", tr, i, hosting)
        if xn is None:
            xn = _rmsnorm_pre(xs[tr], small[i]["ln_pre"])
        proj, casts, post_out = _in_proj(xn, weight(i, "w_in"), shapes[tr][1], side_casts=side_casts, post=post)
        keep(keys, casts)
        return proj, post_out

    def mixer_stage(tr, i, proj, hosting, pre=None):
        batch, seq = shapes[tr]
        keys, side_casts = side("out_proj", tr, i, hosting)
        a = _na_attention(proj, small[i]["na_bias"][seq // GRID_W], batch, seq)
        b = _retention(proj, small[i]["ret_ld"], small[i]["ret_gn"], *rope[seq], batch, seq)
        merged = _merge(a, b, weight(i, "w_proj_a"), weight(i, "w_proj_b"), proj)
        y, casts, xn_other = _out_proj(merged, weight(i, "w_out"), side_casts=side_casts, pre=pre)
        keep(keys, casts)
        return y, xn_other

    def ple_stage(tr, i, y, x1b, rs, hosting):
        keys, side_casts = side("ple", tr, i, hosting)
        t = xs[tr].shape[0]
        xs[tr], *casts = _ple(xs[tr], y, rs, small[i]["ln_post"], x1b, ps[tr][i].reshape(t, PLE_DIM),
                              weight(i, "w_ple_gate"), weight(i, "w_ple"), side_casts=side_casts)
        keep(keys, casts)

    y_b = None
    for i in range(depth):
        hosting = i + 1 < depth
        post_b = None if y_b is None else (xs[B], y_b, small[i - 1]["ln_post"])
        proj_a, post_out_b = in_stage(A, i, hosting, post_b)
        if y_b is not None:
            ple_stage(B, i - 1, y_b, *post_out_b, False)
        y_a, xn_b = mixer_stage(A, i, proj_a, hosting, pre=(xs[B], small[i]["ln_pre"]))
        proj_b, post_out_a = in_stage(B, i, False, (xs[A], y_a, small[i]["ln_post"]), xn=xn_b)
        ple_stage(A, i, y_a, *post_out_a, hosting)
        y_b, _ = mixer_stage(B, i, proj_b, False)
    x1b, rs = _post_norm(xs[B], y_b, small[depth - 1]["ln_post"])
    ple_stage(B, depth - 1, y_b, x1b, rs, False)
    return tuple(x.reshape(batch, seq, D_MODEL) for x, (batch, seq) in zip(xs, shapes))
```

```python
import functools

import numpy as np
import jax
import jax.numpy as jnp
from jax import lax
from jax.experimental import pallas as pl
from jax.experimental.pallas import tpu as pltpu

D_MODEL = 4096
GRID_W = 64
PLE_DIM = 256
NA_HEADS = 16
NA_HEAD_DIM = 128
NA_WIDTH = NA_HEADS * NA_HEAD_DIM
NA_WIN_ROWS = 8
NA_WIN_COLS = 16
RET_HEADS = 8
RET_QK_DIM = 128
RET_V_DIM = 256
RET_QK_WIDTH = RET_HEADS * RET_QK_DIM
RET_V_WIDTH = RET_HEADS * RET_V_DIM
RET_CHUNK = 128
ROPE_BASE = 10000.0
EPS = 1e-6

OFF_NA_Q = 0
OFF_NA_K = OFF_NA_Q + NA_WIDTH
OFF_NA_V = OFF_NA_K + NA_WIDTH
OFF_NA_G = OFF_NA_V + NA_WIDTH
OFF_R_Q = OFF_NA_G + NA_WIDTH
OFF_R_K = OFF_R_Q + RET_QK_WIDTH
OFF_R_V = OFF_R_K + RET_QK_WIDTH
OFF_R_G = OFF_R_V + RET_V_WIDTH
OFF_GATE_A = OFF_R_G + RET_V_WIDTH
OFF_GATE_B = OFF_GATE_A + D_MODEL
IN_WIDTH = OFF_GATE_B + D_MODEL

TILE_N = 1024
MAX_TILE_M = 1024
ROW_TILE = 256
V7X_VMEM_LIMIT_BYTES = 56 * 1024 * 1024
LANES = 128

F32 = jnp.float32
BF16 = jnp.bfloat16


def _params(semantics, vmem=V7X_VMEM_LIMIT_BYTES):
    return pltpu.CompilerParams(dimension_semantics=semantics, vmem_limit_bytes=vmem)


CAST_BLOCK_BYTES = 8 * 1024 * 1024


def _scaled_cast(w, scale, dtype):
    return (w if scale == 1.0 else w * scale).astype(dtype)


def _cast_kernel(w_ref, o_ref, *, scale):
    o_ref[...] = _scaled_cast(w_ref[...], scale, o_ref.dtype)


def _cast_layer_bf16(w, layer, scale=1.0):
    _, k, n = w.shape
    rb = max(r for r in range(16, k + 1, 16) if k % r == 0 and (r * n * 4 <= CAST_BLOCK_BYTES or r == 16))
    return pl.pallas_call(
        functools.partial(_cast_kernel, scale=scale),
        grid=(k // rb,),
        in_specs=[pl.BlockSpec((None, rb, n), lambda i: (layer, i, 0))],
        out_specs=pl.BlockSpec((rb, n), lambda i: (i, 0)),
        out_shape=jax.ShapeDtypeStruct((k, n), BF16),
        compiler_params=_params(("arbitrary",)),
        name="cast_bf16",
    )(w)


SUBLANES_BF16 = 16


SIDE_CAST_MAX_BYTES = 1024 * 1024


def _side_cast_grid_slab_fits(w, ni, nj):
    _, k, n = w.shape
    return (k % ni == 0 and n % nj == 0 and (k // ni) % SUBLANES_BF16 == 0 and (n // nj) % LANES == 0
            and (k // ni) * (n // nj) * 4 <= SIDE_CAST_MAX_BYTES)


def _side_cast_row_slab(w, n_steps):
    _, k, n = w.shape
    rows = [r for r in range(SUBLANES_BF16, k + 1, SUBLANES_BF16) if k % r == 0 and k // r <= n_steps]
    return min(rows) if rows and min(rows) * n * 4 <= SIDE_CAST_MAX_BYTES else None


def _side_cast_fits(w, ni, nj):
    return _side_cast_grid_slab_fits(w, ni, nj) or _side_cast_row_slab(w, ni * nj) is not None


def _side_cast_specs(side_casts, ni, nj):
    in_specs, out_specs, out_shapes = [], [], []
    for w, layer, _ in side_casts:
        _, k, n = w.shape
        if _side_cast_grid_slab_fits(w, ni, nj):
            blk = (k // ni, n // nj)
            in_specs.append(pl.BlockSpec((None,) + blk, lambda i, j, layer=layer: (layer, i, j)))
            out_specs.append(pl.BlockSpec(blk, lambda i, j: (i, j)))
        else:
            rows = _side_cast_row_slab(w, ni * nj)
            last = k // rows - 1
            in_specs.append(pl.BlockSpec(
                (None, rows, n), lambda i, j, layer=layer, last=last: (layer, jnp.minimum(i * nj + j, last), 0)))
            out_specs.append(pl.BlockSpec((rows, n), lambda i, j, last=last: (jnp.minimum(i * nj + j, last), 0)))
        out_shapes.append(jax.ShapeDtypeStruct((k, n), BF16))
    return in_specs, out_specs, out_shapes


def _split_side_refs(refs, n_out):
    n = (len(refs) - n_out) // 2
    return refs[:n], refs[n:n + n_out], refs[n + n_out:]


def _side_cast(src_refs, dst_refs, scales):
    for src, dst, scale in zip(src_refs, dst_refs, scales):
        dst[...] = _scaled_cast(src[...], scale, dst.dtype)


def _rmsnorm_rows(x_ref, g_ref, o_ref):
    x = x_ref[...]
    ms = jnp.mean(x * x, axis=-1, keepdims=True)
    xn = (x * lax.rsqrt(ms + EPS) * g_ref[...]).astype(o_ref.dtype)
    o_ref[...] = xn
    return xn


def _rmsnorm_kernel(x_ref, g_ref, o_ref):
    _rmsnorm_rows(x_ref, g_ref, o_ref)


def _rmsnorm_pre(x, g):
    t = x.shape[0]
    return pl.pallas_call(
        _rmsnorm_kernel,
        grid=(t // ROW_TILE,),
        in_specs=[pl.BlockSpec((ROW_TILE, D_MODEL), lambda i: (i, 0)),
                  pl.BlockSpec((1, D_MODEL), lambda i: (0, 0))],
        out_specs=pl.BlockSpec((ROW_TILE, D_MODEL), lambda i: (i, 0)),
        out_shape=jax.ShapeDtypeStruct((t, D_MODEL), BF16),
        compiler_params=_params(("arbitrary",)),
        name="rmsnorm_pre",
    )(x, g.reshape(1, D_MODEL))


_J_NA_K = OFF_NA_K // TILE_N
_J_NA_G = OFF_NA_G // TILE_N
_J_R_Q = OFF_R_Q // TILE_N
_J_R_K = OFF_R_K // TILE_N
_J_R_G = OFF_R_G // TILE_N
_J_GATE = OFF_GATE_A // TILE_N
assert NA_HEAD_DIM == RET_QK_DIM
QK_SCALE = NA_HEAD_DIM ** -0.5
LOG2E = float(np.log2(np.e))
NA_Q_SCALE = QK_SCALE * LOG2E


def _post_norm_rows(x_ref, y_ref, g_ref, ob_ref, rs_ref):
    y = y_ref[...].astype(F32)
    rs = lax.rsqrt(jnp.mean(y * y, axis=-1, keepdims=True) + EPS)
    x1b = (x_ref[...] + y * rs * g_ref[...]).astype(ob_ref.dtype)
    ob_ref[...] = x1b
    rs_ref[...] = jnp.broadcast_to(rs, rs_ref.shape)
    return x1b


def _post_kernel(x_ref, y_ref, g_ref, ob_ref, rs_ref):
    _post_norm_rows(x_ref, y_ref, g_ref, ob_ref, rs_ref)


def _hosted_slab_rows(t_post, n_steps):
    return min(r for r in range(SUBLANES_BF16, t_post + 1, SUBLANES_BF16)
               if t_post % r == 0 and t_post // r <= n_steps)


def _inproj_kernel(x_ref, w_ref, *refs, cast_scales, has_post):
    n_cast = len(cast_scales)
    n_in = n_cast + (3 if has_post else 0)
    cast_src, post_in = refs[:n_cast], refs[n_cast:n_in]
    o_ref, cast_dst, post_out = refs[n_in], refs[n_in + 1:n_in + 1 + n_cast], refs[n_in + 1 + n_cast:]
    j = pl.program_id(1)
    is_gate = j >= _J_GATE
    is_silu = ((j >= _J_NA_G) & (j < _J_R_Q)) | ((j >= _J_R_G) & (j < _J_GATE))
    scale = jnp.where(j < _J_NA_K, NA_Q_SCALE, jnp.where(j == _J_R_K, QK_SCALE, 1.0))

    def step(epilogue):
        _side_cast(cast_src, cast_dst, cast_scales)
        x = x_ref[...]
        for half in range(2):
            cols = slice(half * (TILE_N // 2), (half + 1) * (TILE_N // 2))
            if half == 1 and has_post:
                x1b = _post_norm_rows(*post_in, *post_out)
                never = pl.program_id(0) < 0
                x = jnp.concatenate([jnp.where(never, x1b, x[:x1b.shape[0]]), x[x1b.shape[0]:]], axis=0)
            y = jnp.dot(x, w_ref[:, cols], preferred_element_type=F32)
            o_ref[:, cols] = epilogue(y).astype(o_ref.dtype)

    pl.when(is_gate)(lambda: step(lambda y: 0.5 * jnp.tanh(0.5 * y) + 0.5))
    pl.when(is_silu)(lambda: step(lambda y: y * (0.5 * jnp.tanh(0.5 * y) + 0.5)))
    pl.when(jnp.logical_not(is_gate | is_silu))(lambda: step(lambda y: y * scale))


def _in_proj_grid(t, seq):
    tm = min(MAX_TILE_M, seq)
    return tm, (t // tm, IN_WIDTH // TILE_N)


def _in_proj(xn, w, seq, side_casts=(), post=None):
    t = xn.shape[0]
    tm, grid = _in_proj_grid(t, seq)
    tn = TILE_N
    side_in, side_out, side_shapes = _side_cast_specs(side_casts, *grid)
    post_args, post_in, post_out, post_shapes = (), [], [], []
    if post is not None:
        px, py, pg = post
        t_post = px.shape[0]
        rows = _hosted_slab_rows(t_post, grid[0] * grid[1])
        slab = lambda i, j: (jnp.minimum(i * grid[1] + j, t_post // rows - 1), 0)
        post_args = (px, py, pg.reshape(1, D_MODEL))
        post_in = [pl.BlockSpec((rows, D_MODEL), slab), pl.BlockSpec((rows, D_MODEL), slab),
                   pl.BlockSpec((1, D_MODEL), lambda i, j: (0, 0))]
        post_out = [pl.BlockSpec((rows, D_MODEL), slab), pl.BlockSpec((rows, LANES), slab)]
        post_shapes = [jax.ShapeDtypeStruct((t_post, D_MODEL), BF16), jax.ShapeDtypeStruct((t_post, LANES), F32)]
    outs = pl.pallas_call(
        functools.partial(_inproj_kernel, cast_scales=tuple(c[2] for c in side_casts), has_post=post is not None),
        grid=grid,
        in_specs=[pl.BlockSpec((tm, D_MODEL), lambda i, j: (i, 0)),
                  pl.BlockSpec((D_MODEL, tn), lambda i, j: (0, j))] + side_in + post_in,
        out_specs=[pl.BlockSpec((tm, tn), lambda i, j: (i, j))] + side_out + post_out,
        out_shape=[jax.ShapeDtypeStruct((t, IN_WIDTH), BF16)] + side_shapes + post_shapes,
        compiler_params=_params(("arbitrary", "arbitrary")),
        name="in_proj",
    )(xn, w, *(c[0] for c in side_casts), *post_args)
    n_cast = len(side_casts)
    return outs[0], list(outs[1:1 + n_cast]), (tuple(outs[1 + n_cast:]) if post is not None else None)


NA_PAIR_ROWS = 2
NA_PAIR_TOKENS = NA_PAIR_ROWS * GRID_W
NA_KEY_ROWS = NA_WIN_ROWS + NA_PAIR_ROWS
NA_KEY_CHUNKS = NA_KEY_ROWS // NA_PAIR_ROWS
NA_KEYS = NA_KEY_ROWS * GRID_W
NA_VARIANTS = 5
NA_QUAD_CHUNKS = NA_KEY_CHUNKS + 1
NA_SUM_ROWS = 16


def _na_kernel(q_ref, k_ref, v_ref, g_ref, bias_ref, o_ref, vt_ref, s_ref, p_ref, *, rows):
    n_pairs = rows // NA_PAIR_ROWS
    n_chunks = n_pairs
    ct = NA_PAIR_TOKENS
    nt = (((1,), (1,)), ((), ()))
    n_quads = n_pairs // 2

    for c in range(n_chunks):
        vt_ref[c] = v_ref[c * ct:(c + 1) * ct, :].astype(F32).T.astype(BF16)
    ones_rows = jnp.ones((NA_SUM_ROWS, NA_QUAD_CHUNKS * ct), BF16)
    zero_block = jnp.zeros((ct, ct), BF16)

    def pair_window(pr):
        c0 = min(max(pr - NA_WIN_ROWS // 4, 0), n_chunks - NA_KEY_CHUNKS)
        var = pr if pr < 2 else (pr - (n_pairs - 2) + 3 if pr >= n_pairs - 2 else 2)
        return c0, var

    quad_chunk0 = [min(max(2 * qd - NA_WIN_ROWS // 4, 0), n_chunks - NA_QUAD_CHUNKS) for qd in range(n_quads)]
    for qd in range(n_quads):
        k0, q0 = quad_chunk0[qd] * ct, 2 * qd * ct
        s_ref[qd] = lax.dot_general(k_ref[k0:k0 + NA_QUAD_CHUNKS * ct, :], q_ref[q0:q0 + 2 * ct, :], nt,
                                    preferred_element_type=F32)
    for qd in range(n_quads):
        for half in range(2):
            c0, var = pair_window(2 * qd + half)
            off = (c0 - quad_chunk0[qd]) * ct
            lanes = slice(half * ct, (half + 1) * ct)
            s = s_ref[qd, off:off + NA_KEYS, lanes] + bias_ref[var]
            m = jnp.max(s, axis=0, keepdims=True)
            p_ref[qd, off:off + NA_KEYS, lanes] = jnp.exp2(s - m).astype(BF16)
            rest = 0 if off else NA_KEYS
            p_ref[qd, rest:rest + ct, lanes] = zero_block
    for qd in range(n_quads):
        q0 = 2 * qd * ct
        vt = jnp.concatenate([vt_ref[quad_chunk0[qd] + c] for c in range(NA_QUAD_CHUNKS)], axis=1)
        ot = jnp.dot(jnp.concatenate([vt, ones_rows], axis=0), p_ref[qd], preferred_element_type=F32)
        ot = ot[:NA_HEAD_DIM] * (1.0 / ot[NA_HEAD_DIM:NA_HEAD_DIM + 1])
        for half in range(2):
            rows_q = slice(q0 + half * ct, q0 + (half + 1) * ct)
            o = ot[:, half * ct:(half + 1) * ct].T * g_ref[rows_q, :].astype(F32)
            o_ref[rows_q, :] = o.astype(o_ref.dtype)


def _na_bias_table(rpb, rows):
    n_pairs = rows // NA_PAIR_ROWS
    cols = np.arange(GRID_W)
    col_start = np.clip(cols - NA_WIN_COLS // 2, 0, GRID_W - NA_WIN_COLS)
    col_valid = (cols[None, :] >= col_start[:, None]) & (cols[None, :] < col_start[:, None] + NA_WIN_COLS)
    n_off, n_rel = 2 * NA_WIN_ROWS - 1, 2 * NA_WIN_COLS - 1
    rel = cols[:, None] - cols[None, :] + NA_WIN_COLS - 1
    pick_col = ((np.arange(n_rel)[:, None, None] == rel[None]) & col_valid.T[None]).astype(np.float32)
    pick_row = np.zeros((NA_VARIANTS, NA_KEY_ROWS, NA_PAIR_ROWS, n_off), np.float32)
    for v, pr in enumerate((0, 1, 2, n_pairs - 2, n_pairs - 1)):
        base = int(np.clip(NA_PAIR_ROWS * pr - NA_WIN_ROWS // 2, 0, rows - NA_KEY_ROWS))
        for w in range(NA_KEY_ROWS):
            for i in range(NA_PAIR_ROWS):
                r_q = NA_PAIR_ROWS * pr + i
                rs = int(np.clip(r_q - NA_WIN_ROWS // 2, 0, rows - NA_WIN_ROWS))
                if rs <= base + w < rs + NA_WIN_ROWS:
                    pick_row[v, w, i, base + w - r_q + NA_WIN_ROWS - 1] = 1.0
    valid = pick_row.sum(-1)[:, :, None, :, None] * col_valid.T[None, None, :, None, :]
    mask = np.where(valid > 0, 0.0, -np.inf).astype(np.float32).reshape(NA_VARIANTS, NA_KEYS, NA_PAIR_TOKENS)
    exact = lax.Precision.HIGHEST
    toep_t = jnp.einsum("hrd,dkq->hrkq", rpb.astype(F32), pick_col, precision=exact)
    table = jnp.einsum("vwir,hrkq->hvwkiq", pick_row, toep_t, precision=exact)
    table = table.reshape(NA_HEADS, NA_VARIANTS, NA_KEYS, NA_PAIR_TOKENS)
    return (table + mask[None]) * LOG2E


def _na_attention(proj, bias, batch, seq):
    rows = seq // GRID_W
    n_pairs = rows // NA_PAIR_ROWS
    n_quads = n_pairs // 2
    assert rows % (2 * NA_PAIR_ROWS) == 0 and rows >= NA_KEY_ROWS + 2 * NA_PAIR_ROWS
    hd = NA_HEAD_DIM
    blk = lambda off: pl.BlockSpec((seq, hd), lambda b, h, off=off: (b, off // hd + h))
    return pl.pallas_call(
        functools.partial(_na_kernel, rows=rows),
        grid=(batch, NA_HEADS),
        in_specs=[blk(OFF_NA_Q), blk(OFF_NA_K), blk(OFF_NA_V), blk(OFF_NA_G),
                  pl.BlockSpec((None, NA_VARIANTS, NA_KEYS, NA_PAIR_TOKENS), lambda b, h: (h, 0, 0, 0))],
        out_specs=pl.BlockSpec((seq, hd), lambda b, h: (b, h)),
        out_shape=jax.ShapeDtypeStruct((batch * seq, NA_WIDTH), BF16),
        scratch_shapes=[pltpu.VMEM((n_pairs, hd, NA_PAIR_TOKENS), BF16),
                        pltpu.VMEM((n_quads, NA_QUAD_CHUNKS * NA_PAIR_TOKENS, 2 * NA_PAIR_TOKENS), F32),
                        pltpu.VMEM((n_quads, NA_QUAD_CHUNKS * NA_PAIR_TOKENS, 2 * NA_PAIR_TOKENS), BF16)],
        compiler_params=_params(("arbitrary", "arbitrary")),
        name="na_attention",
    )(proj, proj, proj, proj, bias)


RET_UNROLL = 32


def _ret_kernel(ld_ref, q_ref, k_ref, v_ref, g_ref, gain_ref, cos_ref, sin_ref, o_ref,
                qr_ref, kr_ref, kv_ref, st_ref, dsum_ref, qd_ref, kd_ref, *, n_chunks):
    c_len, dk = RET_CHUNK, RET_QK_DIM
    unroll = min(RET_UNROLL, n_chunks)
    h = pl.program_id(1)
    ldf = -jnp.abs(ld_ref[0, h])
    ldb = -jnp.abs(ld_ref[1, h])
    row = lax.broadcasted_iota(jnp.int32, (c_len, c_len), 0).astype(F32)
    col = lax.broadcasted_iota(jnp.int32, (c_len, c_len), 1).astype(F32)
    diff = row - col
    dsum_ref[...] = (jnp.where(diff >= 0, jnp.exp(ldf * jnp.maximum(diff, 0.0)), 0.0)
                     + jnp.where(diff <= 0, jnp.exp(ldb * jnp.maximum(-diff, 0.0)), 0.0))
    qd_ref[:, :dk] = jnp.exp(ldf * (row + 1.0))
    qd_ref[:, dk:] = jnp.exp(ldb * (c_len - row))
    kd_ref[:, :dk] = jnp.exp(ldf * (c_len - 1.0 - row))
    kd_ref[:, dk:] = jnp.exp(ldb * row)
    zero_row = jnp.zeros((1, RET_V_DIM), F32)
    cdf = jnp.exp(zero_row + ldf * c_len)
    cdb = jnp.exp(zero_row + ldb * c_len)
    chunk = lambda c: pl.ds(pl.multiple_of(c * c_len, c_len), c_len)

    def rotate(c, carry):
        rows = chunk(c)
        cos, sin = cos_ref[rows, :], sin_ref[rows, :]
        for src, dst in ((q_ref, qr_ref), (k_ref, kr_ref)):
            t = src[rows, :].astype(F32)
            dst[rows, :] = t * cos + pltpu.roll(t, dk // 2, axis=1) * sin
        return carry

    lax.fori_loop(0, n_chunks, rotate, 0, unroll=unroll)

    def chunk_kv(c, carry):
        rows = chunk(c)
        k = kr_ref[rows, :]
        k2 = (jnp.concatenate([k, k], axis=1) * kd_ref[...]).astype(BF16)
        kv_ref[c] = lax.dot_general(k2, v_ref[rows, :], (((0,), (0,)), ((), ())),
                                    preferred_element_type=F32)
        return carry

    lax.fori_loop(0, n_chunks, chunk_kv, 0, unroll=unroll)

    def scan_fwd(c, s):
        st_ref[c, :dk, :] = s.astype(BF16)
        return s * cdf + kv_ref[c, :dk, :]

    def scan_bwd(t, s):
        c = n_chunks - 1 - t
        st_ref[c, dk:, :] = s.astype(BF16)
        return s * cdb + kv_ref[c, dk:, :]

    zero_state = jnp.zeros((dk, RET_V_DIM), F32)
    lax.fori_loop(0, n_chunks, scan_fwd, zero_state, unroll=unroll)
    lax.fori_loop(0, n_chunks, scan_bwd, zero_state, unroll=unroll)

    def chunk_out(c, carry):
        rows = chunk(c)
        q = qr_ref[rows, :]
        v = v_ref[rows, :]
        sc = lax.dot_general(q.astype(BF16), kr_ref[rows, :].astype(BF16), (((1,), (1,)), ((), ())),
                             preferred_element_type=F32)
        o = jnp.dot((sc * dsum_ref[...]).astype(BF16), v, preferred_element_type=F32)
        q2 = (jnp.concatenate([q, q], axis=1) * qd_ref[...]).astype(BF16)
        o = o + jnp.dot(q2, st_ref[c], preferred_element_type=F32)
        o = o * lax.rsqrt(jnp.mean(o * o, axis=-1, keepdims=True) + EPS)
        o = o * gain_ref[...] * g_ref[rows, :].astype(F32)
        o_ref[rows, :] = o.astype(o_ref.dtype)
        return carry

    lax.fori_loop(0, n_chunks, chunk_out, 0, unroll=unroll)


def _retention(proj, ld, gain, cos_t, sin_t, batch, seq):
    n_chunks = seq // RET_CHUNK
    dk, dv, c_len = RET_QK_DIM, RET_V_DIM, RET_CHUNK
    assert n_chunks % min(RET_UNROLL, n_chunks) == 0
    qk_blk = lambda off: pl.BlockSpec((seq, dk), lambda b, h, off=off: (b, off // dk + h))
    v_blk = lambda off: pl.BlockSpec((seq, dv), lambda b, h, off=off: (b, off // dv + h))
    rope_blk = pl.BlockSpec((seq, dk), lambda b, h: (0, 0))
    return pl.pallas_call(
        functools.partial(_ret_kernel, n_chunks=n_chunks),
        grid=(batch, RET_HEADS),
        in_specs=[pl.BlockSpec(memory_space=pltpu.SMEM),
                  qk_blk(OFF_R_Q), qk_blk(OFF_R_K), v_blk(OFF_R_V), v_blk(OFF_R_G),
                  pl.BlockSpec((None, 1, dv), lambda b, h: (h, 0, 0)), rope_blk, rope_blk],
        out_specs=pl.BlockSpec((seq, dv), lambda b, h: (b, h)),
        out_shape=jax.ShapeDtypeStruct((batch * seq, RET_V_WIDTH), BF16),
        scratch_shapes=[pltpu.VMEM((seq, dk), F32),
                        pltpu.VMEM((seq, dk), F32),
                        pltpu.VMEM((n_chunks, 2 * dk, dv), F32),
                        pltpu.VMEM((n_chunks, 2 * dk, dv), BF16),
                        pltpu.VMEM((c_len, c_len), F32),
                        pltpu.VMEM((c_len, 2 * dk), F32),
                        pltpu.VMEM((c_len, 2 * dk), F32)],
        compiler_params=_params(("arbitrary", "arbitrary")),
        name="retention",
    )(ld, proj, proj, proj, proj, gain.astype(F32).reshape(RET_HEADS, 1, dv), cos_t, sin_t)


def _merge_kernel(a_ref, b_ref, wa_ref, wb_ref, ga_ref, gb_ref, o_ref):
    ya = jnp.dot(a_ref[...], wa_ref[...], preferred_element_type=F32).astype(BF16)
    yb = jnp.dot(b_ref[...], wb_ref[...], preferred_element_type=F32).astype(BF16)
    o_ref[...] = ga_ref[...] * ya + gb_ref[...] * yb


def _merge(a, b, wa, wb, proj):
    t = a.shape[0]
    tm = min(MAX_TILE_M, t)
    ja, jb = OFF_GATE_A // TILE_N, OFF_GATE_B // TILE_N
    return pl.pallas_call(
        _merge_kernel,
        grid=(t // tm, D_MODEL // TILE_N),
        in_specs=[pl.BlockSpec((tm, NA_WIDTH), lambda i, j: (i, 0)),
                  pl.BlockSpec((tm, RET_V_WIDTH), lambda i, j: (i, 0)),
                  pl.BlockSpec((NA_WIDTH, TILE_N), lambda i, j: (0, j)),
                  pl.BlockSpec((RET_V_WIDTH, TILE_N), lambda i, j: (0, j)),
                  pl.BlockSpec((tm, TILE_N), lambda i, j: (i, ja + j)),
                  pl.BlockSpec((tm, TILE_N), lambda i, j: (i, jb + j))],
        out_specs=pl.BlockSpec((tm, TILE_N), lambda i, j: (i, j)),
        out_shape=jax.ShapeDtypeStruct((t, D_MODEL), BF16),
        compiler_params=_params(("arbitrary", "arbitrary")),
        name="merge",
    )(a, b, wa, wb, proj, proj)


def _outproj_kernel(x_ref, w_ref, *refs, cast_scales, has_pre):
    n_cast = len(cast_scales)
    n_in = n_cast + (2 if has_pre else 0)
    cast_src, pre_in = refs[:n_cast], refs[n_cast:n_in]
    o_ref, cast_dst, pre_out = refs[n_in], refs[n_in + 1:n_in + 1 + n_cast], refs[n_in + 1 + n_cast:]
    _side_cast(cast_src, cast_dst, cast_scales)
    x = x_ref[...]
    for half in range(2):
        cols = slice(half * (TILE_N // 2), (half + 1) * (TILE_N // 2))
        if half == 1 and has_pre:
            xn = _rmsnorm_rows(*pre_in, *pre_out)
            never = pl.program_id(0) < 0
            x = jnp.concatenate([jnp.where(never, xn, x[:xn.shape[0]]), x[xn.shape[0]:]], axis=0)
        o_ref[:, cols] = jnp.dot(x, w_ref[:, cols], preferred_element_type=F32).astype(o_ref.dtype)


def _out_proj(merged, w, side_casts=(), pre=None):
    t = merged.shape[0]
    tm = min(MAX_TILE_M, t)
    grid = (t // tm, D_MODEL // TILE_N)
    side_in, side_out, side_shapes = _side_cast_specs(side_casts, *grid)
    pre_args, pre_in, pre_out, pre_shapes = (), [], [], []
    if pre is not None:
        px, pg = pre
        t_pre = px.shape[0]
        rows = _hosted_slab_rows(t_pre, grid[0] * grid[1])
        slab = lambda i, j: (jnp.minimum(i * grid[1] + j, t_pre // rows - 1), 0)
        pre_args = (px, pg.reshape(1, D_MODEL))
        pre_in = [pl.BlockSpec((rows, D_MODEL), slab), pl.BlockSpec((1, D_MODEL), lambda i, j: (0, 0))]
        pre_out = [pl.BlockSpec((rows, D_MODEL), slab)]
        pre_shapes = [jax.ShapeDtypeStruct((t_pre, D_MODEL), BF16)]
    outs = pl.pallas_call(
        functools.partial(_outproj_kernel, cast_scales=tuple(c[2] for c in side_casts), has_pre=pre is not None),
        grid=grid,
        in_specs=[pl.BlockSpec((tm, D_MODEL), lambda i, j: (i, 0)),
                  pl.BlockSpec((D_MODEL, TILE_N), lambda i, j: (0, j))] + side_in + pre_in,
        out_specs=[pl.BlockSpec((tm, TILE_N), lambda i, j: (i, j))] + side_out + pre_out,
        out_shape=[jax.ShapeDtypeStruct((t, D_MODEL), BF16)] + side_shapes + pre_shapes,
        compiler_params=_params(("arbitrary", "arbitrary")),
        name="out_proj",
    )(merged, w, *(c[0] for c in side_casts), *pre_args)
    n_cast = len(side_casts)
    return outs[0], list(outs[1:1 + n_cast]), (outs[1 + n_cast] if pre is not None else None)


def _post_norm(x, y, g):
    t = x.shape[0]
    row = pl.BlockSpec((ROW_TILE, D_MODEL), lambda i: (i, 0))
    return pl.pallas_call(
        _post_kernel,
        grid=(t // ROW_TILE,),
        in_specs=[row, row, pl.BlockSpec((1, D_MODEL), lambda i: (0, 0))],
        out_specs=[row, pl.BlockSpec((ROW_TILE, LANES), lambda i: (i, 0))],
        out_shape=[jax.ShapeDtypeStruct((t, D_MODEL), BF16), jax.ShapeDtypeStruct((t, LANES), F32)],
        compiler_params=_params(("arbitrary",)),
        name="post_norm",
    )(x, y, g.reshape(1, D_MODEL))


PLE_TILE_N = 512
PLE_WEIGHT_SCALE = 0.5


def _ple_kernel(xb_ref, wg_ref, p_ref, wp_ref, x_ref, y_ref, rs_ref, g_ref, *refs, cast_scales):
    cast_src, (o_ref,), cast_dst = _split_side_refs(refs, 1)
    _side_cast(cast_src, cast_dst, cast_scales)
    t = jnp.tanh(jnp.dot(xb_ref[...], wg_ref[...], preferred_element_type=F32))
    h = jnp.dot(p_ref[...].astype(BF16), wp_ref[...], preferred_element_type=F32)
    upd = h + h * t
    rs = rs_ref[...]
    for c in range(PLE_TILE_N // LANES):
        sl = slice(c * LANES, (c + 1) * LANES)
        x1 = x_ref[:, sl] + y_ref[:, sl].astype(F32) * rs * g_ref[:, sl]
        o_ref[:, sl] = x1 + upd[:, sl]


def _ple(x, y, rs, g, x1b, p, wg, wp, side_casts=()):
    t = x.shape[0]
    tm, tn = min(MAX_TILE_M, t), PLE_TILE_N
    grid = (t // tm, D_MODEL // tn)
    tile = pl.BlockSpec((tm, tn), lambda i, j: (i, j))
    side_in, side_out, side_shapes = _side_cast_specs(side_casts, *grid)
    return pl.pallas_call(
        functools.partial(_ple_kernel, cast_scales=tuple(c[2] for c in side_casts)),
        grid=grid,
        in_specs=[pl.BlockSpec((tm, D_MODEL), lambda i, j: (i, 0)),
                  pl.BlockSpec((D_MODEL, tn), lambda i, j: (0, j)),
                  pl.BlockSpec((tm, PLE_DIM), lambda i, j: (i, 0)),
                  pl.BlockSpec((PLE_DIM, tn), lambda i, j: (0, j)),
                  tile, tile,
                  pl.BlockSpec((tm, LANES), lambda i, j: (i, 0)),
                  pl.BlockSpec((1, tn), lambda i, j: (0, j))] + side_in,
        out_specs=[tile] + side_out,
        out_shape=[jax.ShapeDtypeStruct((t, D_MODEL), F32)] + side_shapes,
        compiler_params=_params(("arbitrary", "arbitrary")),
        name="ple",
    )(x1b, wg, p, wp, x, y, rs, g.reshape(1, D_MODEL), *(c[0] for c in side_casts))


def _rope_tables(seq):
    half = RET_QK_DIM // 2
    inv = ROPE_BASE ** (-jnp.arange(half, dtype=F32) / half)
    ang = jnp.arange(seq).astype(F32)[:, None] * inv[None, :]
    cos, sin = jnp.cos(ang), jnp.sin(ang)
    return jnp.concatenate([cos, cos], axis=-1), jnp.concatenate([-sin, sin], axis=-1)


SIDE_CAST_HOSTS = {"in_proj": ("w_in",), "out_proj": ("w_out",), "ple": ("w_ple_gate", "w_proj_a", "w_proj_b")}
FIRST_IN_PROJ_CASTS = ("w_proj_a", "w_proj_b", "w_out", "w_ple_gate")


def kernel(x_prompt, x_sample, p_prompt, p_sample, w_in, ln_pre, ln_post, na_rpb, ret_log_decay_fwd,
           ret_log_decay_bwd, ret_gn_gain, w_proj_a, w_proj_b, w_out, w_ple, w_ple_gate):
    depth = w_in.shape[0]
    stacked = {"w_in": w_in, "w_proj_a": w_proj_a, "w_proj_b": w_proj_b, "w_out": w_out,
               "w_ple": w_ple, "w_ple_gate": w_ple_gate}
    ps = [p_prompt, p_sample]
    shapes = [x.shape[:2] for x in (x_prompt, x_sample)]
    xs = [x.reshape(-1, D_MODEL) for x in (x_prompt, x_sample)]
    rope = {seq: _rope_tables(seq) for _, seq in shapes}
    small = [{"ln_pre": ln_pre[i].astype(F32),
              "ln_post": ln_post[i].astype(F32),
              "na_bias": {seq // GRID_W: _na_bias_table(na_rpb[i], seq // GRID_W) for _, seq in shapes},
              "ret_ld": jnp.stack([ret_log_decay_fwd[i], ret_log_decay_bwd[i]]).astype(F32),
              "ret_gn": ret_gn_gain[i]} for i in range(depth)]
    scale_of = lambda name: PLE_WEIGHT_SCALE if name in ("w_ple", "w_ple_gate") else 1.0
    wts = [dict() for _ in range(depth)]
    A, B = 0, 1

    def side(host, tr, i, hosting):
        (batch, seq), t = shapes[tr], shapes[tr][0] * shapes[tr][1]
        grid = {"in_proj": _in_proj_grid(t, seq)[1],
                "out_proj": (t // min(MAX_TILE_M, t), D_MODEL // TILE_N),
                "ple": (t // min(MAX_TILE_M, t), D_MODEL // PLE_TILE_N)}[host]
        wanted = [(n, i + 1) for n in SIDE_CAST_HOSTS[host]] if hosting else []
        if host == "in_proj" and tr == A and i == 0:
            wanted += [(n, 0) for n in FIRST_IN_PROJ_CASTS]
        keys = [(n, layer) for n, layer in wanted if _side_cast_fits(stacked[n], *grid)]
        return keys, tuple((stacked[n], layer, scale_of(n)) for n, layer in keys)

    def keep(keys, casts):
        for (name, layer), w in zip(keys, casts):
            wts[layer][name] = w

    def weight(i, name):
        if name not in wts[i]:
            wts[i][name] = _cast_layer_bf16(stacked[name], i, scale_of(name))
        return wts[i][name]

    def in_stage(tr, i, hosting, post, xn=None):
        keys, side_casts = side("in_proj", tr, i, hosting)
        if xn is None:
            xn = _rmsnorm_pre(xs[tr], small[i]["ln_pre"])
        proj, casts, post_out = _in_proj(xn, weight(i, "w_in"), shapes[tr][1], side_casts=side_casts, post=post)
        keep(keys, casts)
        return proj, post_out

    def mixer_stage(tr, i, proj, hosting, pre=None):
        batch, seq = shapes[tr]
        keys, side_casts = side("out_proj", tr, i, hosting)
        a = _na_attention(proj, small[i]["na_bias"][seq // GRID_W], batch, seq)
        b = _retention(proj, small[i]["ret_ld"], small[i]["ret_gn"], *rope[seq], batch, seq)
        merged = _merge(a, b, weight(i, "w_proj_a"), weight(i, "w_proj_b"), proj)
        y, casts, xn_other = _out_proj(merged, weight(i, "w_out"), side_casts=side_casts, pre=pre)
        keep(keys, casts)
        return y, xn_other

    def ple_stage(tr, i, y, x1b, rs, hosting):
        keys, side_casts = side("ple", tr, i, hosting)
        t = xs[tr].shape[0]
        xs[tr], *casts = _ple(xs[tr], y, rs, small[i]["ln_post"], x1b, ps[tr][i].reshape(t, PLE_DIM),
                              weight(i, "w_ple_gate"), weight(i, "w_ple"), side_casts=side_casts)
        keep(keys, casts)

    y_b = None
    for i in range(depth):
        hosting = i + 1 < depth
        post_b = None if y_b is None else (xs[B], y_b, small[i - 1]["ln_post"])
        proj_a, post_out_b = in_stage(A, i, hosting, post_b)
        if y_b is not None:
            ple_stage(B, i - 1, y_b, *post_out_b, False)
        y_a, xn_b = mixer_stage(A, i, proj_a, hosting, pre=(xs[B], small[i]["ln_pre"]))
        proj_b, post_out_a = in_stage(B, i, False, (xs[A], y_a, small[i]["ln_post"]), xn=xn_b)
        ple_stage(A, i, y_a, *post_out_a, hosting)
        y_b, _ = mixer_stage(B, i, proj_b, False)
    x1b, rs = _post_norm(xs[B], y_b, small[depth - 1]["ln_post"])
    ple_stage(B, depth - 1, y_b, x1b, rs, False)
    return tuple(x.reshape(batch, seq, D_MODEL) for x, (batch, seq) in zip(xs, shapes))
```

```python
import functools

import numpy as np
import jax
import jax.numpy as jnp
from jax import lax
from jax.experimental import pallas as pl
from jax.experimental.pallas import tpu as pltpu

D_MODEL = 4096
GRID_W = 64
PLE_DIM = 256
NA_HEADS = 16
NA_HEAD_DIM = 128
NA_WIDTH = NA_HEADS * NA_HEAD_DIM
NA_WIN_ROWS = 8
NA_WIN_COLS = 16
RET_HEADS = 8
RET_QK_DIM = 128
RET_V_DIM = 256
RET_QK_WIDTH = RET_HEADS * RET_QK_DIM
RET_V_WIDTH = RET_HEADS * RET_V_DIM
RET_CHUNK = 128
ROPE_BASE = 10000.0
EPS = 1e-6

OFF_NA_Q = 0
OFF_NA_K = OFF_NA_Q + NA_WIDTH
OFF_NA_V = OFF_NA_K + NA_WIDTH
OFF_NA_G = OFF_NA_V + NA_WIDTH
OFF_R_Q = OFF_NA_G + NA_WIDTH
OFF_R_K = OFF_R_Q + RET_QK_WIDTH
OFF_R_V = OFF_R_K + RET_QK_WIDTH
OFF_R_G = OFF_R_V + RET_V_WIDTH
OFF_GATE_A = OFF_R_G + RET_V_WIDTH
OFF_GATE_B = OFF_GATE_A + D_MODEL
IN_WIDTH = OFF_GATE_B + D_MODEL

TILE_N = 1024
MAX_TILE_M = 1024
ROW_TILE = 256
V7X_VMEM_LIMIT_BYTES = 56 * 1024 * 1024
LANES = 128

F32 = jnp.float32
BF16 = jnp.bfloat16


def _params(semantics, vmem=V7X_VMEM_LIMIT_BYTES):
    return pltpu.CompilerParams(dimension_semantics=semantics, vmem_limit_bytes=vmem)


CAST_BLOCK_BYTES = 8 * 1024 * 1024


def _scaled_cast(w, scale, dtype):
    return (w if scale == 1.0 else w * scale).astype(dtype)


def _cast_kernel(w_ref, o_ref, *, scale):
    o_ref[...] = _scaled_cast(w_ref[...], scale, o_ref.dtype)


def _cast_layer_bf16(w, layer, scale=1.0):
    _, k, n = w.shape
    rb = max(r for r in range(16, k + 1, 16) if k % r == 0 and (r * n * 4 <= CAST_BLOCK_BYTES or r == 16))
    return pl.pallas_call(
        functools.partial(_cast_kernel, scale=scale),
        grid=(k // rb,),
        in_specs=[pl.BlockSpec((None, rb, n), lambda i: (layer, i, 0))],
        out_specs=pl.BlockSpec((rb, n), lambda i: (i, 0)),
        out_shape=jax.ShapeDtypeStruct((k, n), BF16),
        compiler_params=_params(("arbitrary",)),
        name="cast_bf16",
    )(w)


SUBLANES_BF16 = 16


SIDE_CAST_MAX_BYTES = 1024 * 1024


def _side_cast_grid_slab_fits(w, ni, nj):
    _, k, n = w.shape
    return (k % ni == 0 and n % nj == 0 and (k // ni) % SUBLANES_BF16 == 0 and (n // nj) % LANES == 0
            and (k // ni) * (n // nj) * 4 <= SIDE_CAST_MAX_BYTES)


def _side_cast_row_slab(w, n_steps):
    _, k, n = w.shape
    rows = [r for r in range(SUBLANES_BF16, k + 1, SUBLANES_BF16) if k % r == 0 and k // r <= n_steps]
    return min(rows) if rows and min(rows) * n * 4 <= SIDE_CAST_MAX_BYTES else None


def _side_cast_fits(w, ni, nj):
    return _side_cast_grid_slab_fits(w, ni, nj) or _side_cast_row_slab(w, ni * nj) is not None


def _side_cast_specs(side_casts, ni, nj):
    in_specs, out_specs, out_shapes = [], [], []
    for w, layer, _ in side_casts:
        _, k, n = w.shape
        if _side_cast_grid_slab_fits(w, ni, nj):
            blk = (k // ni, n // nj)
            in_specs.append(pl.BlockSpec((None,) + blk, lambda i, j, layer=layer: (layer, i, j)))
            out_specs.append(pl.BlockSpec(blk, lambda i, j: (i, j)))
        else:
            rows = _side_cast_row_slab(w, ni * nj)
            last = k // rows - 1
            in_specs.append(pl.BlockSpec(
                (None, rows, n), lambda i, j, layer=layer, last=last: (layer, jnp.minimum(i * nj + j, last), 0)))
            out_specs.append(pl.BlockSpec((rows, n), lambda i, j, last=last: (jnp.minimum(i * nj + j, last), 0)))
        out_shapes.append(jax.ShapeDtypeStruct((k, n), BF16))
    return in_specs, out_specs, out_shapes


def _split_side_refs(refs, n_out):
    n = (len(refs) - n_out) // 2
    return refs[:n], refs[n:n + n_out], refs[n + n_out:]


def _side_cast(src_refs, dst_refs, scales):
    for src, dst, scale in zip(src_refs, dst_refs, scales):
        dst[...] = _scaled_cast(src[...], scale, dst.dtype)


def _rmsnorm_rows(x_ref, g_ref, o_ref):
    x = x_ref[...]
    ms = jnp.mean(x * x, axis=-1, keepdims=True)
    xn = (x * lax.rsqrt(ms + EPS) * g_ref[...]).astype(o_ref.dtype)
    o_ref[...] = xn
    return xn


def _rmsnorm_kernel(x_ref, g_ref, o_ref):
    _rmsnorm_rows(x_ref, g_ref, o_ref)


def _rmsnorm_pre(x, g):
    t = x.shape[0]
    return pl.pallas_call(
        _rmsnorm_kernel,
        grid=(t // ROW_TILE,),
        in_specs=[pl.BlockSpec((ROW_TILE, D_MODEL), lambda i: (i, 0)),
                  pl.BlockSpec((1, D_MODEL), lambda i: (0, 0))],
        out_specs=pl.BlockSpec((ROW_TILE, D_MODEL), lambda i: (i, 0)),
        out_shape=jax.ShapeDtypeStruct((t, D_MODEL), BF16),
        compiler_params=_params(("arbitrary",)),
        name="rmsnorm_pre",
    )(x, g.reshape(1, D_MODEL))


_J_NA_K = OFF_NA_K // TILE_N
_J_NA_G = OFF_NA_G // TILE_N
_J_R_Q = OFF_R_Q // TILE_N
_J_R_K = OFF_R_K // TILE_N
_J_R_G = OFF_R_G // TILE_N
_J_GATE = OFF_GATE_A // TILE_N
assert NA_HEAD_DIM == RET_QK_DIM
QK_SCALE = NA_HEAD_DIM ** -0.5
LOG2E = float(np.log2(np.e))
NA_Q_SCALE = QK_SCALE * LOG2E


def _post_norm_rows(x_ref, y_ref, g_ref, ob_ref, rs_ref):
    y = y_ref[...].astype(F32)
    rs = lax.rsqrt(jnp.mean(y * y, axis=-1, keepdims=True) + EPS)
    x1b = (x_ref[...] + y * rs * g_ref[...]).astype(ob_ref.dtype)
    ob_ref[...] = x1b
    rs_ref[...] = jnp.broadcast_to(rs, rs_ref.shape)
    return x1b


def _post_kernel(x_ref, y_ref, g_ref, ob_ref, rs_ref):
    _post_norm_rows(x_ref, y_ref, g_ref, ob_ref, rs_ref)


def _hosted_slab_rows(t_post, n_steps):
    return min(r for r in range(SUBLANES_BF16, t_post + 1, SUBLANES_BF16)
               if t_post % r == 0 and t_post // r <= n_steps)


def _inproj_kernel(x_ref, w_ref, *refs, cast_scales, has_post):
    n_cast = len(cast_scales)
    n_in = n_cast + (3 if has_post else 0)
    cast_src, post_in = refs[:n_cast], refs[n_cast:n_in]
    o_ref, cast_dst, post_out = refs[n_in], refs[n_in + 1:n_in + 1 + n_cast], refs[n_in + 1 + n_cast:]
    j = pl.program_id(1)
    is_gate = j >= _J_GATE
    is_silu = ((j >= _J_NA_G) & (j < _J_R_Q)) | ((j >= _J_R_G) & (j < _J_GATE))
    scale = jnp.where(j < _J_NA_K, NA_Q_SCALE, jnp.where(j == _J_R_K, QK_SCALE, 1.0))

    def step(epilogue):
        _side_cast(cast_src, cast_dst, cast_scales)
        x = x_ref[...]
        for half in range(2):
            cols = slice(half * (TILE_N // 2), (half + 1) * (TILE_N // 2))
            if half == 1 and has_post:
                x1b = _post_norm_rows(*post_in, *post_out)
                never = pl.program_id(0) < 0
                x = jnp.concatenate([jnp.where(never, x1b, x[:x1b.shape[0]]), x[x1b.shape[0]:]], axis=0)
            y = jnp.dot(x, w_ref[:, cols], preferred_element_type=F32)
            o_ref[:, cols] = epilogue(y).astype(o_ref.dtype)

    pl.when(is_gate)(lambda: step(lambda y: 0.5 * jnp.tanh(0.5 * y) + 0.5))
    pl.when(is_silu)(lambda: step(lambda y: y * (0.5 * jnp.tanh(0.5 * y) + 0.5)))
    pl.when(jnp.logical_not(is_gate | is_silu))(lambda: step(lambda y: y * scale))


def _in_proj_grid(t, seq):
    tm = min(MAX_TILE_M, seq)
    return tm, (t // tm, IN_WIDTH // TILE_N)


def _in_proj(xn, w, seq, side_casts=(), post=None):
    t = xn.shape[0]
    tm, grid = _in_proj_grid(t, seq)
    tn = TILE_N
    side_in, side_out, side_shapes = _side_cast_specs(side_casts, *grid)
    post_args, post_in, post_out, post_shapes = (), [], [], []
    if post is not None:
        px, py, pg = post
        t_post = px.shape[0]
        rows = _hosted_slab_rows(t_post, grid[0] * grid[1])
        slab = lambda i, j: (jnp.minimum(i * grid[1] + j, t_post // rows - 1), 0)
        post_args = (px, py, pg.reshape(1, D_MODEL))
        post_in = [pl.BlockSpec((rows, D_MODEL), slab), pl.BlockSpec((rows, D_MODEL), slab),
                   pl.BlockSpec((1, D_MODEL), lambda i, j: (0, 0))]
        post_out = [pl.BlockSpec((rows, D_MODEL), slab), pl.BlockSpec((rows, LANES), slab)]
        post_shapes = [jax.ShapeDtypeStruct((t_post, D_MODEL), BF16), jax.ShapeDtypeStruct((t_post, LANES), F32)]
    outs = pl.pallas_call(
        functools.partial(_inproj_kernel, cast_scales=tuple(c[2] for c in side_casts), has_post=post is not None),
        grid=grid,
        in_specs=[pl.BlockSpec((tm, D_MODEL), lambda i, j: (i, 0)),
                  pl.BlockSpec((D_MODEL, tn), lambda i, j: (0, j))] + side_in + post_in,
        out_specs=[pl.BlockSpec((tm, tn), lambda i, j: (i, j))] + side_out + post_out,
        out_shape=[jax.ShapeDtypeStruct((t, IN_WIDTH), BF16)] + side_shapes + post_shapes,
        compiler_params=_params(("arbitrary", "arbitrary")),
        name="in_proj",
    )(xn, w, *(c[0] for c in side_casts), *post_args)
    n_cast = len(side_casts)
    return outs[0], list(outs[1:1 + n_cast]), (tuple(outs[1 + n_cast:]) if post is not None else None)


NA_PAIR_ROWS = 2
NA_PAIR_TOKENS = NA_PAIR_ROWS * GRID_W
NA_KEY_ROWS = NA_WIN_ROWS + NA_PAIR_ROWS
NA_KEY_CHUNKS = NA_KEY_ROWS // NA_PAIR_ROWS
NA_KEYS = NA_KEY_ROWS * GRID_W
NA_VARIANTS = 5
NA_QUAD_CHUNKS = NA_KEY_CHUNKS + 1
NA_SUM_ROWS = 16


def _na_kernel(q_ref, k_ref, v_ref, g_ref, bias_ref, o_ref, vt_ref, s_ref, p_ref, *, rows):
    n_pairs = rows // NA_PAIR_ROWS
    n_chunks = n_pairs
    ct = NA_PAIR_TOKENS
    nt = (((1,), (1,)), ((), ()))
    n_quads = n_pairs // 2

    for c in range(n_chunks):
        vt_ref[c] = v_ref[c * ct:(c + 1) * ct, :].astype(F32).T.astype(BF16)
    ones_rows = jnp.ones((NA_SUM_ROWS, NA_QUAD_CHUNKS * ct), BF16)
    zero_block = jnp.zeros((ct, ct), BF16)

    def pair_window(pr):
        c0 = min(max(pr - NA_WIN_ROWS // 4, 0), n_chunks - NA_KEY_CHUNKS)
        var = pr if pr < 2 else (pr - (n_pairs - 2) + 3 if pr >= n_pairs - 2 else 2)
        return c0, var

    quad_chunk0 = [min(max(2 * qd - NA_WIN_ROWS // 4, 0), n_chunks - NA_QUAD_CHUNKS) for qd in range(n_quads)]
    for qd in range(n_quads):
        k0, q0 = quad_chunk0[qd] * ct, 2 * qd * ct
        s_ref[qd] = lax.dot_general(k_ref[k0:k0 + NA_QUAD_CHUNKS * ct, :], q_ref[q0:q0 + 2 * ct, :], nt,
                                    preferred_element_type=F32)
    for qd in range(n_quads):
        for half in range(2):
            c0, var = pair_window(2 * qd + half)
            off = (c0 - quad_chunk0[qd]) * ct
            lanes = slice(half * ct, (half + 1) * ct)
            s = s_ref[qd, off:off + NA_KEYS, lanes] + bias_ref[var]
            m = jnp.max(s, axis=0, keepdims=True)
            p_ref[qd, off:off + NA_KEYS, lanes] = jnp.exp2(s - m).astype(BF16)
            rest = 0 if off else NA_KEYS
            p_ref[qd, rest:rest + ct, lanes] = zero_block
    for qd in range(n_quads):
        q0 = 2 * qd * ct
        vt = jnp.concatenate([vt_ref[quad_chunk0[qd] + c] for c in range(NA_QUAD_CHUNKS)], axis=1)
        ot = jnp.dot(jnp.concatenate([vt, ones_rows], axis=0), p_ref[qd], preferred_element_type=F32)
        ot = ot[:NA_HEAD_DIM] * (1.0 / ot[NA_HEAD_DIM:NA_HEAD_DIM + 1])
        for half in range(2):
            rows_q = slice(q0 + half * ct, q0 + (half + 1) * ct)
            o = ot[:, half * ct:(half + 1) * ct].T * g_ref[rows_q, :].astype(F32)
            o_ref[rows_q, :] = o.astype(o_ref.dtype)


def _na_bias_table(rpb, rows):
    n_pairs = rows // NA_PAIR_ROWS
    cols = np.arange(GRID_W)
    col_start = np.clip(cols - NA_WIN_COLS // 2, 0, GRID_W - NA_WIN_COLS)
    col_valid = (cols[None, :] >= col_start[:, None]) & (cols[None, :] < col_start[:, None] + NA_WIN_COLS)
    pad = GRID_W - NA_WIN_COLS
    rp = jnp.pad(rpb.astype(F32), ((0, 0), (0, 0), (pad, pad)))
    toep = jnp.stack([rp[:, :, GRID_W - 1 - qc: 2 * GRID_W - 1 - qc] for qc in range(GRID_W)], axis=2)
    toep = jnp.where(col_valid[None, None], toep, -jnp.inf)
    toep_t = toep.transpose(0, 1, 3, 2)
    tables = []
    for pr in (0, 1, 2, n_pairs - 2, n_pairs - 1):
        base = int(np.clip(NA_PAIR_ROWS * pr - NA_WIN_ROWS // 2, 0, rows - NA_KEY_ROWS))
        per_row = []
        for i in range(NA_PAIR_ROWS):
            r_q = NA_PAIR_ROWS * pr + i
            rs = int(np.clip(r_q - NA_WIN_ROWS // 2, 0, rows - NA_WIN_ROWS))
            w0, r0 = rs - base, rs - r_q + NA_WIN_ROWS - 1
            per_row.append(jnp.pad(toep_t[:, r0:r0 + NA_WIN_ROWS],
                                   ((0, 0), (w0, NA_KEY_ROWS - NA_WIN_ROWS - w0), (0, 0), (0, 0)),
                                   constant_values=-jnp.inf))
        tables.append(jnp.stack(per_row, axis=3).reshape(NA_HEADS, NA_KEYS, NA_PAIR_TOKENS))
    return jnp.stack(tables, axis=1) * LOG2E


def _na_attention(proj, bias, batch, seq):
    rows = seq // GRID_W
    n_pairs = rows // NA_PAIR_ROWS
    n_quads = n_pairs // 2
    assert rows % (2 * NA_PAIR_ROWS) == 0 and rows >= NA_KEY_ROWS + 2 * NA_PAIR_ROWS
    hd = NA_HEAD_DIM
    blk = lambda off: pl.BlockSpec((seq, hd), lambda b, h, off=off: (b, off // hd + h))
    return pl.pallas_call(
        functools.partial(_na_kernel, rows=rows),
        grid=(batch, NA_HEADS),
        in_specs=[blk(OFF_NA_Q), blk(OFF_NA_K), blk(OFF_NA_V), blk(OFF_NA_G),
                  pl.BlockSpec((None, NA_VARIANTS, NA_KEYS, NA_PAIR_TOKENS), lambda b, h: (h, 0, 0, 0))],
        out_specs=pl.BlockSpec((seq, hd), lambda b, h: (b, h)),
        out_shape=jax.ShapeDtypeStruct((batch * seq, NA_WIDTH), BF16),
        scratch_shapes=[pltpu.VMEM((n_pairs, hd, NA_PAIR_TOKENS), BF16),
                        pltpu.VMEM((n_quads, NA_QUAD_CHUNKS * NA_PAIR_TOKENS, 2 * NA_PAIR_TOKENS), F32),
                        pltpu.VMEM((n_quads, NA_QUAD_CHUNKS * NA_PAIR_TOKENS, 2 * NA_PAIR_TOKENS), BF16)],
        compiler_params=_params(("arbitrary", "arbitrary")),
        name="na_attention",
    )(proj, proj, proj, proj, bias)


RET_UNROLL = 32


def _ret_kernel(ld_ref, q_ref, k_ref, v_ref, g_ref, gain_ref, cos_ref, sin_ref, o_ref,
                qr_ref, kr_ref, kv_ref, st_ref, dsum_ref, qd_ref, kd_ref, *, n_chunks):
    c_len, dk = RET_CHUNK, RET_QK_DIM
    unroll = min(RET_UNROLL, n_chunks)
    h = pl.program_id(1)
    ldf = -jnp.abs(ld_ref[0, h])
    ldb = -jnp.abs(ld_ref[1, h])
    row = lax.broadcasted_iota(jnp.int32, (c_len, c_len), 0).astype(F32)
    col = lax.broadcasted_iota(jnp.int32, (c_len, c_len), 1).astype(F32)
    diff = row - col
    dsum_ref[...] = (jnp.where(diff >= 0, jnp.exp(ldf * jnp.maximum(diff, 0.0)), 0.0)
                     + jnp.where(diff <= 0, jnp.exp(ldb * jnp.maximum(-diff, 0.0)), 0.0))
    qd_ref[:, :dk] = jnp.exp(ldf * (row + 1.0))
    qd_ref[:, dk:] = jnp.exp(ldb * (c_len - row))
    kd_ref[:, :dk] = jnp.exp(ldf * (c_len - 1.0 - row))
    kd_ref[:, dk:] = jnp.exp(ldb * row)
    zero_row = jnp.zeros((1, RET_V_DIM), F32)
    cdf = jnp.exp(zero_row + ldf * c_len)
    cdb = jnp.exp(zero_row + ldb * c_len)
    chunk = lambda c: pl.ds(pl.multiple_of(c * c_len, c_len), c_len)

    def rotate(c, carry):
        rows = chunk(c)
        cos, sin = cos_ref[rows, :], sin_ref[rows, :]
        for src, dst in ((q_ref, qr_ref), (k_ref, kr_ref)):
            t = src[rows, :].astype(F32)
            dst[rows, :] = t * cos + pltpu.roll(t, dk // 2, axis=1) * sin
        return carry

    lax.fori_loop(0, n_chunks, rotate, 0, unroll=unroll)

    def chunk_kv(c, carry):
        rows = chunk(c)
        k = kr_ref[rows, :]
        k2 = (jnp.concatenate([k, k], axis=1) * kd_ref[...]).astype(BF16)
        kv_ref[c] = lax.dot_general(k2, v_ref[rows, :], (((0,), (0,)), ((), ())),
                                    preferred_element_type=F32)
        return carry

    lax.fori_loop(0, n_chunks, chunk_kv, 0, unroll=unroll)

    def scan_fwd(c, s):
        st_ref[c, :dk, :] = s.astype(BF16)
        return s * cdf + kv_ref[c, :dk, :]

    def scan_bwd(t, s):
        c = n_chunks - 1 - t
        st_ref[c, dk:, :] = s.astype(BF16)
        return s * cdb + kv_ref[c, dk:, :]

    zero_state = jnp.zeros((dk, RET_V_DIM), F32)
    lax.fori_loop(0, n_chunks, scan_fwd, zero_state, unroll=unroll)
    lax.fori_loop(0, n_chunks, scan_bwd, zero_state, unroll=unroll)

    def chunk_out(c, carry):
        rows = chunk(c)
        q = qr_ref[rows, :]
        v = v_ref[rows, :]
        sc = lax.dot_general(q.astype(BF16), kr_ref[rows, :].astype(BF16), (((1,), (1,)), ((), ())),
                             preferred_element_type=F32)
        o = jnp.dot((sc * dsum_ref[...]).astype(BF16), v, preferred_element_type=F32)
        q2 = (jnp.concatenate([q, q], axis=1) * qd_ref[...]).astype(BF16)
        o = o + jnp.dot(q2, st_ref[c], preferred_element_type=F32)
        o = o * lax.rsqrt(jnp.mean(o * o, axis=-1, keepdims=True) + EPS)
        o = o * gain_ref[...] * g_ref[rows, :].astype(F32)
        o_ref[rows, :] = o.astype(o_ref.dtype)
        return carry

    lax.fori_loop(0, n_chunks, chunk_out, 0, unroll=unroll)


def _retention(proj, ld, gain, cos_t, sin_t, batch, seq):
    n_chunks = seq // RET_CHUNK
    dk, dv, c_len = RET_QK_DIM, RET_V_DIM, RET_CHUNK
    assert n_chunks % min(RET_UNROLL, n_chunks) == 0
    qk_blk = lambda off: pl.BlockSpec((seq, dk), lambda b, h, off=off: (b, off // dk + h))
    v_blk = lambda off: pl.BlockSpec((seq, dv), lambda b, h, off=off: (b, off // dv + h))
    rope_blk = pl.BlockSpec((seq, dk), lambda b, h: (0, 0))
    return pl.pallas_call(
        functools.partial(_ret_kernel, n_chunks=n_chunks),
        grid=(batch, RET_HEADS),
        in_specs=[pl.BlockSpec(memory_space=pltpu.SMEM),
                  qk_blk(OFF_R_Q), qk_blk(OFF_R_K), v_blk(OFF_R_V), v_blk(OFF_R_G),
                  pl.BlockSpec((None, 1, dv), lambda b, h: (h, 0, 0)), rope_blk, rope_blk],
        out_specs=pl.BlockSpec((seq, dv), lambda b, h: (b, h)),
        out_shape=jax.ShapeDtypeStruct((batch * seq, RET_V_WIDTH), BF16),
        scratch_shapes=[pltpu.VMEM((seq, dk), F32),
                        pltpu.VMEM((seq, dk), F32),
                        pltpu.VMEM((n_chunks, 2 * dk, dv), F32),
                        pltpu.VMEM((n_chunks, 2 * dk, dv), BF16),
                        pltpu.VMEM((c_len, c_len), F32),
                        pltpu.VMEM((c_len, 2 * dk), F32),
                        pltpu.VMEM((c_len, 2 * dk), F32)],
        compiler_params=_params(("arbitrary", "arbitrary")),
        name="retention",
    )(ld, proj, proj, proj, proj, gain.astype(F32).reshape(RET_HEADS, 1, dv), cos_t, sin_t)


def _merge_kernel(a_ref, b_ref, wa_ref, wb_ref, ga_ref, gb_ref, o_ref):
    ya = jnp.dot(a_ref[...], wa_ref[...], preferred_element_type=F32).astype(BF16)
    yb = jnp.dot(b_ref[...], wb_ref[...], preferred_element_type=F32).astype(BF16)
    o_ref[...] = ga_ref[...] * ya + gb_ref[...] * yb


def _merge(a, b, wa, wb, proj):
    t = a.shape[0]
    tm = min(MAX_TILE_M, t)
    ja, jb = OFF_GATE_A // TILE_N, OFF_GATE_B // TILE_N
    return pl.pallas_call(
        _merge_kernel,
        grid=(t // tm, D_MODEL // TILE_N),
        in_specs=[pl.BlockSpec((tm, NA_WIDTH), lambda i, j: (i, 0)),
                  pl.BlockSpec((tm, RET_V_WIDTH), lambda i, j: (i, 0)),
                  pl.BlockSpec((NA_WIDTH, TILE_N), lambda i, j: (0, j)),
                  pl.BlockSpec((RET_V_WIDTH, TILE_N), lambda i, j: (0, j)),
                  pl.BlockSpec((tm, TILE_N), lambda i, j: (i, ja + j)),
                  pl.BlockSpec((tm, TILE_N), lambda i, j: (i, jb + j))],
        out_specs=pl.BlockSpec((tm, TILE_N), lambda i, j: (i, j)),
        out_shape=jax.ShapeDtypeStruct((t, D_MODEL), BF16),
        compiler_params=_params(("arbitrary", "arbitrary")),
        name="merge",
    )(a, b, wa, wb, proj, proj)


def _outproj_kernel(x_ref, w_ref, *refs, cast_scales, has_pre):
    n_cast = len(cast_scales)
    n_in = n_cast + (2 if has_pre else 0)
    cast_src, pre_in = refs[:n_cast], refs[n_cast:n_in]
    o_ref, cast_dst, pre_out = refs[n_in], refs[n_in + 1:n_in + 1 + n_cast], refs[n_in + 1 + n_cast:]
    _side_cast(cast_src, cast_dst, cast_scales)
    x = x_ref[...]
    for half in range(2):
        cols = slice(half * (TILE_N // 2), (half + 1) * (TILE_N // 2))
        if half == 1 and has_pre:
            xn = _rmsnorm_rows(*pre_in, *pre_out)
            never = pl.program_id(0) < 0
            x = jnp.concatenate([jnp.where(never, xn, x[:xn.shape[0]]), x[xn.shape[0]:]], axis=0)
        o_ref[:, cols] = jnp.dot(x, w_ref[:, cols], preferred_element_type=F32).astype(o_ref.dtype)


def _out_proj(merged, w, side_casts=(), pre=None):
    t = merged.shape[0]
    tm = min(MAX_TILE_M, t)
    grid = (t // tm, D_MODEL // TILE_N)
    side_in, side_out, side_shapes = _side_cast_specs(side_casts, *grid)
    pre_args, pre_in, pre_out, pre_shapes = (), [], [], []
    if pre is not None:
        px, pg = pre
        t_pre = px.shape[0]
        rows = _hosted_slab_rows(t_pre, grid[0] * grid[1])
        slab = lambda i, j: (jnp.minimum(i * grid[1] + j, t_pre // rows - 1), 0)
        pre_args = (px, pg.reshape(1, D_MODEL))
        pre_in = [pl.BlockSpec((rows, D_MODEL), slab), pl.BlockSpec((1, D_MODEL), lambda i, j: (0, 0))]
        pre_out = [pl.BlockSpec((rows, D_MODEL), slab)]
        pre_shapes = [jax.ShapeDtypeStruct((t_pre, D_MODEL), BF16)]
    outs = pl.pallas_call(
        functools.partial(_outproj_kernel, cast_scales=tuple(c[2] for c in side_casts), has_pre=pre is not None),
        grid=grid,
        in_specs=[pl.BlockSpec((tm, D_MODEL), lambda i, j: (i, 0)),
                  pl.BlockSpec((D_MODEL, TILE_N), lambda i, j: (0, j))] + side_in + pre_in,
        out_specs=[pl.BlockSpec((tm, TILE_N), lambda i, j: (i, j))] + side_out + pre_out,
        out_shape=[jax.ShapeDtypeStruct((t, D_MODEL), BF16)] + side_shapes + pre_shapes,
        compiler_params=_params(("arbitrary", "arbitrary")),
        name="out_proj",
    )(merged, w, *(c[0] for c in side_casts), *pre_args)
    n_cast = len(side_casts)
    return outs[0], list(outs[1:1 + n_cast]), (outs[1 + n_cast] if pre is not None else None)


def _post_norm(x, y, g):
    t = x.shape[0]
    row = pl.BlockSpec((ROW_TILE, D_MODEL), lambda i: (i, 0))
    return pl.pallas_call(
        _post_kernel,
        grid=(t // ROW_TILE,),
        in_specs=[row, row, pl.BlockSpec((1, D_MODEL), lambda i: (0, 0))],
        out_specs=[row, pl.BlockSpec((ROW_TILE, LANES), lambda i: (i, 0))],
        out_shape=[jax.ShapeDtypeStruct((t, D_MODEL), BF16), jax.ShapeDtypeStruct((t, LANES), F32)],
        compiler_params=_params(("arbitrary",)),
        name="post_norm",
    )(x, y, g.reshape(1, D_MODEL))


PLE_TILE_N = 512
PLE_WEIGHT_SCALE = 0.5


def _ple_kernel(xb_ref, wg_ref, p_ref, wp_ref, x_ref, y_ref, rs_ref, g_ref, *refs, cast_scales):
    cast_src, (o_ref,), cast_dst = _split_side_refs(refs, 1)
    _side_cast(cast_src, cast_dst, cast_scales)
    t = jnp.tanh(jnp.dot(xb_ref[...], wg_ref[...], preferred_element_type=F32))
    h = jnp.dot(p_ref[...].astype(BF16), wp_ref[...], preferred_element_type=F32)
    upd = h + h * t
    rs = rs_ref[...]
    for c in range(PLE_TILE_N // LANES):
        sl = slice(c * LANES, (c + 1) * LANES)
        x1 = x_ref[:, sl] + y_ref[:, sl].astype(F32) * rs * g_ref[:, sl]
        o_ref[:, sl] = x1 + upd[:, sl]


def _ple(x, y, rs, g, x1b, p, wg, wp, side_casts=()):
    t = x.shape[0]
    tm, tn = min(MAX_TILE_M, t), PLE_TILE_N
    grid = (t // tm, D_MODEL // tn)
    tile = pl.BlockSpec((tm, tn), lambda i, j: (i, j))
    side_in, side_out, side_shapes = _side_cast_specs(side_casts, *grid)
    return pl.pallas_call(
        functools.partial(_ple_kernel, cast_scales=tuple(c[2] for c in side_casts)),
        grid=grid,
        in_specs=[pl.BlockSpec((tm, D_MODEL), lambda i, j: (i, 0)),
                  pl.BlockSpec((D_MODEL, tn), lambda i, j: (0, j)),
                  pl.BlockSpec((tm, PLE_DIM), lambda i, j: (i, 0)),
                  pl.BlockSpec((PLE_DIM, tn), lambda i, j: (0, j)),
                  tile, tile,
                  pl.BlockSpec((tm, LANES), lambda i, j: (i, 0)),
                  pl.BlockSpec((1, tn), lambda i, j: (0, j))] + side_in,
        out_specs=[tile] + side_out,
        out_shape=[jax.ShapeDtypeStruct((t, D_MODEL), F32)] + side_shapes,
        compiler_params=_params(("arbitrary", "arbitrary")),
        name="ple",
    )(x1b, wg, p, wp, x, y, rs, g.reshape(1, D_MODEL), *(c[0] for c in side_casts))


def _rope_tables(seq):
    half = RET_QK_DIM // 2
    inv = ROPE_BASE ** (-jnp.arange(half, dtype=F32) / half)
    ang = jnp.arange(seq).astype(F32)[:, None] * inv[None, :]
    cos, sin = jnp.cos(ang), jnp.sin(ang)
    return jnp.concatenate([cos, cos], axis=-1), jnp.concatenate([-sin, sin], axis=-1)


SIDE_CAST_HOSTS = {"in_proj": ("w_in",), "out_proj": ("w_out",), "ple": ("w_ple_gate", "w_proj_a", "w_proj_b")}
FIRST_IN_PROJ_CASTS = ("w_proj_a", "w_proj_b", "w_out", "w_ple_gate")


def kernel(x_prompt, x_sample, p_prompt, p_sample, w_in, ln_pre, ln_post, na_rpb, ret_log_decay_fwd,
           ret_log_decay_bwd, ret_gn_gain, w_proj_a, w_proj_b, w_out, w_ple, w_ple_gate):
    depth = w_in.shape[0]
    stacked = {"w_in": w_in, "w_proj_a": w_proj_a, "w_proj_b": w_proj_b, "w_out": w_out,
               "w_ple": w_ple, "w_ple_gate": w_ple_gate}
    ps = [p_prompt, p_sample]
    shapes = [x.shape[:2] for x in (x_prompt, x_sample)]
    xs = [x.reshape(-1, D_MODEL) for x in (x_prompt, x_sample)]
    rope = {seq: _rope_tables(seq) for _, seq in shapes}
    small = [{"ln_pre": ln_pre[i].astype(F32),
              "ln_post": ln_post[i].astype(F32),
              "na_bias": {seq // GRID_W: _na_bias_table(na_rpb[i], seq // GRID_W) for _, seq in shapes},
              "ret_ld": jnp.stack([ret_log_decay_fwd[i], ret_log_decay_bwd[i]]).astype(F32),
              "ret_gn": ret_gn_gain[i]} for i in range(depth)]
    scale_of = lambda name: PLE_WEIGHT_SCALE if name in ("w_ple", "w_ple_gate") else 1.0
    wts = [dict() for _ in range(depth)]
    A, B = 0, 1

    def side(host, tr, i, hosting):
        (batch, seq), t = shapes[tr], shapes[tr][0] * shapes[tr][1]
        grid = {"in_proj": _in_proj_grid(t, seq)[1],
                "out_proj": (t // min(MAX_TILE_M, t), D_MODEL // TILE_N),
                "ple": (t // min(MAX_TILE_M, t), D_MODEL // PLE_TILE_N)}[host]
        wanted = [(n, i + 1) for n in SIDE_CAST_HOSTS[host]] if hosting else []
        if host == "in_proj" and tr == A and i == 0:
            wanted += [(n, 0) for n in FIRST_IN_PROJ_CASTS]
        keys = [(n, layer) for n, layer in wanted if _side_cast_fits(stacked[n], *grid)]
        return keys, tuple((stacked[n], layer, scale_of(n)) for n, layer in keys)

    def keep(keys, casts):
        for (name, layer), w in zip(keys, casts):
            wts[layer][name] = w

    def weight(i, name):
        if name not in wts[i]:
            wts[i][name] = _cast_layer_bf16(stacked[name], i, scale_of(name))
        return wts[i][name]

    def in_stage(tr, i, hosting, post, xn=None):
        keys, side_casts = side("in_proj", tr, i, hosting)
        if xn is None:
            xn = _rmsnorm_pre(xs[tr], small[i]["ln_pre"])
        proj, casts, post_out = _in_proj(xn, weight(i, "w_in"), shapes[tr][1], side_casts=side_casts, post=post)
        keep(keys, casts)
        return proj, post_out

    def mixer_stage(tr, i, proj, hosting, pre=None):
        batch, seq = shapes[tr]
        keys, side_casts = side("out_proj", tr, i, hosting)
        a = _na_attention(proj, small[i]["na_bias"][seq // GRID_W], batch, seq)
        b = _retention(proj, small[i]["ret_ld"], small[i]["ret_gn"], *rope[seq], batch, seq)
        merged = _merge(a, b, weight(i, "w_proj_a"), weight(i, "w_proj_b"), proj)
        y, casts, xn_other = _out_proj(merged, weight(i, "w_out"), side_casts=side_casts, pre=pre)
        keep(keys, casts)
        return y, xn_other

    def ple_stage(tr, i, y, x1b, rs, hosting):
        keys, side_casts = side("ple", tr, i, hosting)
        t = xs[tr].shape[0]
        xs[tr], *casts = _ple(xs[tr], y, rs, small[i]["ln_post"], x1b, ps[tr][i].reshape(t, PLE_DIM),
                              weight(i, "w_ple_gate"), weight(i, "w_ple"), side_casts=side_casts)
        keep(keys, casts)

    y_b = None
    for i in range(depth):
        hosting = i + 1 < depth
        post_b = None if y_b is None else (xs[B], y_b, small[i - 1]["ln_post"])
        proj_a, post_out_b = in_stage(A, i, hosting, post_b)
        if y_b is not None:
            ple_stage(B, i - 1, y_b, *post_out_b, False)
        y_a, xn_b = mixer_stage(A, i, proj_a, hosting, pre=(xs[B], small[i]["ln_pre"]))
        proj_b, post_out_a = in_stage(B, i, False, (xs[A], y_a, small[i]["ln_post"]), xn=xn_b)
        ple_stage(A, i, y_a, *post_out_a, hosting)
        y_b, _ = mixer_stage(B, i, proj_b, False)
    x1b, rs = _post_norm(xs[B], y_b, small[depth - 1]["ln_post"])
    ple_stage(B, depth - 1, y_b, x1b, rs, False)
    return tuple(x.reshape(batch, seq, D_MODEL) for x, (batch, seq) in zip(xs, shapes))
```

```python
import functools

import numpy as np
import jax
import jax.numpy as jnp
from jax import lax
from jax.experimental import pallas as pl
from jax.experimental.pallas import tpu as pltpu

D_MODEL = 4096
GRID_W = 64
PLE_DIM = 256
NA_HEADS = 16
NA_HEAD_DIM = 128
NA_WIDTH = NA_HEADS * NA_HEAD_DIM
NA_WIN_ROWS = 8
NA_WIN_COLS = 16
RET_HEADS = 8
RET_QK_DIM = 128
RET_V_DIM = 256
RET_QK_WIDTH = RET_HEADS * RET_QK_DIM
RET_V_WIDTH = RET_HEADS * RET_V_DIM
RET_CHUNK = 128
ROPE_BASE = 10000.0
EPS = 1e-6

OFF_NA_Q = 0
OFF_NA_K = OFF_NA_Q + NA_WIDTH
OFF_NA_V = OFF_NA_K + NA_WIDTH
OFF_NA_G = OFF_NA_V + NA_WIDTH
OFF_R_Q = OFF_NA_G + NA_WIDTH
OFF_R_K = OFF_R_Q + RET_QK_WIDTH
OFF_R_V = OFF_R_K + RET_QK_WIDTH
OFF_R_G = OFF_R_V + RET_V_WIDTH
OFF_GATE_A = OFF_R_G + RET_V_WIDTH
OFF_GATE_B = OFF_GATE_A + D_MODEL
IN_WIDTH = OFF_GATE_B + D_MODEL

TILE_N = 1024
MAX_TILE_M = 1024
ROW_TILE = 256
V7X_VMEM_LIMIT_BYTES = 56 * 1024 * 1024
LANES = 128

F32 = jnp.float32
BF16 = jnp.bfloat16


def _params(semantics, vmem=V7X_VMEM_LIMIT_BYTES):
    return pltpu.CompilerParams(dimension_semantics=semantics, vmem_limit_bytes=vmem)


CAST_BLOCK_BYTES = 8 * 1024 * 1024


def _scaled_cast(w, scale, dtype):
    return (w if scale == 1.0 else w * scale).astype(dtype)


def _cast_kernel(w_ref, o_ref, *, scale):
    o_ref[...] = _scaled_cast(w_ref[...], scale, o_ref.dtype)


def _cast_layer_bf16(w, layer, scale=1.0):
    _, k, n = w.shape
    rb = max(r for r in range(16, k + 1, 16) if k % r == 0 and (r * n * 4 <= CAST_BLOCK_BYTES or r == 16))
    return pl.pallas_call(
        functools.partial(_cast_kernel, scale=scale),
        grid=(k // rb,),
        in_specs=[pl.BlockSpec((None, rb, n), lambda i: (layer, i, 0))],
        out_specs=pl.BlockSpec((rb, n), lambda i: (i, 0)),
        out_shape=jax.ShapeDtypeStruct((k, n), BF16),
        compiler_params=_params(("arbitrary",)),
        name="cast_bf16",
    )(w)


SUBLANES_BF16 = 16


SIDE_CAST_MAX_BYTES = 1024 * 1024


def _side_cast_grid_slab_fits(w, ni, nj):
    _, k, n = w.shape
    return (k % ni == 0 and n % nj == 0 and (k // ni) % SUBLANES_BF16 == 0 and (n // nj) % LANES == 0
            and (k // ni) * (n // nj) * 4 <= SIDE_CAST_MAX_BYTES)


def _side_cast_row_slab(w, n_steps):
    _, k, n = w.shape
    rows = [r for r in range(SUBLANES_BF16, k + 1, SUBLANES_BF16) if k % r == 0 and k // r <= n_steps]
    return min(rows) if rows and min(rows) * n * 4 <= SIDE_CAST_MAX_BYTES else None


def _side_cast_fits(w, ni, nj):
    return _side_cast_grid_slab_fits(w, ni, nj) or _side_cast_row_slab(w, ni * nj) is not None


def _side_cast_specs(side_casts, ni, nj):
    in_specs, out_specs, out_shapes = [], [], []
    for w, layer, _ in side_casts:
        _, k, n = w.shape
        if _side_cast_grid_slab_fits(w, ni, nj):
            blk = (k // ni, n // nj)
            in_specs.append(pl.BlockSpec((None,) + blk, lambda i, j, layer=layer: (layer, i, j)))
            out_specs.append(pl.BlockSpec(blk, lambda i, j: (i, j)))
        else:
            rows = _side_cast_row_slab(w, ni * nj)
            last = k // rows - 1
            in_specs.append(pl.BlockSpec(
                (None, rows, n), lambda i, j, layer=layer, last=last: (layer, jnp.minimum(i * nj + j, last), 0)))
            out_specs.append(pl.BlockSpec((rows, n), lambda i, j, last=last: (jnp.minimum(i * nj + j, last), 0)))
        out_shapes.append(jax.ShapeDtypeStruct((k, n), BF16))
    return in_specs, out_specs, out_shapes


def _split_side_refs(refs, n_out):
    n = (len(refs) - n_out) // 2
    return refs[:n], refs[n:n + n_out], refs[n + n_out:]


def _side_cast(src_refs, dst_refs, scales):
    for src, dst, scale in zip(src_refs, dst_refs, scales):
        dst[...] = _scaled_cast(src[...], scale, dst.dtype)


def _rmsnorm_rows(x_ref, g_ref, o_ref):
    x = x_ref[...]
    ms = jnp.mean(x * x, axis=-1, keepdims=True)
    xn = (x * lax.rsqrt(ms + EPS) * g_ref[...]).astype(o_ref.dtype)
    o_ref[...] = xn
    return xn


def _rmsnorm_kernel(x_ref, g_ref, o_ref):
    _rmsnorm_rows(x_ref, g_ref, o_ref)


def _rmsnorm_pre(x, g):
    t = x.shape[0]
    return pl.pallas_call(
        _rmsnorm_kernel,
        grid=(t // ROW_TILE,),
        in_specs=[pl.BlockSpec((ROW_TILE, D_MODEL), lambda i: (i, 0)),
                  pl.BlockSpec((1, D_MODEL), lambda i: (0, 0))],
        out_specs=pl.BlockSpec((ROW_TILE, D_MODEL), lambda i: (i, 0)),
        out_shape=jax.ShapeDtypeStruct((t, D_MODEL), BF16),
        compiler_params=_params(("arbitrary",)),
        name="rmsnorm_pre",
    )(x, g.reshape(1, D_MODEL))


_J_NA_K = OFF_NA_K // TILE_N
_J_NA_G = OFF_NA_G // TILE_N
_J_R_Q = OFF_R_Q // TILE_N
_J_R_K = OFF_R_K // TILE_N
_J_R_G = OFF_R_G // TILE_N
_J_GATE = OFF_GATE_A // TILE_N
assert NA_HEAD_DIM == RET_QK_DIM
QK_SCALE = NA_HEAD_DIM ** -0.5
LOG2E = float(np.log2(np.e))
NA_Q_SCALE = QK_SCALE * LOG2E


def _post_norm_rows(x_ref, y_ref, g_ref, ob_ref, rs_ref):
    y = y_ref[...].astype(F32)
    rs = lax.rsqrt(jnp.mean(y * y, axis=-1, keepdims=True) + EPS)
    x1b = (x_ref[...] + y * rs * g_ref[...]).astype(ob_ref.dtype)
    ob_ref[...] = x1b
    rs_ref[...] = jnp.broadcast_to(rs, rs_ref.shape)
    return x1b


def _post_kernel(x_ref, y_ref, g_ref, ob_ref, rs_ref):
    _post_norm_rows(x_ref, y_ref, g_ref, ob_ref, rs_ref)


def _hosted_slab_rows(t_post, n_steps):
    return min(r for r in range(SUBLANES_BF16, t_post + 1, SUBLANES_BF16)
               if t_post % r == 0 and t_post // r <= n_steps)


def _inproj_kernel(x_ref, w_ref, *refs, cast_scales, has_post):
    n_cast = len(cast_scales)
    n_in = n_cast + (3 if has_post else 0)
    cast_src, post_in = refs[:n_cast], refs[n_cast:n_in]
    o_ref, cast_dst, post_out = refs[n_in], refs[n_in + 1:n_in + 1 + n_cast], refs[n_in + 1 + n_cast:]
    j = pl.program_id(1)
    is_gate = j >= _J_GATE
    is_silu = ((j >= _J_NA_G) & (j < _J_R_Q)) | ((j >= _J_R_G) & (j < _J_GATE))
    scale = jnp.where(j < _J_NA_K, NA_Q_SCALE, jnp.where(j == _J_R_K, QK_SCALE, 1.0))

    def step(epilogue):
        _side_cast(cast_src, cast_dst, cast_scales)
        x = x_ref[...]
        for half in range(2):
            cols = slice(half * (TILE_N // 2), (half + 1) * (TILE_N // 2))
            if half == 1 and has_post:
                x1b = _post_norm_rows(*post_in, *post_out)
                never = pl.program_id(0) < 0
                x = jnp.concatenate([jnp.where(never, x1b, x[:x1b.shape[0]]), x[x1b.shape[0]:]], axis=0)
            y = jnp.dot(x, w_ref[:, cols], preferred_element_type=F32)
            o_ref[:, cols] = epilogue(y).astype(o_ref.dtype)

    pl.when(is_gate)(lambda: step(lambda y: 0.5 * jnp.tanh(0.5 * y) + 0.5))
    pl.when(is_silu)(lambda: step(lambda y: y * (0.5 * jnp.tanh(0.5 * y) + 0.5)))
    pl.when(jnp.logical_not(is_gate | is_silu))(lambda: step(lambda y: y * scale))


def _in_proj_grid(t, seq):
    tm = min(MAX_TILE_M, seq)
    return tm, (t // tm, IN_WIDTH // TILE_N)


def _in_proj(xn, w, seq, side_casts=(), post=None):
    t = xn.shape[0]
    tm, grid = _in_proj_grid(t, seq)
    tn = TILE_N
    side_in, side_out, side_shapes = _side_cast_specs(side_casts, *grid)
    post_args, post_in, post_out, post_shapes = (), [], [], []
    if post is not None:
        px, py, pg = post
        t_post = px.shape[0]
        rows = _hosted_slab_rows(t_post, grid[0] * grid[1])
        slab = lambda i, j: (jnp.minimum(i * grid[1] + j, t_post // rows - 1), 0)
        post_args = (px, py, pg.reshape(1, D_MODEL))
        post_in = [pl.BlockSpec((rows, D_MODEL), slab), pl.BlockSpec((rows, D_MODEL), slab),
                   pl.BlockSpec((1, D_MODEL), lambda i, j: (0, 0))]
        post_out = [pl.BlockSpec((rows, D_MODEL), slab), pl.BlockSpec((rows, LANES), slab)]
        post_shapes = [jax.ShapeDtypeStruct((t_post, D_MODEL), BF16), jax.ShapeDtypeStruct((t_post, LANES), F32)]
    outs = pl.pallas_call(
        functools.partial(_inproj_kernel, cast_scales=tuple(c[2] for c in side_casts), has_post=post is not None),
        grid=grid,
        in_specs=[pl.BlockSpec((tm, D_MODEL), lambda i, j: (i, 0)),
                  pl.BlockSpec((D_MODEL, tn), lambda i, j: (0, j))] + side_in + post_in,
        out_specs=[pl.BlockSpec((tm, tn), lambda i, j: (i, j))] + side_out + post_out,
        out_shape=[jax.ShapeDtypeStruct((t, IN_WIDTH), BF16)] + side_shapes + post_shapes,
        compiler_params=_params(("arbitrary", "arbitrary")),
        name="in_proj",
    )(xn, w, *(c[0] for c in side_casts), *post_args)
    n_cast = len(side_casts)
    return outs[0], list(outs[1:1 + n_cast]), (tuple(outs[1 + n_cast:]) if post is not None else None)


NA_PAIR_ROWS = 2
NA_PAIR_TOKENS = NA_PAIR_ROWS * GRID_W
NA_KEY_ROWS = NA_WIN_ROWS + NA_PAIR_ROWS
NA_KEY_CHUNKS = NA_KEY_ROWS // NA_PAIR_ROWS
NA_KEYS = NA_KEY_ROWS * GRID_W
NA_VARIANTS = 5
NA_QUAD_CHUNKS = NA_KEY_CHUNKS + 1
NA_SUM_ROWS = 16


def _na_kernel(q_ref, k_ref, v_ref, g_ref, bias_ref, o_ref, vt_ref, s_ref, p_ref, *, rows):
    n_pairs = rows // NA_PAIR_ROWS
    n_chunks = n_pairs
    ct = NA_PAIR_TOKENS
    nt = (((1,), (1,)), ((), ()))
    n_quads = n_pairs // 2

    for c in range(n_chunks):
        vt_ref[c] = v_ref[c * ct:(c + 1) * ct, :].astype(F32).T.astype(BF16)
    ones_rows = jnp.ones((NA_SUM_ROWS, NA_QUAD_CHUNKS * ct), BF16)
    zero_block = jnp.zeros((ct, ct), BF16)

    def pair_window(pr):
        c0 = min(max(pr - NA_WIN_ROWS // 4, 0), n_chunks - NA_KEY_CHUNKS)
        var = pr if pr < 2 else (pr - (n_pairs - 2) + 3 if pr >= n_pairs - 2 else 2)
        return c0, var

    quad_chunk0 = [min(max(2 * qd - NA_WIN_ROWS // 4, 0), n_chunks - NA_QUAD_CHUNKS) for qd in range(n_quads)]
    for qd in range(n_quads):
        k0, q0 = quad_chunk0[qd] * ct, 2 * qd * ct
        s_ref[qd] = lax.dot_general(k_ref[k0:k0 + NA_QUAD_CHUNKS * ct, :], q_ref[q0:q0 + 2 * ct, :], nt,
                                    preferred_element_type=F32)
    for qd in range(n_quads):
        for half in range(2):
            c0, var = pair_window(2 * qd + half)
            off = (c0 - quad_chunk0[qd]) * ct
            lanes = slice(half * ct, (half + 1) * ct)
            s = s_ref[qd, off:off + NA_KEYS, lanes] + bias_ref[var]
            m = jnp.max(s, axis=0, keepdims=True)
            p_ref[qd, off:off + NA_KEYS, lanes] = jnp.exp2(s - m).astype(BF16)
            rest = 0 if off else NA_KEYS
            p_ref[qd, rest:rest + ct, lanes] = zero_block
    for qd in range(n_quads):
        q0 = 2 * qd * ct
        vt = jnp.concatenate([vt_ref[quad_chunk0[qd] + c] for c in range(NA_QUAD_CHUNKS)], axis=1)
        ot = jnp.dot(jnp.concatenate([vt, ones_rows], axis=0), p_ref[qd], preferred_element_type=F32)
        ot = ot[:NA_HEAD_DIM] * (1.0 / ot[NA_HEAD_DIM:NA_HEAD_DIM + 1])
        for half in range(2):
            rows_q = slice(q0 + half * ct, q0 + (half + 1) * ct)
            o = ot[:, half * ct:(half + 1) * ct].T * g_ref[rows_q, :].astype(F32)
            o_ref[rows_q, :] = o.astype(o_ref.dtype)


def _na_bias_table(rpb, rows):
    n_pairs = rows // NA_PAIR_ROWS
    cols = np.arange(GRID_W)
    col_start = np.clip(cols - NA_WIN_COLS // 2, 0, GRID_W - NA_WIN_COLS)
    col_valid = (cols[None, :] >= col_start[:, None]) & (cols[None, :] < col_start[:, None] + NA_WIN_COLS)
    pad = GRID_W - NA_WIN_COLS
    rp = jnp.pad(rpb.astype(F32), ((0, 0), (0, 0), (pad, pad)))
    toep = jnp.stack([rp[:, :, GRID_W - 1 - qc: 2 * GRID_W - 1 - qc] for qc in range(GRID_W)], axis=2)
    toep = jnp.where(col_valid[None, None], toep, -jnp.inf)
    toep_t = toep.transpose(0, 1, 3, 2)
    masked = jnp.full((rpb.shape[0], GRID_W, GRID_W), -jnp.inf, F32)
    tables = []
    for pr in (0, 1, 2, n_pairs - 2, n_pairs - 1):
        base = int(np.clip(NA_PAIR_ROWS * pr - NA_WIN_ROWS // 2, 0, rows - NA_KEY_ROWS))
        slabs = []
        for w in range(NA_KEY_ROWS):
            per_row = []
            for i in range(NA_PAIR_ROWS):
                r_q = NA_PAIR_ROWS * pr + i
                rs = int(np.clip(r_q - NA_WIN_ROWS // 2, 0, rows - NA_WIN_ROWS))
                kr = base + w
                if rs <= kr < rs + NA_WIN_ROWS:
                    per_row.append(toep_t[:, kr - r_q + NA_WIN_ROWS - 1])
                else:
                    per_row.append(masked)
            slabs.append(jnp.concatenate(per_row, axis=-1))
        tables.append(jnp.concatenate(slabs, axis=1))
    return jnp.stack(tables, axis=1) * LOG2E


def _na_attention(proj, bias, layer, batch, seq):
    rows = seq // GRID_W
    n_pairs = rows // NA_PAIR_ROWS
    n_quads = n_pairs // 2
    assert rows % (2 * NA_PAIR_ROWS) == 0 and rows >= NA_KEY_ROWS + 2 * NA_PAIR_ROWS
    hd = NA_HEAD_DIM
    blk = lambda off: pl.BlockSpec((seq, hd), lambda b, h, off=off: (b, off // hd + h))
    return pl.pallas_call(
        functools.partial(_na_kernel, rows=rows),
        grid=(batch, NA_HEADS),
        in_specs=[blk(OFF_NA_Q), blk(OFF_NA_K), blk(OFF_NA_V), blk(OFF_NA_G),
                  pl.BlockSpec((None, NA_VARIANTS, NA_KEYS, NA_PAIR_TOKENS),
                               lambda b, h: (layer * NA_HEADS + h, 0, 0, 0))],
        out_specs=pl.BlockSpec((seq, hd), lambda b, h: (b, h)),
        out_shape=jax.ShapeDtypeStruct((batch * seq, NA_WIDTH), BF16),
        scratch_shapes=[pltpu.VMEM((n_pairs, hd, NA_PAIR_TOKENS), BF16),
                        pltpu.VMEM((n_quads, NA_QUAD_CHUNKS * NA_PAIR_TOKENS, 2 * NA_PAIR_TOKENS), F32),
                        pltpu.VMEM((n_quads, NA_QUAD_CHUNKS * NA_PAIR_TOKENS, 2 * NA_PAIR_TOKENS), BF16)],
        compiler_params=_params(("arbitrary", "arbitrary")),
        name="na_attention",
    )(proj, proj, proj, proj, bias)


RET_UNROLL = 32


def _ret_kernel(ld_ref, q_ref, k_ref, v_ref, g_ref, gain_ref, cos_ref, sin_ref, o_ref,
                qr_ref, kr_ref, kv_ref, st_ref, dsum_ref, qd_ref, kd_ref, *, n_chunks):
    c_len, dk = RET_CHUNK, RET_QK_DIM
    unroll = min(RET_UNROLL, n_chunks)
    h = pl.program_id(1)
    ldf = -jnp.abs(ld_ref[0, h])
    ldb = -jnp.abs(ld_ref[1, h])
    row = lax.broadcasted_iota(jnp.int32, (c_len, c_len), 0).astype(F32)
    col = lax.broadcasted_iota(jnp.int32, (c_len, c_len), 1).astype(F32)
    diff = row - col
    dsum_ref[...] = (jnp.where(diff >= 0, jnp.exp(ldf * jnp.maximum(diff, 0.0)), 0.0)
                     + jnp.where(diff <= 0, jnp.exp(ldb * jnp.maximum(-diff, 0.0)), 0.0))
    qd_ref[:, :dk] = jnp.exp(ldf * (row + 1.0))
    qd_ref[:, dk:] = jnp.exp(ldb * (c_len - row))
    kd_ref[:, :dk] = jnp.exp(ldf * (c_len - 1.0 - row))
    kd_ref[:, dk:] = jnp.exp(ldb * row)
    zero_row = jnp.zeros((1, RET_V_DIM), F32)
    cdf = jnp.exp(zero_row + ldf * c_len)
    cdb = jnp.exp(zero_row + ldb * c_len)
    chunk = lambda c: pl.ds(pl.multiple_of(c * c_len, c_len), c_len)

    def rotate(c, carry):
        rows = chunk(c)
        cos, sin = cos_ref[rows, :], sin_ref[rows, :]
        for src, dst in ((q_ref, qr_ref), (k_ref, kr_ref)):
            t = src[rows, :].astype(F32)
            dst[rows, :] = t * cos + pltpu.roll(t, dk // 2, axis=1) * sin
        return carry

    lax.fori_loop(0, n_chunks, rotate, 0, unroll=unroll)

    def chunk_kv(c, carry):
        rows = chunk(c)
        k = kr_ref[rows, :]
        k2 = (jnp.concatenate([k, k], axis=1) * kd_ref[...]).astype(BF16)
        kv_ref[c] = lax.dot_general(k2, v_ref[rows, :], (((0,), (0,)), ((), ())),
                                    preferred_element_type=F32)
        return carry

    lax.fori_loop(0, n_chunks, chunk_kv, 0, unroll=unroll)

    def scan_fwd(c, s):
        st_ref[c, :dk, :] = s.astype(BF16)
        return s * cdf + kv_ref[c, :dk, :]

    def scan_bwd(t, s):
        c = n_chunks - 1 - t
        st_ref[c, dk:, :] = s.astype(BF16)
        return s * cdb + kv_ref[c, dk:, :]

    zero_state = jnp.zeros((dk, RET_V_DIM), F32)
    lax.fori_loop(0, n_chunks, scan_fwd, zero_state, unroll=unroll)
    lax.fori_loop(0, n_chunks, scan_bwd, zero_state, unroll=unroll)

    def chunk_out(c, carry):
        rows = chunk(c)
        q = qr_ref[rows, :]
        v = v_ref[rows, :]
        sc = lax.dot_general(q.astype(BF16), kr_ref[rows, :].astype(BF16), (((1,), (1,)), ((), ())),
                             preferred_element_type=F32)
        o = jnp.dot((sc * dsum_ref[...]).astype(BF16), v, preferred_element_type=F32)
        q2 = (jnp.concatenate([q, q], axis=1) * qd_ref[...]).astype(BF16)
        o = o + jnp.dot(q2, st_ref[c], preferred_element_type=F32)
        o = o * lax.rsqrt(jnp.mean(o * o, axis=-1, keepdims=True) + EPS)
        o = o * gain_ref[...] * g_ref[rows, :].astype(F32)
        o_ref[rows, :] = o.astype(o_ref.dtype)
        return carry

    lax.fori_loop(0, n_chunks, chunk_out, 0, unroll=unroll)


def _retention(proj, ld, gain, cos_t, sin_t, batch, seq):
    n_chunks = seq // RET_CHUNK
    dk, dv, c_len = RET_QK_DIM, RET_V_DIM, RET_CHUNK
    assert n_chunks % min(RET_UNROLL, n_chunks) == 0
    qk_blk = lambda off: pl.BlockSpec((seq, dk), lambda b, h, off=off: (b, off // dk + h))
    v_blk = lambda off: pl.BlockSpec((seq, dv), lambda b, h, off=off: (b, off // dv + h))
    rope_blk = pl.BlockSpec((seq, dk), lambda b, h: (0, 0))
    return pl.pallas_call(
        functools.partial(_ret_kernel, n_chunks=n_chunks),
        grid=(batch, RET_HEADS),
        in_specs=[pl.BlockSpec(memory_space=pltpu.SMEM),
                  qk_blk(OFF_R_Q), qk_blk(OFF_R_K), v_blk(OFF_R_V), v_blk(OFF_R_G),
                  pl.BlockSpec((None, 1, dv), lambda b, h: (h, 0, 0)), rope_blk, rope_blk],
        out_specs=pl.BlockSpec((seq, dv), lambda b, h: (b, h)),
        out_shape=jax.ShapeDtypeStruct((batch * seq, RET_V_WIDTH), BF16),
        scratch_shapes=[pltpu.VMEM((seq, dk), F32),
                        pltpu.VMEM((seq, dk), F32),
                        pltpu.VMEM((n_chunks, 2 * dk, dv), F32),
                        pltpu.VMEM((n_chunks, 2 * dk, dv), BF16),
                        pltpu.VMEM((c_len, c_len), F32),
                        pltpu.VMEM((c_len, 2 * dk), F32),
                        pltpu.VMEM((c_len, 2 * dk), F32)],
        compiler_params=_params(("arbitrary", "arbitrary")),
        name="retention",
    )(ld, proj, proj, proj, proj, gain.astype(F32).reshape(RET_HEADS, 1, dv), cos_t, sin_t)


def _merge_kernel(a_ref, b_ref, wa_ref, wb_ref, ga_ref, gb_ref, o_ref):
    ya = jnp.dot(a_ref[...], wa_ref[...], preferred_element_type=F32).astype(BF16)
    yb = jnp.dot(b_ref[...], wb_ref[...], preferred_element_type=F32).astype(BF16)
    o_ref[...] = ga_ref[...] * ya + gb_ref[...] * yb


def _merge(a, b, wa, wb, proj):
    t = a.shape[0]
    tm = min(MAX_TILE_M, t)
    ja, jb = OFF_GATE_A // TILE_N, OFF_GATE_B // TILE_N
    return pl.pallas_call(
        _merge_kernel,
        grid=(t // tm, D_MODEL // TILE_N),
        in_specs=[pl.BlockSpec((tm, NA_WIDTH), lambda i, j: (i, 0)),
                  pl.BlockSpec((tm, RET_V_WIDTH), lambda i, j: (i, 0)),
                  pl.BlockSpec((NA_WIDTH, TILE_N), lambda i, j: (0, j)),
                  pl.BlockSpec((RET_V_WIDTH, TILE_N), lambda i, j: (0, j)),
                  pl.BlockSpec((tm, TILE_N), lambda i, j: (i, ja + j)),
                  pl.BlockSpec((tm, TILE_N), lambda i, j: (i, jb + j))],
        out_specs=pl.BlockSpec((tm, TILE_N), lambda i, j: (i, j)),
        out_shape=jax.ShapeDtypeStruct((t, D_MODEL), BF16),
        compiler_params=_params(("arbitrary", "arbitrary")),
        name="merge",
    )(a, b, wa, wb, proj, proj)


def _outproj_kernel(x_ref, w_ref, *refs, cast_scales, has_pre):
    n_cast = len(cast_scales)
    n_in = n_cast + (2 if has_pre else 0)
    cast_src, pre_in = refs[:n_cast], refs[n_cast:n_in]
    o_ref, cast_dst, pre_out = refs[n_in], refs[n_in + 1:n_in + 1 + n_cast], refs[n_in + 1 + n_cast:]
    _side_cast(cast_src, cast_dst, cast_scales)
    x = x_ref[...]
    for half in range(2):
        cols = slice(half * (TILE_N // 2), (half + 1) * (TILE_N // 2))
        if half == 1 and has_pre:
            xn = _rmsnorm_rows(*pre_in, *pre_out)
            never = pl.program_id(0) < 0
            x = jnp.concatenate([jnp.where(never, xn, x[:xn.shape[0]]), x[xn.shape[0]:]], axis=0)
        o_ref[:, cols] = jnp.dot(x, w_ref[:, cols], preferred_element_type=F32).astype(o_ref.dtype)


def _out_proj(merged, w, side_casts=(), pre=None):
    t = merged.shape[0]
    tm = min(MAX_TILE_M, t)
    grid = (t // tm, D_MODEL // TILE_N)
    side_in, side_out, side_shapes = _side_cast_specs(side_casts, *grid)
    pre_args, pre_in, pre_out, pre_shapes = (), [], [], []
    if pre is not None:
        px, pg = pre
        t_pre = px.shape[0]
        rows = _hosted_slab_rows(t_pre, grid[0] * grid[1])
        slab = lambda i, j: (jnp.minimum(i * grid[1] + j, t_pre // rows - 1), 0)
        pre_args = (px, pg.reshape(1, D_MODEL))
        pre_in = [pl.BlockSpec((rows, D_MODEL), slab), pl.BlockSpec((1, D_MODEL), lambda i, j: (0, 0))]
        pre_out = [pl.BlockSpec((rows, D_MODEL), slab)]
        pre_shapes = [jax.ShapeDtypeStruct((t_pre, D_MODEL), BF16)]
    outs = pl.pallas_call(
        functools.partial(_outproj_kernel, cast_scales=tuple(c[2] for c in side_casts), has_pre=pre is not None),
        grid=grid,
        in_specs=[pl.BlockSpec((tm, D_MODEL), lambda i, j: (i, 0)),
                  pl.BlockSpec((D_MODEL, TILE_N), lambda i, j: (0, j))] + side_in + pre_in,
        out_specs=[pl.BlockSpec((tm, TILE_N), lambda i, j: (i, j))] + side_out + pre_out,
        out_shape=[jax.ShapeDtypeStruct((t, D_MODEL), BF16)] + side_shapes + pre_shapes,
        compiler_params=_params(("arbitrary", "arbitrary")),
        name="out_proj",
    )(merged, w, *(c[0] for c in side_casts), *pre_args)
    n_cast = len(side_casts)
    return outs[0], list(outs[1:1 + n_cast]), (outs[1 + n_cast] if pre is not None else None)


def _post_norm(x, y, g):
    t = x.shape[0]
    row = pl.BlockSpec((ROW_TILE, D_MODEL), lambda i: (i, 0))
    return pl.pallas_call(
        _post_kernel,
        grid=(t // ROW_TILE,),
        in_specs=[row, row, pl.BlockSpec((1, D_MODEL), lambda i: (0, 0))],
        out_specs=[row, pl.BlockSpec((ROW_TILE, LANES), lambda i: (i, 0))],
        out_shape=[jax.ShapeDtypeStruct((t, D_MODEL), BF16), jax.ShapeDtypeStruct((t, LANES), F32)],
        compiler_params=_params(("arbitrary",)),
        name="post_norm",
    )(x, y, g.reshape(1, D_MODEL))


PLE_TILE_N = 512
PLE_WEIGHT_SCALE = 0.5


def _ple_kernel(xb_ref, wg_ref, p_ref, wp_ref, x_ref, y_ref, rs_ref, g_ref, *refs, cast_scales):
    cast_src, (o_ref,), cast_dst = _split_side_refs(refs, 1)
    _side_cast(cast_src, cast_dst, cast_scales)
    t = jnp.tanh(jnp.dot(xb_ref[...], wg_ref[...], preferred_element_type=F32))
    h = jnp.dot(p_ref[...].astype(BF16), wp_ref[...], preferred_element_type=F32)
    upd = h + h * t
    rs = rs_ref[...]
    for c in range(PLE_TILE_N // LANES):
        sl = slice(c * LANES, (c + 1) * LANES)
        x1 = x_ref[:, sl] + y_ref[:, sl].astype(F32) * rs * g_ref[:, sl]
        o_ref[:, sl] = x1 + upd[:, sl]


def _ple(x, y, rs, g, x1b, p, wg, wp, side_casts=()):
    t = x.shape[0]
    tm, tn = min(MAX_TILE_M, t), PLE_TILE_N
    grid = (t // tm, D_MODEL // tn)
    tile = pl.BlockSpec((tm, tn), lambda i, j: (i, j))
    side_in, side_out, side_shapes = _side_cast_specs(side_casts, *grid)
    return pl.pallas_call(
        functools.partial(_ple_kernel, cast_scales=tuple(c[2] for c in side_casts)),
        grid=grid,
        in_specs=[pl.BlockSpec((tm, D_MODEL), lambda i, j: (i, 0)),
                  pl.BlockSpec((D_MODEL, tn), lambda i, j: (0, j)),
                  pl.BlockSpec((tm, PLE_DIM), lambda i, j: (i, 0)),
                  pl.BlockSpec((PLE_DIM, tn), lambda i, j: (0, j)),
                  tile, tile,
                  pl.BlockSpec((tm, LANES), lambda i, j: (i, 0)),
                  pl.BlockSpec((1, tn), lambda i, j: (0, j))] + side_in,
        out_specs=[tile] + side_out,
        out_shape=[jax.ShapeDtypeStruct((t, D_MODEL), F32)] + side_shapes,
        compiler_params=_params(("arbitrary", "arbitrary")),
        name="ple",
    )(x1b, wg, p, wp, x, y, rs, g.reshape(1, D_MODEL), *(c[0] for c in side_casts))


def _rope_tables(seq):
    half = RET_QK_DIM // 2
    inv = ROPE_BASE ** (-jnp.arange(half, dtype=F32) / half)
    ang = jnp.arange(seq).astype(F32)[:, None] * inv[None, :]
    cos, sin = jnp.cos(ang), jnp.sin(ang)
    return jnp.concatenate([cos, cos], axis=-1), jnp.concatenate([-sin, sin], axis=-1)


SIDE_CAST_HOSTS = {"in_proj": ("w_in",), "out_proj": ("w_out",), "ple": ("w_ple_gate", "w_proj_a", "w_proj_b")}
FIRST_IN_PROJ_CASTS = ("w_proj_a", "w_proj_b", "w_out", "w_ple_gate")


def kernel(x_prompt, x_sample, p_prompt, p_sample, w_in, ln_pre, ln_post, na_rpb, ret_log_decay_fwd,
           ret_log_decay_bwd, ret_gn_gain, w_proj_a, w_proj_b, w_out, w_ple, w_ple_gate):
    depth = w_in.shape[0]
    stacked = {"w_in": w_in, "w_proj_a": w_proj_a, "w_proj_b": w_proj_b, "w_out": w_out,
               "w_ple": w_ple, "w_ple_gate": w_ple_gate}
    ps = [p_prompt, p_sample]
    shapes = [x.shape[:2] for x in (x_prompt, x_sample)]
    xs = [x.reshape(-1, D_MODEL) for x in (x_prompt, x_sample)]
    rope = {seq: _rope_tables(seq) for _, seq in shapes}
    rpb_all = na_rpb.reshape(depth * NA_HEADS, *na_rpb.shape[2:])
    na_bias = {seq // GRID_W: _na_bias_table(rpb_all, seq // GRID_W) for _, seq in shapes}
    small = [{"ln_pre": ln_pre[i].astype(F32),
              "ln_post": ln_post[i].astype(F32),
              "ret_ld": jnp.stack([ret_log_decay_fwd[i], ret_log_decay_bwd[i]]).astype(F32),
              "ret_gn": ret_gn_gain[i]} for i in range(depth)]
    scale_of = lambda name: PLE_WEIGHT_SCALE if name in ("w_ple", "w_ple_gate") else 1.0
    wts = [dict() for _ in range(depth)]
    A, B = 0, 1

    def side(host, tr, i, hosting):
        (batch, seq), t = shapes[tr], shapes[tr][0] * shapes[tr][1]
        grid = {"in_proj": _in_proj_grid(t, seq)[1],
                "out_proj": (t // min(MAX_TILE_M, t), D_MODEL // TILE_N),
                "ple": (t // min(MAX_TILE_M, t), D_MODEL // PLE_TILE_N)}[host]
        wanted = [(n, i + 1) for n in SIDE_CAST_HOSTS[host]] if hosting else []
        if host == "in_proj" and tr == A and i == 0:
            wanted += [(n, 0) for n in FIRST_IN_PROJ_CASTS]
        keys = [(n, layer) for n, layer in wanted if _side_cast_fits(stacked[n], *grid)]
        return keys, tuple((stacked[n], layer, scale_of(n)) for n, layer in keys)

    def keep(keys, casts):
        for (name, layer), w in zip(keys, casts):
            wts[layer][name] = w

    def weight(i, name):
        if name not in wts[i]:
            wts[i][name] = _cast_layer_bf16(stacked[name], i, scale_of(name))
        return wts[i][name]

    def in_stage(tr, i, hosting, post, xn=None):
        keys, side_casts = side("in_proj", tr, i, hosting)
        if xn is None:
            xn = _rmsnorm_pre(xs[tr], small[i]["ln_pre"])
        proj, casts, post_out = _in_proj(xn, weight(i, "w_in"), shapes[tr][1], side_casts=side_casts, post=post)
        keep(keys, casts)
        return proj, post_out

    def mixer_stage(tr, i, proj, hosting, pre=None):
        batch, seq = shapes[tr]
        keys, side_casts = side("out_proj", tr, i, hosting)
        a = _na_attention(proj, na_bias[seq // GRID_W], i, batch, seq)
        b = _retention(proj, small[i]["ret_ld"], small[i]["ret_gn"], *rope[seq], batch, seq)
        merged = _merge(a, b, weight(i, "w_proj_a"), weight(i, "w_proj_b"), proj)
        y, casts, xn_other = _out_proj(merged, weight(i, "w_out"), side_casts=side_casts, pre=pre)
        keep(keys, casts)
        return y, xn_other

    def ple_stage(tr, i, y, x1b, rs, hosting):
        keys, side_casts = side("ple", tr, i, hosting)
        t = xs[tr].shape[0]
        xs[tr], *casts = _ple(xs[tr], y, rs, small[i]["ln_post"], x1b, ps[tr][i].reshape(t, PLE_DIM),
                              weight(i, "w_ple_gate"), weight(i, "w_ple"), side_casts=side_casts)
        keep(keys, casts)

    y_b = None
    for i in range(depth):
        hosting = i + 1 < depth
        post_b = None if y_b is None else (xs[B], y_b, small[i - 1]["ln_post"])
        proj_a, post_out_b = in_stage(A, i, hosting, post_b)
        if y_b is not None:
            ple_stage(B, i - 1, y_b, *post_out_b, False)
        y_a, xn_b = mixer_stage(A, i, proj_a, hosting, pre=(xs[B], small[i]["ln_pre"]))
        proj_b, post_out_a = in_stage(B, i, False, (xs[A], y_a, small[i]["ln_post"]), xn=xn_b)
        ple_stage(A, i, y_a, *post_out_a, hosting)
        y_b, _ = mixer_stage(B, i, proj_b, False)
    x1b, rs = _post_norm(xs[B], y_b, small[depth - 1]["ln_post"])
    ple_stage(B, depth - 1, y_b, x1b, rs, False)
    return tuple(x.reshape(batch, seq, D_MODEL) for x, (batch, seq) in zip(xs, shapes))
```

```python
import functools

import numpy as np
import jax
import jax.numpy as jnp
from jax import lax
from jax.experimental import pallas as pl
from jax.experimental.pallas import tpu as pltpu

D_MODEL = 4096
GRID_W = 64
PLE_DIM = 256
NA_HEADS = 16
NA_HEAD_DIM = 128
NA_WIDTH = NA_HEADS * NA_HEAD_DIM
NA_WIN_ROWS = 8
NA_WIN_COLS = 16
RET_HEADS = 8
RET_QK_DIM = 128
RET_V_DIM = 256
RET_QK_WIDTH = RET_HEADS * RET_QK_DIM
RET_V_WIDTH = RET_HEADS * RET_V_DIM
RET_CHUNK = 128
ROPE_BASE = 10000.0
EPS = 1e-6

OFF_NA_Q = 0
OFF_NA_K = OFF_NA_Q + NA_WIDTH
OFF_NA_V = OFF_NA_K + NA_WIDTH
OFF_NA_G = OFF_NA_V + NA_WIDTH
OFF_R_Q = OFF_NA_G + NA_WIDTH
OFF_R_K = OFF_R_Q + RET_QK_WIDTH
OFF_R_V = OFF_R_K + RET_QK_WIDTH
OFF_R_G = OFF_R_V + RET_V_WIDTH
OFF_GATE_A = OFF_R_G + RET_V_WIDTH
OFF_GATE_B = OFF_GATE_A + D_MODEL
IN_WIDTH = OFF_GATE_B + D_MODEL

TILE_N = 1024
MAX_TILE_M = 1024
ROW_TILE = 512
V7X_VMEM_LIMIT_BYTES = 56 * 1024 * 1024
LANES = 128

F32 = jnp.float32
BF16 = jnp.bfloat16


def _params(semantics, vmem=V7X_VMEM_LIMIT_BYTES):
    return pltpu.CompilerParams(dimension_semantics=semantics, vmem_limit_bytes=vmem)


CAST_BLOCK_BYTES = 8 * 1024 * 1024


def _scaled_cast(w, scale, dtype):
    return (w if scale == 1.0 else w * scale).astype(dtype)


def _cast_kernel(w_ref, o_ref, *, scale):
    o_ref[...] = _scaled_cast(w_ref[...], scale, o_ref.dtype)


def _cast_layer_bf16(w, layer, scale=1.0):
    _, k, n = w.shape
    rb = max(r for r in range(16, k + 1, 16) if k % r == 0 and (r * n * 4 <= CAST_BLOCK_BYTES or r == 16))
    return pl.pallas_call(
        functools.partial(_cast_kernel, scale=scale),
        grid=(k // rb,),
        in_specs=[pl.BlockSpec((None, rb, n), lambda i: (layer, i, 0))],
        out_specs=pl.BlockSpec((rb, n), lambda i: (i, 0)),
        out_shape=jax.ShapeDtypeStruct((k, n), BF16),
        compiler_params=_params(("arbitrary",)),
        name="cast_bf16",
    )(w)


SUBLANES_BF16 = 16


SIDE_CAST_MAX_BYTES = 1024 * 1024


def _side_cast_grid_slab_fits(w, ni, nj):
    _, k, n = w.shape
    return (k % ni == 0 and n % nj == 0 and (k // ni) % SUBLANES_BF16 == 0 and (n // nj) % LANES == 0
            and (k // ni) * (n // nj) * 4 <= SIDE_CAST_MAX_BYTES)


def _side_cast_row_slab(w, n_steps):
    _, k, n = w.shape
    rows = [r for r in range(SUBLANES_BF16, k + 1, SUBLANES_BF16) if k % r == 0 and k // r <= n_steps]
    return min(rows) if rows and min(rows) * n * 4 <= SIDE_CAST_MAX_BYTES else None


def _side_cast_fits(w, ni, nj):
    return _side_cast_grid_slab_fits(w, ni, nj) or _side_cast_row_slab(w, ni * nj) is not None


def _side_cast_specs(side_casts, ni, nj):
    in_specs, out_specs, out_shapes = [], [], []
    for w, layer, _ in side_casts:
        _, k, n = w.shape
        if _side_cast_grid_slab_fits(w, ni, nj):
            blk = (k // ni, n // nj)
            in_specs.append(pl.BlockSpec((None,) + blk, lambda i, j, layer=layer: (layer, i, j)))
            out_specs.append(pl.BlockSpec(blk, lambda i, j: (i, j)))
        else:
            rows = _side_cast_row_slab(w, ni * nj)
            last = k // rows - 1
            in_specs.append(pl.BlockSpec(
                (None, rows, n), lambda i, j, layer=layer, last=last: (layer, jnp.minimum(i * nj + j, last), 0)))
            out_specs.append(pl.BlockSpec((rows, n), lambda i, j, last=last: (jnp.minimum(i * nj + j, last), 0)))
        out_shapes.append(jax.ShapeDtypeStruct((k, n), BF16))
    return in_specs, out_specs, out_shapes


def _split_side_refs(refs, n_out):
    n = (len(refs) - n_out) // 2
    return refs[:n], refs[n:n + n_out], refs[n + n_out:]


def _side_cast(src_refs, dst_refs, scales):
    for src, dst, scale in zip(src_refs, dst_refs, scales):
        dst[...] = _scaled_cast(src[...], scale, dst.dtype)


def _rmsnorm_rows(x_ref, g_ref, o_ref):
    x = x_ref[...]
    ms = jnp.mean(x * x, axis=-1, keepdims=True)
    xn = (x * lax.rsqrt(ms + EPS) * g_ref[...]).astype(o_ref.dtype)
    o_ref[...] = xn
    return xn


def _rmsnorm_kernel(x_ref, g_ref, o_ref):
    _rmsnorm_rows(x_ref, g_ref, o_ref)


def _rmsnorm_pre(x, g):
    t = x.shape[0]
    return pl.pallas_call(
        _rmsnorm_kernel,
        grid=(t // ROW_TILE,),
        in_specs=[pl.BlockSpec((ROW_TILE, D_MODEL), lambda i: (i, 0)),
                  pl.BlockSpec((1, D_MODEL), lambda i: (0, 0))],
        out_specs=pl.BlockSpec((ROW_TILE, D_MODEL), lambda i: (i, 0)),
        out_shape=jax.ShapeDtypeStruct((t, D_MODEL), BF16),
        compiler_params=_params(("arbitrary",)),
        name="rmsnorm_pre",
    )(x, g.reshape(1, D_MODEL))


_J_NA_K = OFF_NA_K // TILE_N
_J_NA_G = OFF_NA_G // TILE_N
_J_R_Q = OFF_R_Q // TILE_N
_J_R_K = OFF_R_K // TILE_N
_J_R_G = OFF_R_G // TILE_N
_J_GATE = OFF_GATE_A // TILE_N
assert NA_HEAD_DIM == RET_QK_DIM
QK_SCALE = NA_HEAD_DIM ** -0.5
LOG2E = float(np.log2(np.e))
NA_Q_SCALE = QK_SCALE * LOG2E


def _post_norm_rows(x_ref, y_ref, g_ref, ob_ref, rs_ref):
    y = y_ref[...].astype(F32)
    rs = lax.rsqrt(jnp.mean(y * y, axis=-1, keepdims=True) + EPS)
    x1b = (x_ref[...] + y * rs * g_ref[...]).astype(ob_ref.dtype)
    ob_ref[...] = x1b
    rs_ref[...] = jnp.broadcast_to(rs, rs_ref.shape)
    return x1b


def _post_kernel(x_ref, y_ref, g_ref, ob_ref, rs_ref):
    _post_norm_rows(x_ref, y_ref, g_ref, ob_ref, rs_ref)


def _hosted_slab_rows(t_post, n_steps):
    return min(r for r in range(SUBLANES_BF16, t_post + 1, SUBLANES_BF16)
               if t_post % r == 0 and t_post // r <= n_steps)


def _inproj_kernel(x_ref, w_ref, *refs, cast_scales, has_post):
    n_cast = len(cast_scales)
    n_in = n_cast + (3 if has_post else 0)
    cast_src, post_in = refs[:n_cast], refs[n_cast:n_in]
    o_ref, cast_dst, post_out = refs[n_in], refs[n_in + 1:n_in + 1 + n_cast], refs[n_in + 1 + n_cast:]
    j = pl.program_id(1)
    is_gate = j >= _J_GATE
    is_silu = ((j >= _J_NA_G) & (j < _J_R_Q)) | ((j >= _J_R_G) & (j < _J_GATE))
    scale = jnp.where(j < _J_NA_K, NA_Q_SCALE, jnp.where(j == _J_R_K, QK_SCALE, 1.0))

    def step(epilogue):
        _side_cast(cast_src, cast_dst, cast_scales)
        x = x_ref[...]
        for half in range(2):
            cols = slice(half * (TILE_N // 2), (half + 1) * (TILE_N // 2))
            if half == 1 and has_post:
                x1b = _post_norm_rows(*post_in, *post_out)
                never = pl.program_id(0) < 0
                x = jnp.concatenate([jnp.where(never, x1b, x[:x1b.shape[0]]), x[x1b.shape[0]:]], axis=0)
            y = jnp.dot(x, w_ref[:, cols], preferred_element_type=F32)
            o_ref[:, cols] = epilogue(y).astype(o_ref.dtype)

    pl.when(is_gate)(lambda: step(lambda y: 0.5 * jnp.tanh(0.5 * y) + 0.5))
    pl.when(is_silu)(lambda: step(lambda y: y * (0.5 * jnp.tanh(0.5 * y) + 0.5)))
    pl.when(jnp.logical_not(is_gate | is_silu))(lambda: step(lambda y: y * scale))


def _in_proj_grid(t, seq):
    tm = min(MAX_TILE_M, seq)
    return tm, (t // tm, IN_WIDTH // TILE_N)


def _in_proj(xn, w, seq, side_casts=(), post=None):
    t = xn.shape[0]
    tm, grid = _in_proj_grid(t, seq)
    tn = TILE_N
    side_in, side_out, side_shapes = _side_cast_specs(side_casts, *grid)
    post_args, post_in, post_out, post_shapes = (), [], [], []
    if post is not None:
        px, py, pg = post
        t_post = px.shape[0]
        rows = _hosted_slab_rows(t_post, grid[0] * grid[1])
        slab = lambda i, j: (jnp.minimum(i * grid[1] + j, t_post // rows - 1), 0)
        post_args = (px, py, pg.reshape(1, D_MODEL))
        post_in = [pl.BlockSpec((rows, D_MODEL), slab), pl.BlockSpec((rows, D_MODEL), slab),
                   pl.BlockSpec((1, D_MODEL), lambda i, j: (0, 0))]
        post_out = [pl.BlockSpec((rows, D_MODEL), slab), pl.BlockSpec((rows, LANES), slab)]
        post_shapes = [jax.ShapeDtypeStruct((t_post, D_MODEL), BF16), jax.ShapeDtypeStruct((t_post, LANES), F32)]
    outs = pl.pallas_call(
        functools.partial(_inproj_kernel, cast_scales=tuple(c[2] for c in side_casts), has_post=post is not None),
        grid=grid,
        in_specs=[pl.BlockSpec((tm, D_MODEL), lambda i, j: (i, 0)),
                  pl.BlockSpec((D_MODEL, tn), lambda i, j: (0, j))] + side_in + post_in,
        out_specs=[pl.BlockSpec((tm, tn), lambda i, j: (i, j))] + side_out + post_out,
        out_shape=[jax.ShapeDtypeStruct((t, IN_WIDTH), BF16)] + side_shapes + post_shapes,
        compiler_params=_params(("arbitrary", "arbitrary")),
        name="in_proj",
    )(xn, w, *(c[0] for c in side_casts), *post_args)
    n_cast = len(side_casts)
    return outs[0], list(outs[1:1 + n_cast]), (tuple(outs[1 + n_cast:]) if post is not None else None)


NA_PAIR_ROWS = 2
NA_PAIR_TOKENS = NA_PAIR_ROWS * GRID_W
NA_KEY_ROWS = NA_WIN_ROWS + NA_PAIR_ROWS
NA_KEY_CHUNKS = NA_KEY_ROWS // NA_PAIR_ROWS
NA_KEYS = NA_KEY_ROWS * GRID_W
NA_VARIANTS = 5
NA_QUAD_CHUNKS = NA_KEY_CHUNKS + 1
NA_SUM_ROWS = 16


def _na_kernel(q_ref, k_ref, v_ref, g_ref, bias_ref, o_ref, vt_ref, s_ref, p_ref, *, rows):
    n_pairs = rows // NA_PAIR_ROWS
    n_chunks = n_pairs
    ct = NA_PAIR_TOKENS
    nt = (((1,), (1,)), ((), ()))
    n_quads = n_pairs // 2

    for c in range(n_chunks):
        vt_ref[c] = v_ref[c * ct:(c + 1) * ct, :].astype(F32).T.astype(BF16)
    ones_rows = jnp.ones((NA_SUM_ROWS, NA_QUAD_CHUNKS * ct), BF16)
    zero_block = jnp.zeros((ct, ct), BF16)

    def pair_window(pr):
        c0 = min(max(pr - NA_WIN_ROWS // 4, 0), n_chunks - NA_KEY_CHUNKS)
        var = pr if pr < 2 else (pr - (n_pairs - 2) + 3 if pr >= n_pairs - 2 else 2)
        return c0, var

    quad_chunk0 = [min(max(2 * qd - NA_WIN_ROWS // 4, 0), n_chunks - NA_QUAD_CHUNKS) for qd in range(n_quads)]
    for qd in range(n_quads):
        k0, q0 = quad_chunk0[qd] * ct, 2 * qd * ct
        s_ref[qd] = lax.dot_general(k_ref[k0:k0 + NA_QUAD_CHUNKS * ct, :], q_ref[q0:q0 + 2 * ct, :], nt,
                                    preferred_element_type=F32)
    for qd in range(n_quads):
        for half in range(2):
            c0, var = pair_window(2 * qd + half)
            off = (c0 - quad_chunk0[qd]) * ct
            lanes = slice(half * ct, (half + 1) * ct)
            s = s_ref[qd, off:off + NA_KEYS, lanes] + bias_ref[var]
            m = jnp.max(s, axis=0, keepdims=True)
            p_ref[qd, off:off + NA_KEYS, lanes] = jnp.exp2(s - m).astype(BF16)
            rest = 0 if off else NA_KEYS
            p_ref[qd, rest:rest + ct, lanes] = zero_block
    for qd in range(n_quads):
        q0 = 2 * qd * ct
        vt = jnp.concatenate([vt_ref[quad_chunk0[qd] + c] for c in range(NA_QUAD_CHUNKS)], axis=1)
        ot = jnp.dot(jnp.concatenate([vt, ones_rows], axis=0), p_ref[qd], preferred_element_type=F32)
        ot = ot[:NA_HEAD_DIM] * (1.0 / ot[NA_HEAD_DIM:NA_HEAD_DIM + 1])
        for half in range(2):
            rows_q = slice(q0 + half * ct, q0 + (half + 1) * ct)
            o = ot[:, half * ct:(half + 1) * ct].T * g_ref[rows_q, :].astype(F32)
            o_ref[rows_q, :] = o.astype(o_ref.dtype)


def _na_bias_table(rpb, rows):
    n_pairs = rows // NA_PAIR_ROWS
    cols = np.arange(GRID_W)
    col_start = np.clip(cols - NA_WIN_COLS // 2, 0, GRID_W - NA_WIN_COLS)
    col_valid = (cols[None, :] >= col_start[:, None]) & (cols[None, :] < col_start[:, None] + NA_WIN_COLS)
    pad = GRID_W - NA_WIN_COLS
    rp = jnp.pad(rpb.astype(F32), ((0, 0), (0, 0), (pad, pad)))
    toep = jnp.stack([rp[:, :, GRID_W - 1 - qc: 2 * GRID_W - 1 - qc] for qc in range(GRID_W)], axis=2)
    toep = jnp.where(col_valid[None, None], toep, -jnp.inf)
    toep_t = toep.transpose(0, 1, 3, 2)
    masked = jnp.full((rpb.shape[0], GRID_W, GRID_W), -jnp.inf, F32)
    tables = []
    for pr in (0, 1, 2, n_pairs - 2, n_pairs - 1):
        base = int(np.clip(NA_PAIR_ROWS * pr - NA_WIN_ROWS // 2, 0, rows - NA_KEY_ROWS))
        slabs = []
        for w in range(NA_KEY_ROWS):
            per_row = []
            for i in range(NA_PAIR_ROWS):
                r_q = NA_PAIR_ROWS * pr + i
                rs = int(np.clip(r_q - NA_WIN_ROWS // 2, 0, rows - NA_WIN_ROWS))
                kr = base + w
                if rs <= kr < rs + NA_WIN_ROWS:
                    per_row.append(toep_t[:, kr - r_q + NA_WIN_ROWS - 1])
                else:
                    per_row.append(masked)
            slabs.append(jnp.concatenate(per_row, axis=-1))
        tables.append(jnp.concatenate(slabs, axis=1))
    return jnp.stack(tables, axis=1) * LOG2E


def _na_attention(proj, bias, layer, batch, seq):
    rows = seq // GRID_W
    n_pairs = rows // NA_PAIR_ROWS
    n_quads = n_pairs // 2
    assert rows % (2 * NA_PAIR_ROWS) == 0 and rows >= NA_KEY_ROWS + 2 * NA_PAIR_ROWS
    hd = NA_HEAD_DIM
    blk = lambda off: pl.BlockSpec((seq, hd), lambda b, h, off=off: (b, off // hd + h))
    return pl.pallas_call(
        functools.partial(_na_kernel, rows=rows),
        grid=(batch, NA_HEADS),
        in_specs=[blk(OFF_NA_Q), blk(OFF_NA_K), blk(OFF_NA_V), blk(OFF_NA_G),
                  pl.BlockSpec((None, NA_VARIANTS, NA_KEYS, NA_PAIR_TOKENS),
                               lambda b, h: (layer * NA_HEADS + h, 0, 0, 0))],
        out_specs=pl.BlockSpec((seq, hd), lambda b, h: (b, h)),
        out_shape=jax.ShapeDtypeStruct((batch * seq, NA_WIDTH), BF16),
        scratch_shapes=[pltpu.VMEM((n_pairs, hd, NA_PAIR_TOKENS), BF16),
                        pltpu.VMEM((n_quads, NA_QUAD_CHUNKS * NA_PAIR_TOKENS, 2 * NA_PAIR_TOKENS), F32),
                        pltpu.VMEM((n_quads, NA_QUAD_CHUNKS * NA_PAIR_TOKENS, 2 * NA_PAIR_TOKENS), BF16)],
        compiler_params=_params(("arbitrary", "arbitrary")),
        name="na_attention",
    )(proj, proj, proj, proj, bias)


RET_UNROLL = 32


def _ret_kernel(ld_ref, q_ref, k_ref, v_ref, g_ref, gain_ref, cos_ref, sin_ref, o_ref,
                qr_ref, kr_ref, kv_ref, st_ref, dsum_ref, qd_ref, kd_ref, *, n_chunks):
    c_len, dk = RET_CHUNK, RET_QK_DIM
    unroll = min(RET_UNROLL, n_chunks)
    h = pl.program_id(1)
    ldf = -jnp.abs(ld_ref[0, h])
    ldb = -jnp.abs(ld_ref[1, h])
    row = lax.broadcasted_iota(jnp.int32, (c_len, c_len), 0).astype(F32)
    col = lax.broadcasted_iota(jnp.int32, (c_len, c_len), 1).astype(F32)
    diff = row - col
    dsum_ref[...] = (jnp.where(diff >= 0, jnp.exp(ldf * jnp.maximum(diff, 0.0)), 0.0)
                     + jnp.where(diff <= 0, jnp.exp(ldb * jnp.maximum(-diff, 0.0)), 0.0))
    qd_ref[:, :dk] = jnp.exp(ldf * (row + 1.0))
    qd_ref[:, dk:] = jnp.exp(ldb * (c_len - row))
    kd_ref[:, :dk] = jnp.exp(ldf * (c_len - 1.0 - row))
    kd_ref[:, dk:] = jnp.exp(ldb * row)
    zero_row = jnp.zeros((1, RET_V_DIM), F32)
    cdf = jnp.exp(zero_row + ldf * c_len)
    cdb = jnp.exp(zero_row + ldb * c_len)
    chunk = lambda c: pl.ds(pl.multiple_of(c * c_len, c_len), c_len)

    def rotate(c, carry):
        rows = chunk(c)
        cos, sin = cos_ref[rows, :], sin_ref[rows, :]
        for src, dst in ((q_ref, qr_ref), (k_ref, kr_ref)):
            t = src[rows, :].astype(F32)
            dst[rows, :] = t * cos + pltpu.roll(t, dk // 2, axis=1) * sin
        return carry

    lax.fori_loop(0, n_chunks, rotate, 0, unroll=unroll)

    def chunk_kv(c, carry):
        rows = chunk(c)
        k = kr_ref[rows, :]
        k2 = (jnp.concatenate([k, k], axis=1) * kd_ref[...]).astype(BF16)
        kv_ref[c] = lax.dot_general(k2, v_ref[rows, :], (((0,), (0,)), ((), ())),
                                    preferred_element_type=F32)
        return carry

    lax.fori_loop(0, n_chunks, chunk_kv, 0, unroll=unroll)

    def scan_fwd(c, s):
        st_ref[c, :dk, :] = s.astype(BF16)
        return s * cdf + kv_ref[c, :dk, :]

    def scan_bwd(t, s):
        c = n_chunks - 1 - t
        st_ref[c, dk:, :] = s.astype(BF16)
        return s * cdb + kv_ref[c, dk:, :]

    zero_state = jnp.zeros((dk, RET_V_DIM), F32)
    lax.fori_loop(0, n_chunks, scan_fwd, zero_state, unroll=unroll)
    lax.fori_loop(0, n_chunks, scan_bwd, zero_state, unroll=unroll)

    def chunk_out(c, carry):
        rows = chunk(c)
        q = qr_ref[rows, :]
        v = v_ref[rows, :]
        sc = lax.dot_general(q.astype(BF16), kr_ref[rows, :].astype(BF16), (((1,), (1,)), ((), ())),
                             preferred_element_type=F32)
        o = jnp.dot((sc * dsum_ref[...]).astype(BF16), v, preferred_element_type=F32)
        q2 = (jnp.concatenate([q, q], axis=1) * qd_ref[...]).astype(BF16)
        o = o + jnp.dot(q2, st_ref[c], preferred_element_type=F32)
        o = o * lax.rsqrt(jnp.mean(o * o, axis=-1, keepdims=True) + EPS)
        o = o * gain_ref[...] * g_ref[rows, :].astype(F32)
        o_ref[rows, :] = o.astype(o_ref.dtype)
        return carry

    lax.fori_loop(0, n_chunks, chunk_out, 0, unroll=unroll)


def _retention(proj, ld, gain, cos_t, sin_t, batch, seq):
    n_chunks = seq // RET_CHUNK
    dk, dv, c_len = RET_QK_DIM, RET_V_DIM, RET_CHUNK
    assert n_chunks % min(RET_UNROLL, n_chunks) == 0
    qk_blk = lambda off: pl.BlockSpec((seq, dk), lambda b, h, off=off: (b, off // dk + h))
    v_blk = lambda off: pl.BlockSpec((seq, dv), lambda b, h, off=off: (b, off // dv + h))
    rope_blk = pl.BlockSpec((seq, dk), lambda b, h: (0, 0))
    return pl.pallas_call(
        functools.partial(_ret_kernel, n_chunks=n_chunks),
        grid=(batch, RET_HEADS),
        in_specs=[pl.BlockSpec(memory_space=pltpu.SMEM),
                  qk_blk(OFF_R_Q), qk_blk(OFF_R_K), v_blk(OFF_R_V), v_blk(OFF_R_G),
                  pl.BlockSpec((None, 1, dv), lambda b, h: (h, 0, 0)), rope_blk, rope_blk],
        out_specs=pl.BlockSpec((seq, dv), lambda b, h: (b, h)),
        out_shape=jax.ShapeDtypeStruct((batch * seq, RET_V_WIDTH), BF16),
        scratch_shapes=[pltpu.VMEM((seq, dk), F32),
                        pltpu.VMEM((seq, dk), F32),
                        pltpu.VMEM((n_chunks, 2 * dk, dv), F32),
                        pltpu.VMEM((n_chunks, 2 * dk, dv), BF16),
                        pltpu.VMEM((c_len, c_len), F32),
                        pltpu.VMEM((c_len, 2 * dk), F32),
                        pltpu.VMEM((c_len, 2 * dk), F32)],
        compiler_params=_params(("arbitrary", "arbitrary")),
        name="retention",
    )(ld, proj, proj, proj, proj, gain.astype(F32).reshape(RET_HEADS, 1, dv), cos_t, sin_t)


def _merge_kernel(a_ref, b_ref, wa_ref, wb_ref, ga_ref, gb_ref, o_ref):
    ya = jnp.dot(a_ref[...], wa_ref[...], preferred_element_type=F32).astype(BF16)
    yb = jnp.dot(b_ref[...], wb_ref[...], preferred_element_type=F32).astype(BF16)
    o_ref[...] = ga_ref[...] * ya + gb_ref[...] * yb


def _merge(a, b, wa, wb, proj):
    t = a.shape[0]
    tm = min(MAX_TILE_M, t)
    ja, jb = OFF_GATE_A // TILE_N, OFF_GATE_B // TILE_N
    return pl.pallas_call(
        _merge_kernel,
        grid=(t // tm, D_MODEL // TILE_N),
        in_specs=[pl.BlockSpec((tm, NA_WIDTH), lambda i, j: (i, 0)),
                  pl.BlockSpec((tm, RET_V_WIDTH), lambda i, j: (i, 0)),
                  pl.BlockSpec((NA_WIDTH, TILE_N), lambda i, j: (0, j)),
                  pl.BlockSpec((RET_V_WIDTH, TILE_N), lambda i, j: (0, j)),
                  pl.BlockSpec((tm, TILE_N), lambda i, j: (i, ja + j)),
                  pl.BlockSpec((tm, TILE_N), lambda i, j: (i, jb + j))],
        out_specs=pl.BlockSpec((tm, TILE_N), lambda i, j: (i, j)),
        out_shape=jax.ShapeDtypeStruct((t, D_MODEL), BF16),
        compiler_params=_params(("arbitrary", "arbitrary")),
        name="merge",
    )(a, b, wa, wb, proj, proj)


def _outproj_kernel(x_ref, w_ref, *refs, cast_scales, has_pre):
    n_cast = len(cast_scales)
    n_in = n_cast + (2 if has_pre else 0)
    cast_src, pre_in = refs[:n_cast], refs[n_cast:n_in]
    o_ref, cast_dst, pre_out = refs[n_in], refs[n_in + 1:n_in + 1 + n_cast], refs[n_in + 1 + n_cast:]
    _side_cast(cast_src, cast_dst, cast_scales)
    x = x_ref[...]
    for half in range(2):
        cols = slice(half * (TILE_N // 2), (half + 1) * (TILE_N // 2))
        if half == 1 and has_pre:
            xn = _rmsnorm_rows(*pre_in, *pre_out)
            never = pl.program_id(0) < 0
            x = jnp.concatenate([jnp.where(never, xn, x[:xn.shape[0]]), x[xn.shape[0]:]], axis=0)
        o_ref[:, cols] = jnp.dot(x, w_ref[:, cols], preferred_element_type=F32).astype(o_ref.dtype)


def _out_proj(merged, w, side_casts=(), pre=None):
    t = merged.shape[0]
    tm = min(MAX_TILE_M, t)
    grid = (t // tm, D_MODEL // TILE_N)
    side_in, side_out, side_shapes = _side_cast_specs(side_casts, *grid)
    pre_args, pre_in, pre_out, pre_shapes = (), [], [], []
    if pre is not None:
        px, pg = pre
        t_pre = px.shape[0]
        rows = _hosted_slab_rows(t_pre, grid[0] * grid[1])
        slab = lambda i, j: (jnp.minimum(i * grid[1] + j, t_pre // rows - 1), 0)
        pre_args = (px, pg.reshape(1, D_MODEL))
        pre_in = [pl.BlockSpec((rows, D_MODEL), slab), pl.BlockSpec((1, D_MODEL), lambda i, j: (0, 0))]
        pre_out = [pl.BlockSpec((rows, D_MODEL), slab)]
        pre_shapes = [jax.ShapeDtypeStruct((t_pre, D_MODEL), BF16)]
    outs = pl.pallas_call(
        functools.partial(_outproj_kernel, cast_scales=tuple(c[2] for c in side_casts), has_pre=pre is not None),
        grid=grid,
        in_specs=[pl.BlockSpec((tm, D_MODEL), lambda i, j: (i, 0)),
                  pl.BlockSpec((D_MODEL, TILE_N), lambda i, j: (0, j))] + side_in + pre_in,
        out_specs=[pl.BlockSpec((tm, TILE_N), lambda i, j: (i, j))] + side_out + pre_out,
        out_shape=[jax.ShapeDtypeStruct((t, D_MODEL), BF16)] + side_shapes + pre_shapes,
        compiler_params=_params(("arbitrary", "arbitrary")),
        name="out_proj",
    )(merged, w, *(c[0] for c in side_casts), *pre_args)
    n_cast = len(side_casts)
    return outs[0], list(outs[1:1 + n_cast]), (outs[1 + n_cast] if pre is not None else None)


def _post_norm(x, y, g):
    t = x.shape[0]
    row = pl.BlockSpec((ROW_TILE, D_MODEL), lambda i: (i, 0))
    return pl.pallas_call(
        _post_kernel,
        grid=(t // ROW_TILE,),
        in_specs=[row, row, pl.BlockSpec((1, D_MODEL), lambda i: (0, 0))],
        out_specs=[row, pl.BlockSpec((ROW_TILE, LANES), lambda i: (i, 0))],
        out_shape=[jax.ShapeDtypeStruct((t, D_MODEL), BF16), jax.ShapeDtypeStruct((t, LANES), F32)],
        compiler_params=_params(("arbitrary",)),
        name="post_norm",
    )(x, y, g.reshape(1, D_MODEL))


PLE_TILE_N = 512
PLE_WEIGHT_SCALE = 0.5


def _ple_kernel(xb_ref, wg_ref, p_ref, wp_ref, x_ref, y_ref, rs_ref, g_ref, *refs, cast_scales):
    cast_src, (o_ref,), cast_dst = _split_side_refs(refs, 1)
    _side_cast(cast_src, cast_dst, cast_scales)
    t = jnp.tanh(jnp.dot(xb_ref[...], wg_ref[...], preferred_element_type=F32))
    h = jnp.dot(p_ref[...].astype(BF16), wp_ref[...], preferred_element_type=F32)
    upd = h + h * t
    rs = rs_ref[...]
    for c in range(PLE_TILE_N // LANES):
        sl = slice(c * LANES, (c + 1) * LANES)
        x1 = x_ref[:, sl] + y_ref[:, sl].astype(F32) * rs * g_ref[:, sl]
        o_ref[:, sl] = x1 + upd[:, sl]


def _ple(x, y, rs, g, x1b, p, wg, wp, side_casts=()):
    t = x.shape[0]
    tm, tn = min(MAX_TILE_M, t), PLE_TILE_N
    grid = (t // tm, D_MODEL // tn)
    tile = pl.BlockSpec((tm, tn), lambda i, j: (i, j))
    side_in, side_out, side_shapes = _side_cast_specs(side_casts, *grid)
    return pl.pallas_call(
        functools.partial(_ple_kernel, cast_scales=tuple(c[2] for c in side_casts)),
        grid=grid,
        in_specs=[pl.BlockSpec((tm, D_MODEL), lambda i, j: (i, 0)),
                  pl.BlockSpec((D_MODEL, tn), lambda i, j: (0, j)),
                  pl.BlockSpec((tm, PLE_DIM), lambda i, j: (i, 0)),
                  pl.BlockSpec((PLE_DIM, tn), lambda i, j: (0, j)),
                  tile, tile,
                  pl.BlockSpec((tm, LANES), lambda i, j: (i, 0)),
                  pl.BlockSpec((1, tn), lambda i, j: (0, j))] + side_in,
        out_specs=[tile] + side_out,
        out_shape=[jax.ShapeDtypeStruct((t, D_MODEL), F32)] + side_shapes,
        compiler_params=_params(("arbitrary", "arbitrary")),
        name="ple",
    )(x1b, wg, p, wp, x, y, rs, g.reshape(1, D_MODEL), *(c[0] for c in side_casts))


def _rope_tables(seq):
    half = RET_QK_DIM // 2
    inv = ROPE_BASE ** (-jnp.arange(half, dtype=F32) / half)
    ang = jnp.arange(seq).astype(F32)[:, None] * inv[None, :]
    cos, sin = jnp.cos(ang), jnp.sin(ang)
    return jnp.concatenate([cos, cos], axis=-1), jnp.concatenate([-sin, sin], axis=-1)


SIDE_CAST_HOSTS = {"in_proj": ("w_in",), "out_proj": ("w_out",), "ple": ("w_ple_gate", "w_proj_a", "w_proj_b")}
FIRST_IN_PROJ_CASTS = ("w_proj_a", "w_proj_b", "w_out", "w_ple_gate")


def kernel(x_prompt, x_sample, p_prompt, p_sample, w_in, ln_pre, ln_post, na_rpb, ret_log_decay_fwd,
           ret_log_decay_bwd, ret_gn_gain, w_proj_a, w_proj_b, w_out, w_ple, w_ple_gate):
    depth = w_in.shape[0]
    stacked = {"w_in": w_in, "w_proj_a": w_proj_a, "w_proj_b": w_proj_b, "w_out": w_out,
               "w_ple": w_ple, "w_ple_gate": w_ple_gate}
    ps = [p_prompt, p_sample]
    shapes = [x.shape[:2] for x in (x_prompt, x_sample)]
    xs = [x.reshape(-1, D_MODEL) for x in (x_prompt, x_sample)]
    rope = {seq: _rope_tables(seq) for _, seq in shapes}
    rpb_all = na_rpb.reshape(depth * NA_HEADS, *na_rpb.shape[2:])
    na_bias = {seq // GRID_W: _na_bias_table(rpb_all, seq // GRID_W) for _, seq in shapes}
    small = [{"ln_pre": ln_pre[i].astype(F32),
              "ln_post": ln_post[i].astype(F32),
              "ret_ld": jnp.stack([ret_log_decay_fwd[i], ret_log_decay_bwd[i]]).astype(F32),
              "ret_gn": ret_gn_gain[i]} for i in range(depth)]
    scale_of = lambda name: PLE_WEIGHT_SCALE if name in ("w_ple", "w_ple_gate") else 1.0
    wts = [dict() for _ in range(depth)]
    A, B = 0, 1

    def side(host, tr, i, hosting):
        (batch, seq), t = shapes[tr], shapes[tr][0] * shapes[tr][1]
        grid = {"in_proj": _in_proj_grid(t, seq)[1],
                "out_proj": (t // min(MAX_TILE_M, t), D_MODEL // TILE_N),
                "ple": (t // min(MAX_TILE_M, t), D_MODEL // PLE_TILE_N)}[host]
        wanted = [(n, i + 1) for n in SIDE_CAST_HOSTS[host]] if hosting else []
        if host == "in_proj" and tr == A and i == 0:
            wanted += [(n, 0) for n in FIRST_IN_PROJ_CASTS]
        keys = [(n, layer) for n, layer in wanted if _side_cast_fits(stacked[n], *grid)]
        return keys, tuple((stacked[n], layer, scale_of(n)) for n, layer in keys)

    def keep(keys, casts):
        for (name, layer), w in zip(keys, casts):
            wts[layer][name] = w

    def weight(i, name):
        if name not in wts[i]:
            wts[i][name] = _cast_layer_bf16(stacked[name], i, scale_of(name))
        return wts[i][name]

    def in_stage(tr, i, hosting, post, xn=None):
        keys, side_casts = side("in_proj", tr, i, hosting)
        if xn is None:
            xn = _rmsnorm_pre(xs[tr], small[i]["ln_pre"])
        proj, casts, post_out = _in_proj(xn, weight(i, "w_in"), shapes[tr][1], side_casts=side_casts, post=post)
        keep(keys, casts)
        return proj, post_out

    def mixer_stage(tr, i, proj, hosting, pre=None):
        batch, seq = shapes[tr]
        keys, side_casts = side("out_proj", tr, i, hosting)
        a = _na_attention(proj, na_bias[seq // GRID_W], i, batch, seq)
        b = _retention(proj, small[i]["ret_ld"], small[i]["ret_gn"], *rope[seq], batch, seq)
        merged = _merge(a, b, weight(i, "w_proj_a"), weight(i, "w_proj_b"), proj)
        y, casts, xn_other = _out_proj(merged, weight(i, "w_out"), side_casts=side_casts, pre=pre)
        keep(keys, casts)
        return y, xn_other

    def ple_stage(tr, i, y, x1b, rs, hosting):
        keys, side_casts = side("ple", tr, i, hosting)
        t = xs[tr].shape[0]
        xs[tr], *casts = _ple(xs[tr], y, rs, small[i]["ln_post"], x1b, ps[tr][i].reshape(t, PLE_DIM),
                              weight(i, "w_ple_gate"), weight(i, "w_ple"), side_casts=side_casts)
        keep(keys, casts)

    y_b = None
    for i in range(depth):
        hosting = i + 1 < depth
        post_b = None if y_b is None else (xs[B], y_b, small[i - 1]["ln_post"])
        proj_a, post_out_b = in_stage(A, i, hosting, post_b)
        if y_b is not None:
            ple_stage(B, i - 1, y_b, *post_out_b, False)
        y_a, xn_b = mixer_stage(A, i, proj_a, hosting, pre=(xs[B], small[i]["ln_pre"]))
        proj_b, post_out_a = in_stage(B, i, False, (xs[A], y_a, small[i]["ln_post"]), xn=xn_b)
        ple_stage(A, i, y_a, *post_out_a, hosting)
        y_b, _ = mixer_stage(B, i, proj_b, False)
    x1b, rs = _post_norm(xs[B], y_b, small[depth - 1]["ln_post"])
    ple_stage(B, depth - 1, y_b, x1b, rs, False)
    return tuple(x.reshape(batch, seq, D_MODEL) for x, (batch, seq) in zip(xs, shapes))
```

```python
import functools

import numpy as np
import jax
import jax.numpy as jnp
from jax import lax
from jax.experimental import pallas as pl
from jax.experimental.pallas import tpu as pltpu

D_MODEL = 4096
GRID_W = 64
PLE_DIM = 256
NA_HEADS = 16
NA_HEAD_DIM = 128
NA_WIDTH = NA_HEADS * NA_HEAD_DIM
NA_WIN_ROWS = 8
NA_WIN_COLS = 16
RET_HEADS = 8
RET_QK_DIM = 128
RET_V_DIM = 256
RET_QK_WIDTH = RET_HEADS * RET_QK_DIM
RET_V_WIDTH = RET_HEADS * RET_V_DIM
RET_CHUNK = 128
ROPE_BASE = 10000.0
EPS = 1e-6

OFF_NA_Q = 0
OFF_NA_K = OFF_NA_Q + NA_WIDTH
OFF_NA_V = OFF_NA_K + NA_WIDTH
OFF_NA_G = OFF_NA_V + NA_WIDTH
OFF_R_Q = OFF_NA_G + NA_WIDTH
OFF_R_K = OFF_R_Q + RET_QK_WIDTH
OFF_R_V = OFF_R_K + RET_QK_WIDTH
OFF_R_G = OFF_R_V + RET_V_WIDTH
OFF_GATE_A = OFF_R_G + RET_V_WIDTH
OFF_GATE_B = OFF_GATE_A + D_MODEL
IN_WIDTH = OFF_GATE_B + D_MODEL

TILE_N = 1024
MAX_TILE_M = 1024
ROW_TILE = 512
V7X_VMEM_LIMIT_BYTES = 56 * 1024 * 1024
LANES = 128

F32 = jnp.float32
BF16 = jnp.bfloat16


def _params(semantics, vmem=V7X_VMEM_LIMIT_BYTES):
    return pltpu.CompilerParams(dimension_semantics=semantics, vmem_limit_bytes=vmem)


CAST_BLOCK_BYTES = 8 * 1024 * 1024


def _scaled_cast(w, scale, dtype):
    return (w if scale == 1.0 else w * scale).astype(dtype)


def _cast_kernel(w_ref, o_ref, *, scale):
    o_ref[...] = _scaled_cast(w_ref[...], scale, o_ref.dtype)


def _cast_layer_bf16(w, layer, scale=1.0):
    _, k, n = w.shape
    rb = max(r for r in range(16, k + 1, 16) if k % r == 0 and (r * n * 4 <= CAST_BLOCK_BYTES or r == 16))
    return pl.pallas_call(
        functools.partial(_cast_kernel, scale=scale),
        grid=(k // rb,),
        in_specs=[pl.BlockSpec((None, rb, n), lambda i: (layer, i, 0))],
        out_specs=pl.BlockSpec((rb, n), lambda i: (i, 0)),
        out_shape=jax.ShapeDtypeStruct((k, n), BF16),
        compiler_params=_params(("arbitrary",)),
        name="cast_bf16",
    )(w)


SUBLANES_BF16 = 16


SIDE_CAST_MAX_BYTES = 1024 * 1024


def _side_cast_grid_slab_fits(w, ni, nj):
    _, k, n = w.shape
    return (k % ni == 0 and n % nj == 0 and (k // ni) % SUBLANES_BF16 == 0 and (n // nj) % LANES == 0
            and (k // ni) * (n // nj) * 4 <= SIDE_CAST_MAX_BYTES)


def _side_cast_row_slab(w, n_steps):
    _, k, n = w.shape
    rows = [r for r in range(SUBLANES_BF16, k + 1, SUBLANES_BF16) if k % r == 0 and k // r <= n_steps]
    return min(rows) if rows and min(rows) * n * 4 <= SIDE_CAST_MAX_BYTES else None


def _side_cast_fits(w, ni, nj):
    return _side_cast_grid_slab_fits(w, ni, nj) or _side_cast_row_slab(w, ni * nj) is not None


def _side_cast_specs(side_casts, ni, nj):
    in_specs, out_specs, out_shapes = [], [], []
    for w, layer, _ in side_casts:
        _, k, n = w.shape
        if _side_cast_grid_slab_fits(w, ni, nj):
            blk = (k // ni, n // nj)
            in_specs.append(pl.BlockSpec((None,) + blk, lambda i, j, layer=layer: (layer, i, j)))
            out_specs.append(pl.BlockSpec(blk, lambda i, j: (i, j)))
        else:
            rows = _side_cast_row_slab(w, ni * nj)
            last = k // rows - 1
            in_specs.append(pl.BlockSpec(
                (None, rows, n), lambda i, j, layer=layer, last=last: (layer, jnp.minimum(i * nj + j, last), 0)))
            out_specs.append(pl.BlockSpec((rows, n), lambda i, j, last=last: (jnp.minimum(i * nj + j, last), 0)))
        out_shapes.append(jax.ShapeDtypeStruct((k, n), BF16))
    return in_specs, out_specs, out_shapes


def _split_side_refs(refs, n_out):
    n = (len(refs) - n_out) // 2
    return refs[:n], refs[n:n + n_out], refs[n + n_out:]


def _side_cast(src_refs, dst_refs, scales):
    for src, dst, scale in zip(src_refs, dst_refs, scales):
        dst[...] = _scaled_cast(src[...], scale, dst.dtype)


def _rmsnorm_rows(x_ref, g_ref, o_ref):
    x = x_ref[...]
    ms = jnp.mean(x * x, axis=-1, keepdims=True)
    xn = (x * lax.rsqrt(ms + EPS) * g_ref[...]).astype(o_ref.dtype)
    o_ref[...] = xn
    return xn


def _rmsnorm_kernel(x_ref, g_ref, o_ref):
    _rmsnorm_rows(x_ref, g_ref, o_ref)


def _rmsnorm_pre(x, g):
    t = x.shape[0]
    return pl.pallas_call(
        _rmsnorm_kernel,
        grid=(t // ROW_TILE,),
        in_specs=[pl.BlockSpec((ROW_TILE, D_MODEL), lambda i: (i, 0)),
                  pl.BlockSpec((1, D_MODEL), lambda i: (0, 0))],
        out_specs=pl.BlockSpec((ROW_TILE, D_MODEL), lambda i: (i, 0)),
        out_shape=jax.ShapeDtypeStruct((t, D_MODEL), BF16),
        compiler_params=_params(("arbitrary",)),
        name="rmsnorm_pre",
    )(x, g.reshape(1, D_MODEL))


_J_NA_K = OFF_NA_K // TILE_N
_J_NA_G = OFF_NA_G // TILE_N
_J_R_Q = OFF_R_Q // TILE_N
_J_R_K = OFF_R_K // TILE_N
_J_R_G = OFF_R_G // TILE_N
_J_GATE = OFF_GATE_A // TILE_N
assert NA_HEAD_DIM == RET_QK_DIM
QK_SCALE = NA_HEAD_DIM ** -0.5
LOG2E = float(np.log2(np.e))
NA_Q_SCALE = QK_SCALE * LOG2E


def _post_norm_rows(x_ref, y_ref, g_ref, ob_ref, rs_ref):
    y = y_ref[...].astype(F32)
    rs = lax.rsqrt(jnp.mean(y * y, axis=-1, keepdims=True) + EPS)
    x1b = (x_ref[...] + y * rs * g_ref[...]).astype(ob_ref.dtype)
    ob_ref[...] = x1b
    rs_ref[...] = jnp.broadcast_to(rs, rs_ref.shape)
    return x1b


def _post_kernel(x_ref, y_ref, g_ref, ob_ref, rs_ref):
    _post_norm_rows(x_ref, y_ref, g_ref, ob_ref, rs_ref)


def _hosted_slab_rows(t_post, n_steps):
    return min(r for r in range(SUBLANES_BF16, t_post + 1, SUBLANES_BF16)
               if t_post % r == 0 and t_post // r <= n_steps)


def _inproj_kernel(x_ref, w_ref, *refs, cast_scales, has_post):
    n_cast = len(cast_scales)
    n_in = n_cast + (3 if has_post else 0)
    cast_src, post_in = refs[:n_cast], refs[n_cast:n_in]
    o_ref, cast_dst, post_out = refs[n_in], refs[n_in + 1:n_in + 1 + n_cast], refs[n_in + 1 + n_cast:]
    j = pl.program_id(0)
    is_gate = j >= _J_GATE
    is_silu = ((j >= _J_NA_G) & (j < _J_R_Q)) | ((j >= _J_R_G) & (j < _J_GATE))
    scale = jnp.where(j < _J_NA_K, NA_Q_SCALE, jnp.where(j == _J_R_K, QK_SCALE, 1.0))

    def step(epilogue):
        _side_cast(cast_src, cast_dst, cast_scales)
        x = x_ref[...]
        for half in range(2):
            cols = slice(half * (TILE_N // 2), (half + 1) * (TILE_N // 2))
            if half == 1 and has_post:
                x1b = _post_norm_rows(*post_in, *post_out)
                never = pl.program_id(0) < 0
                x = jnp.concatenate([jnp.where(never, x1b, x[:x1b.shape[0]]), x[x1b.shape[0]:]], axis=0)
            y = jnp.dot(x, w_ref[:, cols], preferred_element_type=F32)
            o_ref[:, cols] = epilogue(y).astype(o_ref.dtype)

    pl.when(is_gate)(lambda: step(lambda y: 0.5 * jnp.tanh(0.5 * y) + 0.5))
    pl.when(is_silu)(lambda: step(lambda y: y * (0.5 * jnp.tanh(0.5 * y) + 0.5)))
    pl.when(jnp.logical_not(is_gate | is_silu))(lambda: step(lambda y: y * scale))


def _in_proj_grid(t, seq):
    tm = min(MAX_TILE_M, seq)
    return tm, (t // tm, IN_WIDTH // TILE_N)


def _in_proj(xn, w, seq, side_casts=(), post=None):
    t = xn.shape[0]
    tm, grid = _in_proj_grid(t, seq)
    tn = TILE_N
    side_in, side_out, side_shapes = _side_cast_specs(side_casts, *grid)
    post_args, post_in, post_out, post_shapes = (), [], [], []
    if post is not None:
        px, py, pg = post
        t_post = px.shape[0]
        rows = _hosted_slab_rows(t_post, grid[0] * grid[1])
        slab = lambda i, j: (jnp.minimum(i * grid[1] + j, t_post // rows - 1), 0)
        post_args = (px, py, pg.reshape(1, D_MODEL))
        post_in = [pl.BlockSpec((rows, D_MODEL), slab), pl.BlockSpec((rows, D_MODEL), slab),
                   pl.BlockSpec((1, D_MODEL), lambda i, j: (0, 0))]
        post_out = [pl.BlockSpec((rows, D_MODEL), slab), pl.BlockSpec((rows, LANES), slab)]
        post_shapes = [jax.ShapeDtypeStruct((t_post, D_MODEL), BF16), jax.ShapeDtypeStruct((t_post, LANES), F32)]
    col_major = lambda spec: pl.BlockSpec(spec.block_shape, lambda j, i, f=spec.index_map: f(i, j))
    in_specs = [pl.BlockSpec((tm, D_MODEL), lambda i, j: (i, 0)),
                pl.BlockSpec((D_MODEL, tn), lambda i, j: (0, j))] + side_in + post_in
    out_specs = [pl.BlockSpec((tm, tn), lambda i, j: (i, j))] + side_out + post_out
    outs = pl.pallas_call(
        functools.partial(_inproj_kernel, cast_scales=tuple(c[2] for c in side_casts), has_post=post is not None),
        grid=(grid[1], grid[0]),
        in_specs=[col_major(spec) for spec in in_specs],
        out_specs=[col_major(spec) for spec in out_specs],
        out_shape=[jax.ShapeDtypeStruct((t, IN_WIDTH), BF16)] + side_shapes + post_shapes,
        compiler_params=_params(("arbitrary", "arbitrary")),
        name="in_proj",
    )(xn, w, *(c[0] for c in side_casts), *post_args)
    n_cast = len(side_casts)
    return outs[0], list(outs[1:1 + n_cast]), (tuple(outs[1 + n_cast:]) if post is not None else None)


NA_PAIR_ROWS = 2
NA_PAIR_TOKENS = NA_PAIR_ROWS * GRID_W
NA_KEY_ROWS = NA_WIN_ROWS + NA_PAIR_ROWS
NA_KEY_CHUNKS = NA_KEY_ROWS // NA_PAIR_ROWS
NA_KEYS = NA_KEY_ROWS * GRID_W
NA_VARIANTS = 5
NA_QUAD_CHUNKS = NA_KEY_CHUNKS + 1
NA_SUM_ROWS = 16


def _na_kernel(q_ref, k_ref, v_ref, g_ref, bias_ref, o_ref, vt_ref, s_ref, p_ref, *, rows):
    n_pairs = rows // NA_PAIR_ROWS
    n_chunks = n_pairs
    ct = NA_PAIR_TOKENS
    nt = (((1,), (1,)), ((), ()))
    n_quads = n_pairs // 2

    for c in range(n_chunks):
        vt_ref[c] = v_ref[c * ct:(c + 1) * ct, :].astype(F32).T.astype(BF16)
    ones_rows = jnp.ones((NA_SUM_ROWS, NA_QUAD_CHUNKS * ct), BF16)
    zero_block = jnp.zeros((ct, ct), BF16)

    def pair_window(pr):
        c0 = min(max(pr - NA_WIN_ROWS // 4, 0), n_chunks - NA_KEY_CHUNKS)
        var = pr if pr < 2 else (pr - (n_pairs - 2) + 3 if pr >= n_pairs - 2 else 2)
        return c0, var

    quad_chunk0 = [min(max(2 * qd - NA_WIN_ROWS // 4, 0), n_chunks - NA_QUAD_CHUNKS) for qd in range(n_quads)]
    for qd in range(n_quads):
        k0, q0 = quad_chunk0[qd] * ct, 2 * qd * ct
        s_ref[qd] = lax.dot_general(k_ref[k0:k0 + NA_QUAD_CHUNKS * ct, :], q_ref[q0:q0 + 2 * ct, :], nt,
                                    preferred_element_type=F32)
    for qd in range(n_quads):
        for half in range(2):
            c0, var = pair_window(2 * qd + half)
            off = (c0 - quad_chunk0[qd]) * ct
            lanes = slice(half * ct, (half + 1) * ct)
            s = s_ref[qd, off:off + NA_KEYS, lanes] + bias_ref[var]
            m = jnp.max(s, axis=0, keepdims=True)
            p_ref[qd, off:off + NA_KEYS, lanes] = jnp.exp2(s - m).astype(BF16)
            rest = 0 if off else NA_KEYS
            p_ref[qd, rest:rest + ct, lanes] = zero_block
    for qd in range(n_quads):
        q0 = 2 * qd * ct
        vt = jnp.concatenate([vt_ref[quad_chunk0[qd] + c] for c in range(NA_QUAD_CHUNKS)], axis=1)
        ot = jnp.dot(jnp.concatenate([vt, ones_rows], axis=0), p_ref[qd], preferred_element_type=F32)
        ot = ot[:NA_HEAD_DIM] * (1.0 / ot[NA_HEAD_DIM:NA_HEAD_DIM + 1])
        for half in range(2):
            rows_q = slice(q0 + half * ct, q0 + (half + 1) * ct)
            o = ot[:, half * ct:(half + 1) * ct].T * g_ref[rows_q, :].astype(F32)
            o_ref[rows_q, :] = o.astype(o_ref.dtype)


def _na_bias_table(rpb, rows):
    n_pairs = rows // NA_PAIR_ROWS
    cols = np.arange(GRID_W)
    col_start = np.clip(cols - NA_WIN_COLS // 2, 0, GRID_W - NA_WIN_COLS)
    col_valid = (cols[None, :] >= col_start[:, None]) & (cols[None, :] < col_start[:, None] + NA_WIN_COLS)
    pad = GRID_W - NA_WIN_COLS
    rp = jnp.pad(rpb.astype(F32), ((0, 0), (0, 0), (pad, pad)))
    toep = jnp.stack([rp[:, :, GRID_W - 1 - qc: 2 * GRID_W - 1 - qc] for qc in range(GRID_W)], axis=2)
    toep = jnp.where(col_valid[None, None], toep, -jnp.inf)
    toep_t = toep.transpose(0, 1, 3, 2)
    masked = jnp.full((rpb.shape[0], GRID_W, GRID_W), -jnp.inf, F32)
    tables = []
    for pr in (0, 1, 2, n_pairs - 2, n_pairs - 1):
        base = int(np.clip(NA_PAIR_ROWS * pr - NA_WIN_ROWS // 2, 0, rows - NA_KEY_ROWS))
        slabs = []
        for w in range(NA_KEY_ROWS):
            per_row = []
            for i in range(NA_PAIR_ROWS):
                r_q = NA_PAIR_ROWS * pr + i
                rs = int(np.clip(r_q - NA_WIN_ROWS // 2, 0, rows - NA_WIN_ROWS))
                kr = base + w
                if rs <= kr < rs + NA_WIN_ROWS:
                    per_row.append(toep_t[:, kr - r_q + NA_WIN_ROWS - 1])
                else:
                    per_row.append(masked)
            slabs.append(jnp.concatenate(per_row, axis=-1))
        tables.append(jnp.concatenate(slabs, axis=1))
    return jnp.stack(tables, axis=1) * LOG2E


def _na_attention(proj, bias, layer, batch, seq):
    rows = seq // GRID_W
    n_pairs = rows // NA_PAIR_ROWS
    n_quads = n_pairs // 2
    assert rows % (2 * NA_PAIR_ROWS) == 0 and rows >= NA_KEY_ROWS + 2 * NA_PAIR_ROWS
    hd = NA_HEAD_DIM
    blk = lambda off: pl.BlockSpec((seq, hd), lambda b, h, off=off: (b, off // hd + h))
    return pl.pallas_call(
        functools.partial(_na_kernel, rows=rows),
        grid=(batch, NA_HEADS),
        in_specs=[blk(OFF_NA_Q), blk(OFF_NA_K), blk(OFF_NA_V), blk(OFF_NA_G),
                  pl.BlockSpec((None, NA_VARIANTS, NA_KEYS, NA_PAIR_TOKENS),
                               lambda b, h: (layer * NA_HEADS + h, 0, 0, 0))],
        out_specs=pl.BlockSpec((seq, hd), lambda b, h: (b, h)),
        out_shape=jax.ShapeDtypeStruct((batch * seq, NA_WIDTH), BF16),
        scratch_shapes=[pltpu.VMEM((n_pairs, hd, NA_PAIR_TOKENS), BF16),
                        pltpu.VMEM((n_quads, NA_QUAD_CHUNKS * NA_PAIR_TOKENS, 2 * NA_PAIR_TOKENS), F32),
                        pltpu.VMEM((n_quads, NA_QUAD_CHUNKS * NA_PAIR_TOKENS, 2 * NA_PAIR_TOKENS), BF16)],
        compiler_params=_params(("arbitrary", "arbitrary")),
        name="na_attention",
    )(proj, proj, proj, proj, bias)


RET_UNROLL = 32


def _ret_kernel(ld_ref, q_ref, k_ref, v_ref, g_ref, gain_ref, cos_ref, sin_ref, o_ref,
                qr_ref, kr_ref, kv_ref, st_ref, dsum_ref, qd_ref, kd_ref, *, n_chunks):
    c_len, dk = RET_CHUNK, RET_QK_DIM
    unroll = min(RET_UNROLL, n_chunks)
    h = pl.program_id(1)
    ldf = -jnp.abs(ld_ref[0, h])
    ldb = -jnp.abs(ld_ref[1, h])
    row = lax.broadcasted_iota(jnp.int32, (c_len, c_len), 0).astype(F32)
    col = lax.broadcasted_iota(jnp.int32, (c_len, c_len), 1).astype(F32)
    diff = row - col
    dsum_ref[...] = (jnp.where(diff >= 0, jnp.exp(ldf * jnp.maximum(diff, 0.0)), 0.0)
                     + jnp.where(diff <= 0, jnp.exp(ldb * jnp.maximum(-diff, 0.0)), 0.0))
    qd_ref[:, :dk] = jnp.exp(ldf * (row + 1.0))
    qd_ref[:, dk:] = jnp.exp(ldb * (c_len - row))
    kd_ref[:, :dk] = jnp.exp(ldf * (c_len - 1.0 - row))
    kd_ref[:, dk:] = jnp.exp(ldb * row)
    zero_row = jnp.zeros((1, RET_V_DIM), F32)
    cdf = jnp.exp(zero_row + ldf * c_len)
    cdb = jnp.exp(zero_row + ldb * c_len)
    chunk = lambda c: pl.ds(pl.multiple_of(c * c_len, c_len), c_len)

    def rotate(c, carry):
        rows = chunk(c)
        cos, sin = cos_ref[rows, :], sin_ref[rows, :]
        for src, dst in ((q_ref, qr_ref), (k_ref, kr_ref)):
            t = src[rows, :].astype(F32)
            dst[rows, :] = t * cos + pltpu.roll(t, dk // 2, axis=1) * sin
        return carry

    lax.fori_loop(0, n_chunks, rotate, 0, unroll=unroll)

    def chunk_kv(c, carry):
        rows = chunk(c)
        k = kr_ref[rows, :]
        k2 = (jnp.concatenate([k, k], axis=1) * kd_ref[...]).astype(BF16)
        kv_ref[c] = lax.dot_general(k2, v_ref[rows, :], (((0,), (0,)), ((), ())),
                                    preferred_element_type=F32)
        return carry

    lax.fori_loop(0, n_chunks, chunk_kv, 0, unroll=unroll)

    def scan_fwd(c, s):
        st_ref[c, :dk, :] = s.astype(BF16)
        return s * cdf + kv_ref[c, :dk, :]

    def scan_bwd(t, s):
        c = n_chunks - 1 - t
        st_ref[c, dk:, :] = s.astype(BF16)
        return s * cdb + kv_ref[c, dk:, :]

    zero_state = jnp.zeros((dk, RET_V_DIM), F32)
    lax.fori_loop(0, n_chunks, scan_fwd, zero_state, unroll=unroll)
    lax.fori_loop(0, n_chunks, scan_bwd, zero_state, unroll=unroll)

    def chunk_out(c, carry):
        rows = chunk(c)
        q = qr_ref[rows, :]
        v = v_ref[rows, :]
        sc = lax.dot_general(q.astype(BF16), kr_ref[rows, :].astype(BF16), (((1,), (1,)), ((), ())),
                             preferred_element_type=F32)
        o = jnp.dot((sc * dsum_ref[...]).astype(BF16), v, preferred_element_type=F32)
        q2 = (jnp.concatenate([q, q], axis=1) * qd_ref[...]).astype(BF16)
        o = o + jnp.dot(q2, st_ref[c], preferred_element_type=F32)
        o = o * lax.rsqrt(jnp.mean(o * o, axis=-1, keepdims=True) + EPS)
        o = o * gain_ref[...] * g_ref[rows, :].astype(F32)
        o_ref[rows, :] = o.astype(o_ref.dtype)
        return carry

    lax.fori_loop(0, n_chunks, chunk_out, 0, unroll=unroll)


def _retention(proj, ld, gain, cos_t, sin_t, batch, seq):
    n_chunks = seq // RET_CHUNK
    dk, dv, c_len = RET_QK_DIM, RET_V_DIM, RET_CHUNK
    assert n_chunks % min(RET_UNROLL, n_chunks) == 0
    qk_blk = lambda off: pl.BlockSpec((seq, dk), lambda b, h, off=off: (b, off // dk + h))
    v_blk = lambda off: pl.BlockSpec((seq, dv), lambda b, h, off=off: (b, off // dv + h))
    rope_blk = pl.BlockSpec((seq, dk), lambda b, h: (0, 0))
    return pl.pallas_call(
        functools.partial(_ret_kernel, n_chunks=n_chunks),
        grid=(batch, RET_HEADS),
        in_specs=[pl.BlockSpec(memory_space=pltpu.SMEM),
                  qk_blk(OFF_R_Q), qk_blk(OFF_R_K), v_blk(OFF_R_V), v_blk(OFF_R_G),
                  pl.BlockSpec((None, 1, dv), lambda b, h: (h, 0, 0)), rope_blk, rope_blk],
        out_specs=pl.BlockSpec((seq, dv), lambda b, h: (b, h)),
        out_shape=jax.ShapeDtypeStruct((batch * seq, RET_V_WIDTH), BF16),
        scratch_shapes=[pltpu.VMEM((seq, dk), F32),
                        pltpu.VMEM((seq, dk), F32),
                        pltpu.VMEM((n_chunks, 2 * dk, dv), F32),
                        pltpu.VMEM((n_chunks, 2 * dk, dv), BF16),
                        pltpu.VMEM((c_len, c_len), F32),
                        pltpu.VMEM((c_len, 2 * dk), F32),
                        pltpu.VMEM((c_len, 2 * dk), F32)],
        compiler_params=_params(("arbitrary", "arbitrary")),
        name="retention",
    )(ld, proj, proj, proj, proj, gain.astype(F32).reshape(RET_HEADS, 1, dv), cos_t, sin_t)


def _merge_kernel(a_ref, b_ref, wa_ref, wb_ref, ga_ref, gb_ref, o_ref):
    ya = jnp.dot(a_ref[...], wa_ref[...], preferred_element_type=F32).astype(BF16)
    yb = jnp.dot(b_ref[...], wb_ref[...], preferred_element_type=F32).astype(BF16)
    o_ref[...] = ga_ref[...] * ya + gb_ref[...] * yb


def _merge(a, b, wa, wb, proj):
    t = a.shape[0]
    tm = min(MAX_TILE_M, t)
    ja, jb = OFF_GATE_A // TILE_N, OFF_GATE_B // TILE_N
    return pl.pallas_call(
        _merge_kernel,
        grid=(t // tm, D_MODEL // TILE_N),
        in_specs=[pl.BlockSpec((tm, NA_WIDTH), lambda i, j: (i, 0)),
                  pl.BlockSpec((tm, RET_V_WIDTH), lambda i, j: (i, 0)),
                  pl.BlockSpec((NA_WIDTH, TILE_N), lambda i, j: (0, j)),
                  pl.BlockSpec((RET_V_WIDTH, TILE_N), lambda i, j: (0, j)),
                  pl.BlockSpec((tm, TILE_N), lambda i, j: (i, ja + j)),
                  pl.BlockSpec((tm, TILE_N), lambda i, j: (i, jb + j))],
        out_specs=pl.BlockSpec((tm, TILE_N), lambda i, j: (i, j)),
        out_shape=jax.ShapeDtypeStruct((t, D_MODEL), BF16),
        compiler_params=_params(("arbitrary", "arbitrary")),
        name="merge",
    )(a, b, wa, wb, proj, proj)


def _outproj_kernel(x_ref, w_ref, *refs, cast_scales, has_pre):
    n_cast = len(cast_scales)
    n_in = n_cast + (2 if has_pre else 0)
    cast_src, pre_in = refs[:n_cast], refs[n_cast:n_in]
    o_ref, cast_dst, pre_out = refs[n_in], refs[n_in + 1:n_in + 1 + n_cast], refs[n_in + 1 + n_cast:]
    _side_cast(cast_src, cast_dst, cast_scales)
    x = x_ref[...]
    for half in range(2):
        cols = slice(half * (TILE_N // 2), (half + 1) * (TILE_N // 2))
        if half == 1 and has_pre:
            xn = _rmsnorm_rows(*pre_in, *pre_out)
            never = pl.program_id(0) < 0
            x = jnp.concatenate([jnp.where(never, xn, x[:xn.shape[0]]), x[xn.shape[0]:]], axis=0)
        o_ref[:, cols] = jnp.dot(x, w_ref[:, cols], preferred_element_type=F32).astype(o_ref.dtype)


def _out_proj(merged, w, side_casts=(), pre=None):
    t = merged.shape[0]
    tm = min(MAX_TILE_M, t)
    grid = (t // tm, D_MODEL // TILE_N)
    side_in, side_out, side_shapes = _side_cast_specs(side_casts, *grid)
    pre_args, pre_in, pre_out, pre_shapes = (), [], [], []
    if pre is not None:
        px, pg = pre
        t_pre = px.shape[0]
        rows = _hosted_slab_rows(t_pre, grid[0] * grid[1])
        slab = lambda i, j: (jnp.minimum(i * grid[1] + j, t_pre // rows - 1), 0)
        pre_args = (px, pg.reshape(1, D_MODEL))
        pre_in = [pl.BlockSpec((rows, D_MODEL), slab), pl.BlockSpec((1, D_MODEL), lambda i, j: (0, 0))]
        pre_out = [pl.BlockSpec((rows, D_MODEL), slab)]
        pre_shapes = [jax.ShapeDtypeStruct((t_pre, D_MODEL), BF16)]
    outs = pl.pallas_call(
        functools.partial(_outproj_kernel, cast_scales=tuple(c[2] for c in side_casts), has_pre=pre is not None),
        grid=grid,
        in_specs=[pl.BlockSpec((tm, D_MODEL), lambda i, j: (i, 0)),
                  pl.BlockSpec((D_MODEL, TILE_N), lambda i, j: (0, j))] + side_in + pre_in,
        out_specs=[pl.BlockSpec((tm, TILE_N), lambda i, j: (i, j))] + side_out + pre_out,
        out_shape=[jax.ShapeDtypeStruct((t, D_MODEL), BF16)] + side_shapes + pre_shapes,
        compiler_params=_params(("arbitrary", "arbitrary")),
        name="out_proj",
    )(merged, w, *(c[0] for c in side_casts), *pre_args)
    n_cast = len(side_casts)
    return outs[0], list(outs[1:1 + n_cast]), (outs[1 + n_cast] if pre is not None else None)


def _post_norm(x, y, g):
    t = x.shape[0]
    row = pl.BlockSpec((ROW_TILE, D_MODEL), lambda i: (i, 0))
    return pl.pallas_call(
        _post_kernel,
        grid=(t // ROW_TILE,),
        in_specs=[row, row, pl.BlockSpec((1, D_MODEL), lambda i: (0, 0))],
        out_specs=[row, pl.BlockSpec((ROW_TILE, LANES), lambda i: (i, 0))],
        out_shape=[jax.ShapeDtypeStruct((t, D_MODEL), BF16), jax.ShapeDtypeStruct((t, LANES), F32)],
        compiler_params=_params(("arbitrary",)),
        name="post_norm",
    )(x, y, g.reshape(1, D_MODEL))


PLE_TILE_N = 512
PLE_WEIGHT_SCALE = 0.5


def _ple_kernel(xb_ref, wg_ref, p_ref, wp_ref, x_ref, y_ref, rs_ref, g_ref, *refs, cast_scales):
    cast_src, (o_ref,), cast_dst = _split_side_refs(refs, 1)
    _side_cast(cast_src, cast_dst, cast_scales)
    t = jnp.tanh(jnp.dot(xb_ref[...], wg_ref[...], preferred_element_type=F32))
    h = jnp.dot(p_ref[...].astype(BF16), wp_ref[...], preferred_element_type=F32)
    upd = h + h * t
    rs = rs_ref[...]
    for c in range(PLE_TILE_N // LANES):
        sl = slice(c * LANES, (c + 1) * LANES)
        x1 = x_ref[:, sl] + y_ref[:, sl].astype(F32) * rs * g_ref[:, sl]
        o_ref[:, sl] = x1 + upd[:, sl]


def _ple(x, y, rs, g, x1b, p, wg, wp, side_casts=()):
    t = x.shape[0]
    tm, tn = min(MAX_TILE_M, t), PLE_TILE_N
    grid = (t // tm, D_MODEL // tn)
    tile = pl.BlockSpec((tm, tn), lambda i, j: (i, j))
    side_in, side_out, side_shapes = _side_cast_specs(side_casts, *grid)
    return pl.pallas_call(
        functools.partial(_ple_kernel, cast_scales=tuple(c[2] for c in side_casts)),
        grid=grid,
        in_specs=[pl.BlockSpec((tm, D_MODEL), lambda i, j: (i, 0)),
                  pl.BlockSpec((D_MODEL, tn), lambda i, j: (0, j)),
                  pl.BlockSpec((tm, PLE_DIM), lambda i, j: (i, 0)),
                  pl.BlockSpec((PLE_DIM, tn), lambda i, j: (0, j)),
                  tile, tile,
                  pl.BlockSpec((tm, LANES), lambda i, j: (i, 0)),
                  pl.BlockSpec((1, tn), lambda i, j: (0, j))] + side_in,
        out_specs=[tile] + side_out,
        out_shape=[jax.ShapeDtypeStruct((t, D_MODEL), F32)] + side_shapes,
        compiler_params=_params(("arbitrary", "arbitrary")),
        name="ple",
    )(x1b, wg, p, wp, x, y, rs, g.reshape(1, D_MODEL), *(c[0] for c in side_casts))


def _rope_tables(seq):
    half = RET_QK_DIM // 2
    inv = ROPE_BASE ** (-jnp.arange(half, dtype=F32) / half)
    ang = jnp.arange(seq).astype(F32)[:, None] * inv[None, :]
    cos, sin = jnp.cos(ang), jnp.sin(ang)
    return jnp.concatenate([cos, cos], axis=-1), jnp.concatenate([-sin, sin], axis=-1)


SIDE_CAST_HOSTS = {"in_proj": ("w_in",), "out_proj": ("w_out",), "ple": ("w_ple_gate", "w_proj_a", "w_proj_b")}
FIRST_IN_PROJ_CASTS = ("w_proj_a", "w_proj_b", "w_out", "w_ple_gate")


def kernel(x_prompt, x_sample, p_prompt, p_sample, w_in, ln_pre, ln_post, na_rpb, ret_log_decay_fwd,
           ret_log_decay_bwd, ret_gn_gain, w_proj_a, w_proj_b, w_out, w_ple, w_ple_gate):
    depth = w_in.shape[0]
    stacked = {"w_in": w_in, "w_proj_a": w_proj_a, "w_proj_b": w_proj_b, "w_out": w_out,
               "w_ple": w_ple, "w_ple_gate": w_ple_gate}
    ps = [p_prompt, p_sample]
    shapes = [x.shape[:2] for x in (x_prompt, x_sample)]
    xs = [x.reshape(-1, D_MODEL) for x in (x_prompt, x_sample)]
    rope = {seq: _rope_tables(seq) for _, seq in shapes}
    rpb_all = na_rpb.reshape(depth * NA_HEADS, *na_rpb.shape[2:])
    na_bias = {seq // GRID_W: _na_bias_table(rpb_all, seq // GRID_W) for _, seq in shapes}
    small = [{"ln_pre": ln_pre[i].astype(F32),
              "ln_post": ln_post[i].astype(F32),
              "ret_ld": jnp.stack([ret_log_decay_fwd[i], ret_log_decay_bwd[i]]).astype(F32),
              "ret_gn": ret_gn_gain[i]} for i in range(depth)]
    scale_of = lambda name: PLE_WEIGHT_SCALE if name in ("w_ple", "w_ple_gate") else 1.0
    wts = [dict() for _ in range(depth)]
    A, B = 0, 1

    def side(host, tr, i, hosting):
        (batch, seq), t = shapes[tr], shapes[tr][0] * shapes[tr][1]
        grid = {"in_proj": _in_proj_grid(t, seq)[1],
                "out_proj": (t // min(MAX_TILE_M, t), D_MODEL // TILE_N),
                "ple": (t // min(MAX_TILE_M, t), D_MODEL // PLE_TILE_N)}[host]
        wanted = [(n, i + 1) for n in SIDE_CAST_HOSTS[host]] if hosting else []
        if host == "in_proj" and tr == A and i == 0:
            wanted += [(n, 0) for n in FIRST_IN_PROJ_CASTS]
        keys = [(n, layer) for n, layer in wanted if _side_cast_fits(stacked[n], *grid)]
        return keys, tuple((stacked[n], layer, scale_of(n)) for n, layer in keys)

    def keep(keys, casts):
        for (name, layer), w in zip(keys, casts):
            wts[layer][name] = w

    def weight(i, name):
        if name not in wts[i]:
            wts[i][name] = _cast_layer_bf16(stacked[name], i, scale_of(name))
        return wts[i][name]

    def in_stage(tr, i, hosting, post, xn=None):
        keys, side_casts = side("in_proj", tr, i, hosting)
        if xn is None:
            xn = _rmsnorm_pre(xs[tr], small[i]["ln_pre"])
        proj, casts, post_out = _in_proj(xn, weight(i, "w_in"), shapes[tr][1], side_casts=side_casts, post=post)
        keep(keys, casts)
        return proj, post_out

    def mixer_stage(tr, i, proj, hosting, pre=None):
        batch, seq = shapes[tr]
        keys, side_casts = side("out_proj", tr, i, hosting)
        a = _na_attention(proj, na_bias[seq // GRID_W], i, batch, seq)
        b = _retention(proj, small[i]["ret_ld"], small[i]["ret_gn"], *rope[seq], batch, seq)
        merged = _merge(a, b, weight(i, "w_proj_a"), weight(i, "w_proj_b"), proj)
        y, casts, xn_other = _out_proj(merged, weight(i, "w_out"), side_casts=side_casts, pre=pre)
        keep(keys, casts)
        return y, xn_other

    def ple_stage(tr, i, y, x1b, rs, hosting):
        keys, side_casts = side("ple", tr, i, hosting)
        t = xs[tr].shape[0]
        xs[tr], *casts = _ple(xs[tr], y, rs, small[i]["ln_post"], x1b, ps[tr][i].reshape(t, PLE_DIM),
                              weight(i, "w_ple_gate"), weight(i, "w_ple"), side_casts=side_casts)
        keep(keys, casts)

    y_b = None
    for i in range(depth):
        hosting = i + 1 < depth
        post_b = None if y_b is None else (xs[B], y_b, small[i - 1]["ln_post"])
        proj_a, post_out_b = in_stage(A, i, hosting, post_b)
        if y_b is not None:
            ple_stage(B, i - 1, y_b, *post_out_b, False)
        y_a, xn_b = mixer_stage(A, i, proj_a, hosting, pre=(xs[B], small[i]["ln_pre"]))
        proj_b, post_out_a = in_stage(B, i, False, (xs[A], y_a, small[i]["ln_post"]), xn=xn_b)
        ple_stage(A, i, y_a, *post_out_a, hosting)
        y_b, _ = mixer_stage(B, i, proj_b, False)
    x1b, rs = _post_norm(xs[B], y_b, small[depth - 1]["ln_post"])
    ple_stage(B, depth - 1, y_b, x1b, rs, False)
    return tuple(x.reshape(batch, seq, D_MODEL) for x, (batch, seq) in zip(xs, shapes))
```
